```python
import math
import jax
import jax.numpy as jnp
from jax import lax
import numpy as np

D_MODEL = 1024
BATCH = 2
SEQ = 8192
DEPTH = 2

GRID_W = 64
CTX_LEN = 256

N_MIXERS = 2
N_SSM_LAYERS = (DEPTH + 1) // 2
N_ATT_LAYERS = DEPTH // 2
N_MOD = 6

S5_GROUP = 16
S5_GROUPS = D_MODEL // S5_GROUP
S5_STATE = 64
S5_DT_MIN = 1e-3
S5_DT_MAX = 1e-1

MLA_HEADS = 8
MLA_NOPE = 128
MLA_ROPE = 64
MLA_V = 128
MLA_Q_LORA = 384
MLA_KV_LORA = 256
ROPE_BASE = 10000.0
Q_BLOCK = 128

PEER_HEADS = 8
PEER_NKEYS = 128
PEER_EXPERTS = PEER_NKEYS * PEER_NKEYS
PEER_DK = 128
PEER_TOPK = 16
PEER_TOKEN_BLOCK = 128

ALPHA = (2 * DEPTH) ** 0.25
BETA = (8 * DEPTH) ** -0.25
LN_EPS = 1e-5
RMS_EPS = 1e-6

kernel_name = 'hybrid_s5_mla_peer_diffusion_block'


def layer_norm(x, g, b):
    xf = x.astype(jnp.float32)
    mu = jnp.mean(xf, axis=-1, keepdims=True)
    var = jnp.mean(jnp.square(xf - mu), axis=-1, keepdims=True)
    return (xf - mu) * lax.rsqrt(var + LN_EPS) * g + b


def rms_norm(x, g):
    xf = x.astype(jnp.float32)
    return xf * lax.rsqrt(jnp.mean(jnp.square(xf), axis=-1, keepdims=True) + RMS_EPS) * g


def axial_rope_angles(n_tokens):
    rows = n_tokens // GRID_W
    r, col = jnp.meshgrid(jnp.arange(rows, dtype=jnp.float32),
                          jnp.arange(GRID_W, dtype=jnp.float32), indexing='ij')
    n_freq = MLA_ROPE // 4
    inv = ROPE_BASE ** (-jnp.arange(n_freq, dtype=jnp.float32) / n_freq)
    ang = jnp.concatenate([r.reshape(-1, 1) * inv, col.reshape(-1, 1) * inv], axis=-1)
    return jnp.cos(ang), jnp.sin(ang)


def apply_rope(x, cos, sin):
    half = x.shape[-1] // 2
    x1, x2 = x[..., :half], x[..., half:]
    return jnp.concatenate([x1 * cos - x2 * sin, x2 * cos + x1 * sin], axis=-1)


def s5_discretize(a_re, a_im, log_dt):
    a_re = a_re.astype(jnp.float32)
    a_im = a_im.astype(jnp.float32)
    dt = jnp.exp(log_dt.astype(jnp.float32))[:, None]
    mag = jnp.exp(a_re * dt)
    ab_re = mag * jnp.cos(a_im * dt)
    ab_im = mag * jnp.sin(a_im * dt)
    nr, ni = ab_re - 1.0, ab_im
    den = a_re * a_re + a_im * a_im
    f_re = (nr * a_re + ni * a_im) / den
    f_im = (ni * a_re - nr * a_im) / den
    return ab_re, ab_im, f_re, f_im


def _complex_affine_combine(e1, e2):
    a1r, a1i, b1r, b1i = e1
    a2r, a2i, b2r, b2i = e2
    return (a2r * a1r - a2i * a1i,
            a2r * a1i + a2i * a1r,
            a2r * b1r - a2i * b1i + b2r,
            a2r * b1i + a2i * b1r + b2i)


def complex_linear_scan(ab_re, ab_im, bu_re, bu_im, h0, reverse):
    if h0 is not None:
        first = -1 if reverse else 0
        h0_re, h0_im = h0
        bu_re = bu_re.at[first].add(ab_re * h0_re - ab_im * h0_im)
        bu_im = bu_im.at[first].add(ab_re * h0_im + ab_im * h0_re)
    n = bu_re.shape[0]
    a_re = jnp.broadcast_to(ab_re, (n, 1) + ab_re.shape)
    a_im = jnp.broadcast_to(ab_im, (n, 1) + ab_im.shape)
    _, _, h_re, h_im = lax.associative_scan(_complex_affine_combine, (a_re, a_im, bu_re, bu_im),
                                            reverse=reverse, axis=0)
    return h_re, h_im


def s5_direction(u_lat, u_ctx, a_re, a_im, log_dt, b_re, b_im, c_re, c_im, reverse, need_ctx_out):
    ab_re, ab_im, f_re, f_im = s5_discretize(a_re, a_im, log_dt)
    b_re = b_re.astype(jnp.float32)
    b_im = b_im.astype(jnp.float32)
    c_re = c_re.astype(jnp.float32)
    c_im = c_im.astype(jnp.float32)

    def drive(u):
        bu_re = jnp.einsum('lbgc,gpc->lbgp', u, b_re)
        bu_im = jnp.einsum('lbgc,gpc->lbgp', u, b_im)
        return f_re * bu_re - f_im * bu_im, f_re * bu_im + f_im * bu_re

    def readout(h_re, h_im):
        return (jnp.einsum('lbgp,gcp->lbgc', h_re, c_re)
                - jnp.einsum('lbgp,gcp->lbgc', h_im, c_im))

    hc_re, hc_im = complex_linear_scan(ab_re, ab_im, *drive(u_ctx), None, reverse)
    last = 0 if reverse else -1
    hl_re, hl_im = complex_linear_scan(ab_re, ab_im, *drive(u_lat),
                                       (hc_re[last], hc_im[last]), reverse)
    y_ctx = readout(hc_re, hc_im) if need_ctx_out else None
    return readout(hl_re, hl_im), y_ctx


def s5_output(y, w_glu, w_o):
    y = jax.nn.gelu(y, approximate=False)
    return (y * jax.nn.sigmoid(y @ w_glu)) @ w_o


def s5_mixer(h_lat, h_ctx, a_re, a_im, log_dt, b_re, b_im, c_re, c_im, d, w_glu, w_o, need_ctx_out):
    def to_groups(h):
        bsz, n, _ = h.shape
        return h.astype(jnp.float32).transpose(1, 0, 2).reshape(n, bsz, S5_GROUPS, S5_GROUP)

    def from_groups(y):
        n, bsz = y.shape[:2]
        return y.reshape(n, bsz, D_MODEL).transpose(1, 0, 2)

    u_lat, u_ctx = to_groups(h_lat), to_groups(h_ctx)
    yf_l, yf_c = s5_direction(u_lat, u_ctx, a_re[0], a_im[0], log_dt[0], b_re[0], b_im[0],
                              c_re[0], c_im[0], False, need_ctx_out)
    yb_l, yb_c = s5_direction(u_lat, u_ctx, a_re[1], a_im[1], log_dt[1], b_re[1], b_im[1],
                              c_re[1], c_im[1], True, need_ctx_out)
    o_lat = s5_output(from_groups(yf_l + yb_l) + d * h_lat, w_glu, w_o)
    o_ctx = None
    if need_ctx_out:
        o_ctx = s5_output(from_groups(yf_c + yb_c) + d * h_ctx, w_glu, w_o)
    return o_lat, o_ctx


def mla_queries(h, w_dq, q_norm, w_uq, rope):
    bsz, n, _ = h.shape
    q = (rms_norm(h @ w_dq, q_norm) @ w_uq).reshape(bsz, n, MLA_HEADS, MLA_NOPE + MLA_ROPE)
    q_nope, q_rope = q[..., :MLA_NOPE], q[..., MLA_NOPE:]
    if rope is not None:
        cos, sin = rope
        q_rope = apply_rope(q_rope, cos[None, :, None, :], sin[None, :, None, :])
    return q_nope, q_rope


def mla_keys_values(h, w_dkv, kv_norm, w_ukv, rope):
    bsz, n, _ = h.shape
    kv = h @ w_dkv
    c_kv = rms_norm(kv[..., :MLA_KV_LORA], kv_norm)
    k_rope = kv[..., MLA_KV_LORA:]
    if rope is not None:
        cos, sin = rope
        k_rope = apply_rope(k_rope, cos[None], sin[None])
    kvu = (c_kv @ w_ukv).reshape(bsz, n, MLA_HEADS, MLA_NOPE + MLA_V)
    return kvu[..., :MLA_NOPE], k_rope, kvu[..., MLA_NOPE:]


def mla_attend(q_nope, q_rope, k_nope, k_rope, v):
    scale = (MLA_NOPE + MLA_ROPE) ** -0.5
    s = (jnp.einsum('bqhd,bkhd->bhqk', q_nope, k_nope)
         + jnp.einsum('bqhr,bkr->bhqk', q_rope, k_rope))
    p = jax.nn.softmax(s.astype(jnp.float32) * scale, axis=-1).astype(v.dtype)
    return jnp.einsum('bhqk,bkhd->bqhd', p, v)


def mla_attend_blocked(q_nope, q_rope, k_nope, k_rope, v):
    bsz, n = q_nope.shape[:2]
    nb = n // Q_BLOCK
    qn = q_nope.reshape(bsz, nb, Q_BLOCK, MLA_HEADS, MLA_NOPE).transpose(1, 0, 2, 3, 4)
    qr = q_rope.reshape(bsz, nb, Q_BLOCK, MLA_HEADS, MLA_ROPE).transpose(1, 0, 2, 3, 4)
    out = lax.map(lambda qs: mla_attend(qs[0], qs[1], k_nope, k_rope, v), (qn, qr))
    return out.transpose(1, 0, 2, 3, 4).reshape(bsz, n, MLA_HEADS, MLA_V)


def mla_mixer(h_lat, h_ctx, w_dq, q_norm, w_uq, w_dkv, kv_norm, w_ukv, w_o, need_ctx_out):
    bsz, n_lat, _ = h_lat.shape
    rope = axial_rope_angles(n_lat)
    kn_c, kr_c, v_c = mla_keys_values(h_ctx, w_dkv, kv_norm, w_ukv, None)
    kn_l, kr_l, v_l = mla_keys_values(h_lat, w_dkv, kv_norm, w_ukv, rope)
    qn_l, qr_l = mla_queries(h_lat, w_dq, q_norm, w_uq, rope)
    k_nope = jnp.concatenate([kn_l, kn_c], axis=1)
    k_rope = jnp.concatenate([kr_l, kr_c], axis=1)
    v = jnp.concatenate([v_l, v_c], axis=1)
    o_lat = mla_attend_blocked(qn_l, qr_l, k_nope, k_rope, v).reshape(bsz, n_lat, MLA_HEADS * MLA_V) @ w_o
    o_ctx = None
    if need_ctx_out:
        qn_c, qr_c = mla_queries(h_ctx, w_dq, q_norm, w_uq, None)
        o_ctx = mla_attend(qn_c, qr_c, kn_c, kr_c, v_c).reshape(bsz, h_ctx.shape[1], MLA_HEADS * MLA_V) @ w_o
    return o_lat, o_ctx


def peer_route(t, w_q, keys):
    n_tok = t.shape[0]
    q = (t @ w_q).reshape(n_tok, PEER_HEADS, 2, PEER_DK // 2)
    s = jnp.einsum('thsd,hsnd->thsn', q, keys).astype(jnp.float32)
    sv, si = lax.top_k(s, PEER_TOPK)
    cand = sv[:, :, 0, :, None] + sv[:, :, 1, None, :]
    cv, ci = lax.top_k(cand.reshape(n_tok, PEER_HEADS, PEER_TOPK * PEER_TOPK), PEER_TOPK)
    i1 = jnp.take_along_axis(si[:, :, 0], ci // PEER_TOPK, axis=-1)
    i2 = jnp.take_along_axis(si[:, :, 1], ci % PEER_TOPK, axis=-1)
    e = i1 * PEER_NKEYS + i2
    g = jax.nn.softmax(cv, axis=-1)
    return e.reshape(n_tok, -1), g.reshape(n_tok, -1)


def peer_channel(h, w_q, keys, u_tab, v_tab):
    bsz, n, d = h.shape
    t = h.reshape(bsz * n, d)
    idx, gate = peer_route(t, w_q, keys)
    nb = t.shape[0] // PEER_TOKEN_BLOCK

    def block(args):
        hb, ib, gb = args
        act = jax.nn.gelu(jnp.einsum('td,tkd->tk', hb, u_tab[ib]), approximate=False) * gb
        return jnp.einsum('tk,tkd->td', act, v_tab[ib])

    out = lax.map(block, (t.reshape(nb, PEER_TOKEN_BLOCK, d),
                          idx.reshape(nb, PEER_TOKEN_BLOCK, -1),
                          gate.reshape(nb, PEER_TOKEN_BLOCK, -1)))
    return out.reshape(bsz, n, d)


def setup_inputs(seed: int = 0) -> dict:
    key = jax.random.key(seed)
    ks = jax.random.split(key, 32)
    f32 = jnp.float32

    def nrm(k, shape, std):
        return jax.random.normal(k, shape, f32) * std

    D = D_MODEL
    NS, NA = N_SSM_LAYERS, N_ATT_LAYERS
    G, P, GC = S5_GROUPS, S5_STATE, S5_GROUP
    n_idx = jnp.arange(P, dtype=f32)
    return {
        'x': nrm(ks[0], (BATCH, SEQ, D), 1.0),
        'c': nrm(ks[1], (BATCH, D), 1.0),
        'ctx': nrm(ks[2], (BATCH, CTX_LEN, D), 1.0),
        'c_ctx': nrm(ks[3], (D,), 1.0),
        'ada_w': nrm(ks[4], (DEPTH, D, N_MOD * D), 0.5 * D ** -0.5),
        'ada_b': nrm(ks[5], (DEPTH, N_MOD * D), 0.02),
        'ln_g': 1.0 + nrm(ks[6], (DEPTH, 2, D), 0.02),
        'ln_b': nrm(ks[7], (DEPTH, 2, D), 0.02),
        's5_a_re': -0.5 + nrm(ks[8], (NS, 2, G, P), 0.01),
        's5_a_im': math.pi * n_idx + nrm(ks[9], (NS, 2, G, P), 0.01),
        's5_log_dt': jax.random.uniform(ks[10], (NS, 2, G), f32, math.log(S5_DT_MIN), math.log(S5_DT_MAX)),
        's5_b_re': nrm(ks[11], (NS, 2, G, P, GC), (2 * GC) ** -0.5),
        's5_b_im': nrm(ks[12], (NS, 2, G, P, GC), (2 * GC) ** -0.5),
        's5_c_re': nrm(ks[13], (NS, 2, G, GC, P), P ** -0.5),
        's5_c_im': nrm(ks[14], (NS, 2, G, GC, P), P ** -0.5),
        's5_d': nrm(ks[15], (NS, D), 1.0),
        's5_w_glu': nrm(ks[16], (NS, D, D), D ** -0.5),
        's5_w_o': nrm(ks[17], (NS, D, D), BETA * D ** -0.5),
        'mla_w_dq': nrm(ks[18], (NA, D, MLA_Q_LORA), D ** -0.5),
        'mla_q_norm': 1.0 + nrm(ks[19], (NA, MLA_Q_LORA), 0.02),
        'mla_w_uq': nrm(ks[20], (NA, MLA_Q_LORA, MLA_HEADS * (MLA_NOPE + MLA_ROPE)), MLA_Q_LORA ** -0.5),
        'mla_w_dkv': nrm(ks[21], (NA, D, MLA_KV_LORA + MLA_ROPE), D ** -0.5),
        'mla_kv_norm': 1.0 + nrm(ks[22], (NA, MLA_KV_LORA), 0.02),
        'mla_w_ukv': nrm(ks[23], (NA, MLA_KV_LORA, MLA_HEADS * (MLA_NOPE + MLA_V)), MLA_KV_LORA ** -0.5),
        'mla_w_o': nrm(ks[24], (NA, MLA_HEADS * MLA_V, D), BETA * (MLA_HEADS * MLA_V) ** -0.5),
        'peer_w_q': nrm(ks[25], (DEPTH, D, PEER_HEADS * PEER_DK), D ** -0.5),
        'peer_keys': nrm(ks[26], (DEPTH, PEER_HEADS, 2, PEER_NKEYS, PEER_DK // 2), (PEER_DK // 2) ** -0.5),
        'peer_u': nrm(ks[27], (DEPTH, PEER_EXPERTS, D), D ** -0.5),
        'peer_v': nrm(ks[28], (DEPTH, PEER_EXPERTS, D), BETA),
    }


def reference(x, c, ctx, c_ctx, ada_w, ada_b, ln_g, ln_b,
              s5_a_re, s5_a_im, s5_log_dt, s5_b_re, s5_b_im, s5_c_re, s5_c_im, s5_d, s5_w_glu, s5_w_o,
              mla_w_dq, mla_q_norm, mla_w_uq, mla_w_dkv, mla_kv_norm, mla_w_ukv, mla_w_o,
              peer_w_q, peer_keys, peer_u, peer_v):
    bsz, n_lat, d = x.shape
    h_lat, h_ctx = x, ctx
    sc = jax.nn.silu(c)
    scc = jax.nn.silu(c_ctx)
    for i in range(DEPTH):
        last = i == DEPTH - 1
        mod_l = (sc @ ada_w[i] + ada_b[i]).reshape(bsz, N_MOD, d).transpose(1, 0, 2)[:, :, None, :]
        mod_c = (scc @ ada_w[i] + ada_b[i]).reshape(N_MOD, 1, 1, d)
        m_l = h_lat * (1.0 + mod_l[1]) + mod_l[0]
        m_c = h_ctx * (1.0 + mod_c[1]) + mod_c[0]
        j = i // N_MIXERS
        if i % N_MIXERS == 0:
            o_l, o_c = s5_mixer(m_l, m_c, s5_a_re[j], s5_a_im[j], s5_log_dt[j], s5_b_re[j], s5_b_im[j],
                                s5_c_re[j], s5_c_im[j], s5_d[j], s5_w_glu[j], s5_w_o[j], not last)
        else:
            o_l, o_c = mla_mixer(m_l, m_c, mla_w_dq[j], mla_q_norm[j], mla_w_uq[j], mla_w_dkv[j],
                                 mla_kv_norm[j], mla_w_ukv[j], mla_w_o[j], not last)
        h_lat = layer_norm(ALPHA * h_lat + mod_l[2] * o_l, ln_g[i, 0], ln_b[i, 0])
        f_l = peer_channel(h_lat * (1.0 + mod_l[4]) + mod_l[3], peer_w_q[i], peer_keys[i], peer_u[i], peer_v[i])
        h_lat = layer_norm(ALPHA * h_lat + mod_l[5] * f_l, ln_g[i, 1], ln_b[i, 1])
        if not last:
            h_ctx = layer_norm(ALPHA * h_ctx + mod_c[2] * o_c, ln_g[i, 0], ln_b[i, 0])
            f_c = peer_channel(h_ctx * (1.0 + mod_c[4]) + mod_c[3], peer_w_q[i], peer_keys[i], peer_u[i], peer_v[i])
            h_ctx = layer_norm(ALPHA * h_ctx + mod_c[5] * f_c, ln_g[i, 1], ln_b[i, 1])
    return h_lat.astype(x.dtype)
```

```python
import functools
import math

import jax
import jax.numpy as jnp
import numpy as np
from jax import lax
from jax.experimental import pallas as pl
from jax.experimental.pallas import tpu as pltpu

F32 = jnp.float32
BF16 = jnp.bfloat16
HIGHEST = lax.Precision.HIGHEST

LANES = 128
SUBLANES = 8
VMEM_LIMIT_BYTES = 56 * 1024 * 1024

N_MOD = 6
GRID_W = 64
S5_GROUP = 16
S5_STATE = 64
MLA_HEADS = 8
MLA_NOPE = 128
MLA_ROPE = 64
MLA_V = 128
ROPE_BASE = 10000.0
PEER_HEADS = 8
PEER_NKEYS = 128
PEER_DK = 128
PEER_TOPK = 16
LN_EPS = 1e-5
RMS_EPS = 1e-6

ROW_TILE = 256
S5_CHUNK = 128
PEER_TOK_TILE = 32
ATT_KV_CHUNK = 768


def _cparams(*sem):
    return pltpu.CompilerParams(dimension_semantics=sem, vmem_limit_bytes=VMEM_LIMIT_BYTES)


def _full(shape):
    n = len(shape)
    return pl.BlockSpec(shape, lambda *_: (0,) * n)


def _gelu(x):
    return 0.5 * x * (1.0 + lax.erf(x * (1.0 / math.sqrt(2.0))))


def _layer_norm(x, g, b):
    mu = jnp.mean(x, axis=-1, keepdims=True)
    xc = x - mu
    var = jnp.mean(xc * xc, axis=-1, keepdims=True)
    return xc * lax.rsqrt(var + LN_EPS) * g + b


def _rms_norm(x, g):
    return x * lax.rsqrt(jnp.mean(x * x, axis=-1, keepdims=True) + RMS_EPS) * g


def _bdot(a, b):
    return jnp.dot(a.astype(BF16), b, preferred_element_type=F32)


def _ada_kernel(c_ref, w_ref, b_ref, o_ref):
    c = c_ref[...]
    s = c * jax.nn.sigmoid(c)
    o_ref[...] = jnp.dot(s, w_ref[...], precision=HIGHEST, preferred_element_type=F32) + b_ref[...]


def ada_mod(cvec, w, b):
    d = cvec.shape[1]
    n = w.shape[1]
    tn = 1024
    return pl.pallas_call(
        _ada_kernel,
        grid=(n // tn,),
        in_specs=[_full((SUBLANES, d)), pl.BlockSpec((d, tn), lambda j: (0, j)),
                  pl.BlockSpec((1, tn), lambda j: (0, j))],
        out_specs=pl.BlockSpec((SUBLANES, tn), lambda j: (0, j)),
        out_shape=jax.ShapeDtypeStruct((SUBLANES, n), F32),
        compiler_params=_cparams("arbitrary"),
        name="ada_mod",
    )(cvec, w, b.reshape(1, n))


def _s5_kernel(xf_ref, xb_ref, mod_ref, bf_ref, cc_ref, ar_ref, ai_ref, yf_ref, yb_ref, bu_ref, x_ref,
               *, chunk, nbatch):
    nlc = bf_ref.shape[2]
    spc = bf_ref.shape[4] // LANES

    @pl.when(pl.program_id(0) == 0)
    def _():
        x_ref[...] = jnp.zeros_like(x_ref)

    for q in range(2 * nbatch):
        d, b = divmod(q, nbatch)
        src = xf_ref if d == 0 else xb_ref
        u = (src[b] * (1.0 + mod_ref[0, b, 1:2, :]) + mod_ref[0, b, 0:1, :]).astype(BF16)
        for part in range(2):
            r = q + 2 * nbatch * part
            for j in range(nlc):
                res = jnp.dot(u[:, LANES * j:LANES * (j + 1)], bf_ref[d, part, j], preferred_element_type=F32)
                for k in range(spc):
                    bu_ref[spc * j + k, pl.ds(r, chunk, stride=SUBLANES), :] = res[:, LANES * k:LANES * (k + 1)]

    ar = ar_ref[...]
    ai = ai_ref[...]

    def step(s, x):
        off = pl.multiple_of(s * SUBLANES, SUBLANES)
        xn = ar * x + ai * pltpu.roll(x, SUBLANES // 2, axis=1) + bu_ref[:, pl.ds(off, SUBLANES), :]
        bu_ref[:, pl.ds(off, SUBLANES), :] = xn
        return xn

    x_ref[...] = lax.fori_loop(0, chunk, step, x_ref[...])

    def state_rows(r, j):
        return jnp.concatenate([bu_ref[spc * j + k, pl.ds(r, chunk, stride=SUBLANES), :] for k in range(spc)],
                               axis=1).astype(BF16)

    for q in range(2 * nbatch):
        d, b = divmod(q, nbatch)
        out = yf_ref if d == 0 else yb_ref
        for j in range(nlc):
            y = (jnp.dot(state_rows(q, j), cc_ref[d, 0, j], preferred_element_type=F32)
                 + jnp.dot(state_rows(q + 2 * nbatch, j), cc_ref[d, 1, j], preferred_element_type=F32))
            out[b, :, LANES * j:LANES * (j + 1)] = y


def s5_scan(xf, xb, mod_seg, bfc, ccc, ar, ai, ctx_len):
    nb, ltot, d = xf.shape
    assert 2 * nb * 2 == SUBLANES, "state rows must fill one sublane tile"
    chunk = S5_CHUNK
    assert ltot % chunk == 0 and ctx_len % chunk == 0
    ctx_chunks = ctx_len // chunk
    nslab = ar.shape[0]
    seq_spec = pl.BlockSpec((nb, chunk, d), lambda c: (0, c, 0))
    return pl.pallas_call(
        functools.partial(_s5_kernel, chunk=chunk, nbatch=nb),
        grid=(ltot // chunk,),
        in_specs=[seq_spec, seq_spec,
                  pl.BlockSpec((1, nb, N_MOD, d), lambda c: (jnp.minimum(c // ctx_chunks, 1), 0, 0, 0)),
                  _full(bfc.shape), _full(ccc.shape), _full(ar.shape), _full(ai.shape)],
        out_specs=[seq_spec, seq_spec],
        out_shape=[jax.ShapeDtypeStruct(xf.shape, F32)] * 2,
        scratch_shapes=[pltpu.VMEM((nslab, SUBLANES * chunk, LANES), F32),
                        pltpu.VMEM((nslab, SUBLANES, LANES), F32)],
        compiler_params=_cparams("arbitrary"),
        name="s5_scan",
    )(xf, xb, mod_seg, bfc, ccc, ar, ai)


def s5_prepare(a_re, a_im, log_dt, b_re, b_im, c_re, c_im, nbatch):
    ndir, g, p = a_re.shape
    gc = b_re.shape[-1]
    gpc = LANES // gc
    nlc = g // gpc
    dt = jnp.exp(log_dt.astype(F32))[..., None]
    a_re = a_re.astype(F32)
    a_im = a_im.astype(F32)
    mag = jnp.exp(a_re * dt)
    ab_re = mag * jnp.cos(a_im * dt)
    ab_im = mag * jnp.sin(a_im * dt)
    nr, ni = ab_re - 1.0, ab_im
    den = a_re * a_re + a_im * a_im
    f_re = (nr * a_re + ni * a_im) / den
    f_im = (ni * a_re - nr * a_im) / den
    bf_re = f_re[..., None] * b_re - f_im[..., None] * b_im
    bf_im = f_re[..., None] * b_im + f_im[..., None] * b_re
    bf = jnp.stack([bf_re, bf_im], axis=1).reshape(ndir, 2, nlc, gpc, p, gc)
    eye = jnp.eye(gpc, dtype=F32)
    bfc = jnp.einsum('dqjgpc,gh->dqjgchp', bf, eye).reshape(ndir, 2, nlc, gpc * gc, gpc * p).astype(BF16)
    cc = jnp.stack([c_re, -c_im], axis=1).astype(F32).reshape(ndir, 2, nlc, gpc, gc, p)
    ccc = jnp.einsum('dqjgcp,gh->dqjgphc', cc, eye).reshape(ndir, 2, nlc, gpc * p, gpc * gc).astype(BF16)
    rows_r, rows_i = [], []
    for part in range(2):
        for d in range(ndir):
            for _ in range(nbatch):
                rows_r.append(ab_re[d].reshape(-1))
                rows_i.append(ab_im[d].reshape(-1) * (-1.0 if part == 0 else 1.0))
    slabs = lambda rows: jnp.stack(rows).reshape(len(rows), -1, LANES).transpose(1, 0, 2)
    return bfc, ccc, slabs(rows_r), slabs(rows_i)


def _post_kernel(*refs, alpha, glu):
    if glu:
        h_ref, yf_ref, yb_ref, mod_ref, dsk_ref, wg_ref, wo_ref, lng_ref, lnb_ref, h1_ref, t_ref = refs
        h = h_ref[...]
        m = h * (1.0 + mod_ref[0, 1:2, :]) + mod_ref[0, 0:1, :]
        z = _gelu(yf_ref[...] + yb_ref[...] + dsk_ref[...] * m)
        z = z * jax.nn.sigmoid(_bdot(z, wg_ref[...]))
    else:
        h_ref, z_ref, mod_ref, wo_ref, lng_ref, lnb_ref, h1_ref, t_ref = refs
        h = h_ref[...]
        z = z_ref[...]
    o = _bdot(z, wo_ref[...])
    h1 = _layer_norm(alpha * h + mod_ref[0, 2:3, :] * o, lng_ref[...], lnb_ref[...])
    h1_ref[...] = h1
    t_ref[...] = h1 * (1.0 + mod_ref[0, 4:5, :]) + mod_ref[0, 3:4, :]


def post_mixer(h, pre, mod_tiles, w_o, ln_g, ln_b, alpha, d_skip=None, w_glu=None):
    n, d = h.shape
    tm = ROW_TILE
    glu = w_glu is not None
    row = pl.BlockSpec((tm, d), lambda i: (i, 0))
    vec = _full((1, d))
    modspec = pl.BlockSpec((1, N_MOD, d), lambda i: (i, 0, 0))
    if glu:
        args = (h, pre[0], pre[1], mod_tiles, d_skip.reshape(1, d), w_glu.astype(BF16), w_o.astype(BF16),
                ln_g.reshape(1, d), ln_b.reshape(1, d))
        in_specs = [row, row, row, modspec, vec, _full((d, d)), _full((d, d)), vec, vec]
    else:
        args = (h, pre[0], mod_tiles, w_o.astype(BF16), ln_g.reshape(1, d), ln_b.reshape(1, d))
        in_specs = [row, pl.BlockSpec((tm, pre[0].shape[1]), lambda i: (i, 0)), modspec,
                    _full(w_o.shape), vec, vec]
    return pl.pallas_call(
        functools.partial(_post_kernel, alpha=alpha, glu=glu),
        grid=(n // tm,),
        in_specs=in_specs,
        out_specs=[row, row],
        out_shape=[jax.ShapeDtypeStruct((n, d), F32)] * 2,
        compiler_params=_cparams("arbitrary"),
        name="post_mixer_glu" if glu else "post_mixer",
    )(*args)


def _res_ln_kernel(h_ref, f_ref, mod_ref, lng_ref, lnb_ref, o_ref, *, alpha):
    o_ref[...] = _layer_norm(alpha * h_ref[...] + mod_ref[0, 5:6, :] * f_ref[...], lng_ref[...], lnb_ref[...])


def residual_ln(h, f, mod_tiles, ln_g, ln_b, alpha):
    n, d = h.shape
    tm = ROW_TILE
    row = pl.BlockSpec((tm, d), lambda i: (i, 0))
    vec = _full((1, d))
    return pl.pallas_call(
        functools.partial(_res_ln_kernel, alpha=alpha),
        grid=(n // tm,),
        in_specs=[row, row, pl.BlockSpec((1, N_MOD, d), lambda i: (i, 0, 0)), vec, vec],
        out_specs=row,
        out_shape=jax.ShapeDtypeStruct((n, d), F32),
        compiler_params=_cparams("arbitrary"),
        name="residual_ln",
    )(h, f, mod_tiles, ln_g.reshape(1, d), ln_b.reshape(1, d))


def _topk_rows(s, k, payload=None):
    nrow = s.shape[0]
    iota = lax.broadcasted_iota(jnp.int32, s.shape, 0).astype(F32)
    vals, idxs, pays = [], [], []
    for _ in range(k):
        m = jnp.max(s, axis=0, keepdims=True)
        i = jnp.min(jnp.where(s == m, iota, float(nrow)), axis=0, keepdims=True)
        hit = iota == i
        vals.append(m)
        idxs.append(i)
        if payload is not None:
            pays.append(jnp.max(jnp.where(hit, payload, -1.0), axis=0, keepdims=True))
        s = jnp.where(hit, -jnp.inf, s)
    return vals, idxs, pays


def _route_kernel(t_ref, wqt_ref, keys_ref, e_ref, g_ref):
    nt = (((1,), (1,)), ((), ()))
    qt = lax.dot_general(wqt_ref[...], t_ref[...], nt, precision=HIGHEST, preferred_element_type=F32)
    half = PEER_DK // 2
    for h in range(PEER_HEADS):
        sv, si = [], []
        for s in range(2):
            row0 = (h * 2 + s) * half
            sc = jnp.dot(keys_ref[h * 2 + s], qt[row0:row0 + half, :], precision=HIGHEST,
                         preferred_element_type=F32)
            v, i, _ = _topk_rows(sc, PEER_TOPK)
            sv.append(v)
            si.append(i)
        v2 = jnp.concatenate(sv[1], axis=0)
        i2 = jnp.concatenate(si[1], axis=0)
        cand = jnp.concatenate([sv[0][a] + v2 for a in range(PEER_TOPK)], axis=0)
        cexp = jnp.concatenate([si[0][a] * float(PEER_NKEYS) + i2 for a in range(PEER_TOPK)], axis=0)
        cv, _, ce = _topk_rows(cand, PEER_TOPK, payload=cexp)
        cvm = jnp.concatenate(cv, axis=0)
        ex = jnp.exp(cvm - cv[0])
        lo = h * PEER_TOPK
        g_ref[lo:lo + PEER_TOPK, :] = ex / jnp.sum(ex, axis=0, keepdims=True)
        e_ref[lo:lo + PEER_TOPK, :] = jnp.concatenate(ce, axis=0).astype(jnp.int32)


def peer_route(t, w_q, keys):
    n, d = t.shape
    tm = ROW_TILE
    hk = PEER_HEADS * PEER_TOPK
    keys2 = keys.reshape(PEER_HEADS * 2, PEER_NKEYS, PEER_DK // 2)
    out = pl.BlockSpec((hk, tm), lambda i: (0, i))
    return pl.pallas_call(
        _route_kernel,
        grid=(n // tm,),
        in_specs=[pl.BlockSpec((tm, d), lambda i: (i, 0)), _full((w_q.shape[1], d)), _full(keys2.shape)],
        out_specs=[out, out],
        out_shape=[jax.ShapeDtypeStruct((hk, n), jnp.int32), jax.ShapeDtypeStruct((hk, n), F32)],
        compiler_params=_cparams("arbitrary"),
        name="peer_route",
    )(t, w_q.T, keys2)


def _peer_dot_kernel(idx_ref, tab_ref, th_ref, prev_ref, gate_ref, o_ref, z_ref, *, tm, nk, final):
    def token(t, carry):
        h = th_ref[t]
        for k in range(nk):
            row = tab_ref[idx_ref[t, k]]
            z_ref[t, pl.ds(k, 1), :] = jnp.sum(row * h, axis=0, keepdims=True)
        return carry

    lax.fori_loop(0, tm, token, 0)
    ones = jnp.ones((SUBLANES, LANES), F32)
    z = z_ref[...].reshape(tm * nk, LANES)
    r = lax.dot_general(ones, z, (((1,), (1,)), ((), ())), precision=HIGHEST, preferred_element_type=F32)
    act = r[0:1, :] + prev_ref[...]
    if final:
        act = _gelu(act) * gate_ref[...]
    o_ref[...] = act


def peer_dot(idx, tab, th, prev, gate, final):
    n, nk = idx.shape
    tm = PEER_TOK_TILE
    flat = pl.BlockSpec((1, tm * nk), lambda i: (0, i))
    return pl.pallas_call(
        functools.partial(_peer_dot_kernel, tm=tm, nk=nk, final=final),
        grid=(n // tm,),
        in_specs=[pl.BlockSpec((tm, nk), lambda i: (i, 0), memory_space=pltpu.SMEM),
                  pl.BlockSpec(tab.shape, lambda i: (0, 0, 0), pipeline_mode=pl.Buffered(1)),
                  pl.BlockSpec((tm,) + th.shape[1:], lambda i: (i, 0, 0)), flat, flat],
        out_specs=flat,
        out_shape=jax.ShapeDtypeStruct((1, n * nk), F32),
        scratch_shapes=[pltpu.VMEM((tm, nk, LANES), F32)],
        compiler_params=_cparams("arbitrary"),
        name="peer_dot",
    )(idx, tab, th, prev, gate)


def _peer_acc_kernel(idx_ref, a_ref, tab_ref, o_ref, *, tm, nk, nacc):
    def token(t, carry):
        acc = [jnp.zeros(tab_ref.shape[1:], F32) for _ in range(nacc)]
        for k in range(nk):
            acc[k % nacc] = acc[k % nacc] + a_ref[t, k] * tab_ref[idx_ref[t, k]]
        tot = acc[0]
        for a in acc[1:]:
            tot = tot + a
        o_ref[t] = tot
        return carry

    lax.fori_loop(0, tm, token, 0)


def peer_acc(idx, act, tab):
    n, nk = idx.shape
    tm = PEER_TOK_TILE
    smem = pl.BlockSpec((tm, nk), lambda i: (i, 0), memory_space=pltpu.SMEM)
    return pl.pallas_call(
        functools.partial(_peer_acc_kernel, tm=tm, nk=nk, nacc=4),
        grid=(n // tm,),
        in_specs=[smem, smem, pl.BlockSpec(tab.shape, lambda i: (0, 0, 0), pipeline_mode=pl.Buffered(1))],
        out_specs=pl.BlockSpec((tm,) + tab.shape[1:], lambda i: (i, 0, 0)),
        out_shape=jax.ShapeDtypeStruct((n,) + tab.shape[1:], F32),
        compiler_params=_cparams("arbitrary"),
        name="peer_acc",
    )(idx, act, tab)


def peer_channel(t, w_q, keys, u_tab, v_tab):
    n, d = t.shape
    half = d // 2
    rows = half // LANES
    e, g = peer_route(t, w_q, keys)
    idx = e.T
    nk = idx.shape[1]
    gate = g.T.reshape(1, n * nk)
    act = jnp.zeros((1, n * nk), F32)
    for s in range(2):
        tab = u_tab[:, s * half:(s + 1) * half].reshape(-1, rows, LANES)
        th = t[:, s * half:(s + 1) * half].reshape(n, rows, LANES)
        act = peer_dot(idx, tab, th, act, gate, final=(s == 1))
    act = act.reshape(n, nk)
    outs = []
    for s in range(2):
        tab = v_tab[:, s * half:(s + 1) * half].reshape(-1, rows, LANES)
        outs.append(peer_acc(idx, act, tab).reshape(n, half))
    return jnp.concatenate(outs, axis=1)


def _mla_proj_kernel(h_ref, mod_ref, ck_ref, sk_ref, cq_ref, sq_ref, wdc_ref, wdr_ref, wdrs_ref, kvn_ref,
                     wuk_ref, wuv_ref, wdq_ref, qn_ref, wqn_ref, wqr_ref, wqrs_ref, k_ref, v_ref, q_ref, *, scale):
    m = (h_ref[...] * (1.0 + mod_ref[0, 1:2, :]) + mod_ref[0, 0:1, :]).astype(BF16)
    ckv = _rms_norm(jnp.dot(m, wdc_ref[...], preferred_element_type=F32), kvn_ref[...]).astype(BF16)
    kr = (jnp.dot(m, wdr_ref[...], preferred_element_type=F32) * ck_ref[...]
          + jnp.dot(m, wdrs_ref[...], preferred_element_type=F32) * sk_ref[...]).astype(BF16)
    kn = jnp.dot(ckv, wuk_ref[...], preferred_element_type=F32).astype(BF16)
    v = jnp.dot(ckv, wuv_ref[...], preferred_element_type=F32).astype(BF16)
    cq = _rms_norm(jnp.dot(m, wdq_ref[...], preferred_element_type=F32), qn_ref[...]).astype(BF16)
    qn = (jnp.dot(cq, wqn_ref[...], preferred_element_type=F32) * scale).astype(BF16)
    qr = ((jnp.dot(cq, wqr_ref[...], preferred_element_type=F32) * cq_ref[...]
           + jnp.dot(cq, wqrs_ref[...], preferred_element_type=F32) * sq_ref[...]) * scale).astype(BF16)
    for h in range(MLA_HEADS):
        k_ref[0, h, :, 0:MLA_NOPE] = kn[:, h * MLA_NOPE:(h + 1) * MLA_NOPE]
        k_ref[0, h, :, MLA_NOPE:] = kr
        v_ref[0, h] = v[:, h * MLA_V:(h + 1) * MLA_V]
        q_ref[0, h, :, 0:MLA_NOPE] = qn[:, h * MLA_NOPE:(h + 1) * MLA_NOPE]
        q_ref[0, h, :, MLA_NOPE:] = qr[:, h * MLA_ROPE:(h + 1) * MLA_ROPE]


def _swap_halves(w, width):
    r = w.reshape(w.shape[0], -1, 2, width // 2)
    return r[:, :, ::-1, :].reshape(w.shape)


def mla_project(x, mod_tiles, ctx_len, nbatch, w_dq, q_norm, w_uq, w_dkv, kv_norm, w_ukv):
    n, d = x.shape
    ltot = n // nbatch
    nlat = ltot - ctx_len
    tm = ROW_TILE
    tpb = ltot // tm
    hd = MLA_NOPE + MLA_ROPE
    kvl = kv_norm.shape[0]
    ql = q_norm.shape[0]
    pos = jnp.arange(nlat, dtype=jnp.int32)
    nf = MLA_ROPE // 4
    inv = ROPE_BASE ** (-jnp.arange(nf, dtype=F32) / nf)
    ang = jnp.concatenate([(pos // GRID_W).astype(F32)[:, None] * inv,
                           (pos % GRID_W).astype(F32)[:, None] * inv], axis=-1)
    cos = jnp.concatenate([jnp.ones((ctx_len, MLA_ROPE // 2), F32), jnp.cos(ang)], axis=0)
    sin = jnp.concatenate([jnp.zeros((ctx_len, MLA_ROPE // 2), F32), jnp.sin(ang)], axis=0)
    ck = jnp.concatenate([cos, cos], axis=1)
    sk = jnp.concatenate([-sin, sin], axis=1)
    cq = jnp.tile(ck, (1, MLA_HEADS))
    sq = jnp.tile(sk, (1, MLA_HEADS))
    w_dkv_c = w_dkv[:, :kvl].astype(BF16)
    w_dkv_r = w_dkv[:, kvl:]
    ukv = w_ukv.reshape(kvl, MLA_HEADS, MLA_NOPE + MLA_V)
    w_uk = ukv[:, :, :MLA_NOPE].reshape(kvl, -1).astype(BF16)
    w_uv = ukv[:, :, MLA_NOPE:].reshape(kvl, -1).astype(BF16)
    uq = w_uq.reshape(ql, MLA_HEADS, hd)
    w_qn = uq[:, :, :MLA_NOPE].reshape(ql, -1).astype(BF16)
    w_qr = uq[:, :, MLA_NOPE:].reshape(ql, -1)
    args = (x, mod_tiles, ck, sk, cq, sq, w_dkv_c, w_dkv_r.astype(BF16),
            _swap_halves(w_dkv_r, MLA_ROPE).astype(BF16), kv_norm.reshape(1, kvl), w_uk, w_uv,
            w_dq.astype(BF16), q_norm.reshape(1, ql), w_qn, w_qr.astype(BF16),
            _swap_halves(w_qr, MLA_ROPE).astype(BF16))
    pos_spec = lambda w: pl.BlockSpec((tm, w), lambda i: (i % tpb, 0))
    in_specs = [pl.BlockSpec((tm, d), lambda i: (i, 0)), pl.BlockSpec((1, N_MOD, d), lambda i: (i, 0, 0)),
                pos_spec(MLA_ROPE), pos_spec(MLA_ROPE), pos_spec(MLA_ROPE * MLA_HEADS),
                pos_spec(MLA_ROPE * MLA_HEADS)] + [_full(a.shape) for a in args[6:]]
    head_spec = lambda w: pl.BlockSpec((1, MLA_HEADS, tm, w), lambda i: (i // tpb, 0, i % tpb, 0))
    return pl.pallas_call(
        functools.partial(_mla_proj_kernel, scale=float(hd) ** -0.5),
        grid=(n // tm,),
        in_specs=in_specs,
        out_specs=[head_spec(hd), head_spec(MLA_V), head_spec(hd)],
        out_shape=[jax.ShapeDtypeStruct((nbatch, MLA_HEADS, ltot, hd), BF16),
                   jax.ShapeDtypeStruct((nbatch, MLA_HEADS, ltot, MLA_V), BF16),
                   jax.ShapeDtypeStruct((nbatch, MLA_HEADS, ltot, hd), BF16)],
        compiler_params=_cparams("arbitrary"),
        name="mla_project",
    )(*args)


def _attn_kernel(q_ref, k_ref, v_ref, o_ref, *, tk, nchunks):
    q = q_ref[0, 0]
    tq = q.shape[0]

    def chunk(c, carry):
        m_i, l_i, acc = carry
        off = pl.multiple_of(c * tk, tk)
        k = k_ref[0, 0, pl.ds(off, tk), :]
        s = lax.dot_general(q, k, (((1,), (1,)), ((), ())), preferred_element_type=F32)
        m_new = jnp.maximum(m_i, jnp.max(s, axis=-1, keepdims=True))
        alpha = jnp.exp(m_i - m_new)
        p = jnp.exp(s - m_new)
        l_new = alpha * l_i + jnp.sum(p, axis=-1, keepdims=True)
        acc_new = alpha * acc + jnp.dot(p.astype(BF16), v_ref[0, 0, pl.ds(off, tk), :],
                                        preferred_element_type=F32)
        return m_new, l_new, acc_new

    init = (jnp.full((tq, 1), -jnp.inf, F32), jnp.zeros((tq, 1), F32), jnp.zeros((tq, MLA_V), F32))
    _, l_i, acc = lax.fori_loop(0, nchunks, chunk, init)
    o_ref[0] = acc / l_i


def mla_attend(q, k, v, ctx_len):
    nb, nh, ltot, hd = k.shape
    tq = ROW_TILE
    assert ctx_len % tq == 0
    skip = ctx_len // tq
    nlat = ltot - ctx_len
    tk = ATT_KV_CHUNK
    assert ltot % tk == 0
    return pl.pallas_call(
        functools.partial(_attn_kernel, tk=tk, nchunks=ltot // tk),
        grid=(nb, nh, nlat // tq),
        in_specs=[pl.BlockSpec((1, 1, tq, hd), lambda b, h, i: (b, h, i + skip, 0)),
                  pl.BlockSpec((1, 1, ltot, hd), lambda b, h, i: (b, h, 0, 0)),
                  pl.BlockSpec((1, 1, ltot, MLA_V), lambda b, h, i: (b, h, 0, 0))],
        out_specs=pl.BlockSpec((1, tq, MLA_V), lambda b, h, i: (b, i, h)),
        out_shape=jax.ShapeDtypeStruct((nb, nlat, nh * MLA_V), F32),
        compiler_params=_cparams("arbitrary", "arbitrary", "arbitrary"),
        name="mla_attend",
    )(q, k, v)


def kernel(x, c, ctx, c_ctx, ada_w, ada_b, ln_g, ln_b, s5_a_re, s5_a_im, s5_log_dt, s5_b_re, s5_b_im, s5_c_re, s5_c_im, s5_d, s5_w_glu, s5_w_o, mla_w_dq, mla_q_norm, mla_w_uq, mla_w_dkv, mla_kv_norm, mla_w_ukv, mla_w_o, peer_w_q, peer_keys, peer_u, peer_v):
    nb, nlat, d = x.shape
    nctx = ctx.shape[1]
    ltot = nctx + nlat
    depth = ada_w.shape[0]
    alpha = (2 * depth) ** 0.25
    tm = ROW_TILE

    cvec = jnp.zeros((SUBLANES, d), F32).at[:nb].set(c).at[nb].set(c_ctx)
    tiles_per_batch = ltot // tm
    tile_rows_all = np.array([nb if j < nctx // tm else b for b in range(nb) for j in range(tiles_per_batch)])
    tile_rows_lat = np.array([b for b in range(nb) for _ in range(nlat // tm)])
    seg_rows = np.array([[nb] * nb, list(range(nb))])

    h_all = jnp.concatenate([ctx, x], axis=1).reshape(nb * ltot, d)

    mod = ada_mod(cvec, ada_w[0], ada_b[0]).reshape(SUBLANES, N_MOD, d)
    mod_all = mod[tile_rows_all]
    h3 = h_all.reshape(nb, ltot, d)
    xb = jnp.flip(jnp.concatenate([h3[:, nctx:], h3[:, :nctx]], axis=1), axis=1)
    bfc, ccc, ar, ai = s5_prepare(s5_a_re[0], s5_a_im[0], s5_log_dt[0], s5_b_re[0], s5_b_im[0],
                                  s5_c_re[0], s5_c_im[0], nb)
    yf, yb = s5_scan(h3, xb, mod[seg_rows], bfc, ccc, ar, ai, nctx)
    yb = jnp.flip(yb, axis=1)
    yb = jnp.concatenate([yb[:, nlat:], yb[:, :nlat]], axis=1)
    h1, t = post_mixer(h_all, (yf.reshape(-1, d), yb.reshape(-1, d)), mod_all, s5_w_o[0], ln_g[0, 0], ln_b[0, 0],
                       alpha, d_skip=s5_d[0], w_glu=s5_w_glu[0])
    f = peer_channel(t, peer_w_q[0], peer_keys[0], peer_u[0], peer_v[0])
    h_all = residual_ln(h1, f, mod_all, ln_g[0, 1], ln_b[0, 1], alpha)

    mod = ada_mod(cvec, ada_w[1], ada_b[1]).reshape(SUBLANES, N_MOD, d)
    k, v, q = mla_project(h_all, mod[tile_rows_all], nctx, nb, mla_w_dq[0], mla_q_norm[0], mla_w_uq[0],
                          mla_w_dkv[0], mla_kv_norm[0], mla_w_ukv[0])
    att = mla_attend(q, k, v, nctx).reshape(nb * nlat, -1)
    h_lat = h_all.reshape(nb, ltot, d)[:, nctx:].reshape(nb * nlat, d)
    mod_lat = mod[tile_rows_lat]
    h1, t = post_mixer(h_lat, (att,), mod_lat, mla_w_o[0], ln_g[1, 0], ln_b[1, 0], alpha)
    f = peer_channel(t, peer_w_q[1], peer_keys[1], peer_u[1], peer_v[1])
    out = residual_ln(h1, f, mod_lat, ln_g[1, 1], ln_b[1, 1], alpha)
    return out.reshape(nb, nlat, d).astype(x.dtype)
```

```python
import functools
import math

import jax
import jax.numpy as jnp
import numpy as np
from jax import lax
from jax.experimental import pallas as pl
from jax.experimental.pallas import tpu as pltpu

F32 = jnp.float32
BF16 = jnp.bfloat16
HIGHEST = lax.Precision.HIGHEST

LANES = 128
SUBLANES = 8
VMEM_LIMIT_BYTES = 56 * 1024 * 1024

N_MOD = 6
GRID_W = 64
S5_GROUP = 16
S5_STATE = 64
MLA_HEADS = 8
MLA_NOPE = 128
MLA_ROPE = 64
MLA_V = 128
ROPE_BASE = 10000.0
PEER_HEADS = 8
PEER_NKEYS = 128
PEER_DK = 128
PEER_TOPK = 16
LN_EPS = 1e-5
RMS_EPS = 1e-6

ROW_TILE = 256
S5_CHUNK = 128
PEER_TOK_TILE = 32
ATT_KV_CHUNK = 768


def _cparams(*sem):
    return pltpu.CompilerParams(dimension_semantics=sem, vmem_limit_bytes=VMEM_LIMIT_BYTES)


def _full(shape):
    n = len(shape)
    return pl.BlockSpec(shape, lambda *_: (0,) * n)


def _gelu(x):
    return 0.5 * x * (1.0 + lax.erf(x * (1.0 / math.sqrt(2.0))))


def _layer_norm(x, g, b):
    mu = jnp.mean(x, axis=-1, keepdims=True)
    xc = x - mu
    var = jnp.mean(xc * xc, axis=-1, keepdims=True)
    return xc * lax.rsqrt(var + LN_EPS) * g + b


def _rms_norm(x, g):
    return x * lax.rsqrt(jnp.mean(x * x, axis=-1, keepdims=True) + RMS_EPS) * g


def _bdot(a, b):
    return jnp.dot(a.astype(BF16), b, preferred_element_type=F32)


def _ada_kernel(c_ref, w_ref, b_ref, o_ref):
    c = c_ref[...]
    s = c * jax.nn.sigmoid(c)
    o_ref[...] = jnp.dot(s, w_ref[...], precision=HIGHEST, preferred_element_type=F32) + b_ref[...]


def ada_mod(cvec, w, b):
    d = cvec.shape[1]
    n = w.shape[1]
    tn = 1024
    return pl.pallas_call(
        _ada_kernel,
        grid=(n // tn,),
        in_specs=[_full((SUBLANES, d)), pl.BlockSpec((d, tn), lambda j: (0, j)),
                  pl.BlockSpec((1, tn), lambda j: (0, j))],
        out_specs=pl.BlockSpec((SUBLANES, tn), lambda j: (0, j)),
        out_shape=jax.ShapeDtypeStruct((SUBLANES, n), F32),
        compiler_params=_cparams("arbitrary"),
        name="ada_mod",
    )(cvec, w, b.reshape(1, n))


def _s5_kernel(xf_ref, xb_ref, mod_ref, bf_ref, cc_ref, ar_ref, ai_ref, yf_ref, yb_ref, bu_ref, x_ref,
               *, chunk, nbatch):
    nlc = bf_ref.shape[2]
    spc = bf_ref.shape[4] // LANES

    @pl.when(pl.program_id(0) == 0)
    def _():
        x_ref[...] = jnp.zeros_like(x_ref)

    for q in range(2 * nbatch):
        d, b = divmod(q, nbatch)
        src = xf_ref if d == 0 else xb_ref
        u = (src[b] * (1.0 + mod_ref[0, b, 1:2, :]) + mod_ref[0, b, 0:1, :]).astype(BF16)
        for part in range(2):
            r = q + 2 * nbatch * part
            for j in range(nlc):
                res = jnp.dot(u[:, LANES * j:LANES * (j + 1)], bf_ref[d, part, j], preferred_element_type=F32)
                for k in range(spc):
                    bu_ref[spc * j + k, pl.ds(r, chunk, stride=SUBLANES), :] = res[:, LANES * k:LANES * (k + 1)]

    ar = ar_ref[...]
    ai = ai_ref[...]

    def step(s, x):
        off = pl.multiple_of(s * SUBLANES, SUBLANES)
        xn = ar * x + ai * pltpu.roll(x, SUBLANES // 2, axis=1) + bu_ref[:, pl.ds(off, SUBLANES), :]
        bu_ref[:, pl.ds(off, SUBLANES), :] = xn
        return xn

    x_ref[...] = lax.fori_loop(0, chunk, step, x_ref[...])

    def state_rows(r, j):
        return jnp.concatenate([bu_ref[spc * j + k, pl.ds(r, chunk, stride=SUBLANES), :] for k in range(spc)],
                               axis=1).astype(BF16)

    for q in range(2 * nbatch):
        d, b = divmod(q, nbatch)
        out = yf_ref if d == 0 else yb_ref
        for j in range(nlc):
            y = (jnp.dot(state_rows(q, j), cc_ref[d, 0, j], preferred_element_type=F32)
                 + jnp.dot(state_rows(q + 2 * nbatch, j), cc_ref[d, 1, j], preferred_element_type=F32))
            out[b, :, LANES * j:LANES * (j + 1)] = y


def s5_scan(xf, xb, mod_seg, bfc, ccc, ar, ai, ctx_len):
    nb, ltot, d = xf.shape
    assert 2 * nb * 2 == SUBLANES, "state rows must fill one sublane tile"
    chunk = S5_CHUNK
    assert ltot % chunk == 0 and ctx_len % chunk == 0
    ctx_chunks = ctx_len // chunk
    nslab = ar.shape[0]
    seq_spec = pl.BlockSpec((nb, chunk, d), lambda c: (0, c, 0))
    return pl.pallas_call(
        functools.partial(_s5_kernel, chunk=chunk, nbatch=nb),
        grid=(ltot // chunk,),
        in_specs=[seq_spec, seq_spec,
                  pl.BlockSpec((1, nb, N_MOD, d), lambda c: (jnp.minimum(c // ctx_chunks, 1), 0, 0, 0)),
                  _full(bfc.shape), _full(ccc.shape), _full(ar.shape), _full(ai.shape)],
        out_specs=[seq_spec, seq_spec],
        out_shape=[jax.ShapeDtypeStruct(xf.shape, F32)] * 2,
        scratch_shapes=[pltpu.VMEM((nslab, SUBLANES * chunk, LANES), F32),
                        pltpu.VMEM((nslab, SUBLANES, LANES), F32)],
        compiler_params=_cparams("arbitrary"),
        name="s5_scan",
    )(xf, xb, mod_seg, bfc, ccc, ar, ai)


def s5_prepare(a_re, a_im, log_dt, b_re, b_im, c_re, c_im, nbatch):
    ndir, g, p = a_re.shape
    gc = b_re.shape[-1]
    gpc = LANES // gc
    nlc = g // gpc
    dt = jnp.exp(log_dt.astype(F32))[..., None]
    a_re = a_re.astype(F32)
    a_im = a_im.astype(F32)
    mag = jnp.exp(a_re * dt)
    ab_re = mag * jnp.cos(a_im * dt)
    ab_im = mag * jnp.sin(a_im * dt)
    nr, ni = ab_re - 1.0, ab_im
    den = a_re * a_re + a_im * a_im
    f_re = (nr * a_re + ni * a_im) / den
    f_im = (ni * a_re - nr * a_im) / den
    bf_re = f_re[..., None] * b_re - f_im[..., None] * b_im
    bf_im = f_re[..., None] * b_im + f_im[..., None] * b_re
    bf = jnp.stack([bf_re, bf_im], axis=1).reshape(ndir, 2, nlc, gpc, p, gc)
    eye = jnp.eye(gpc, dtype=F32)
    bfc = jnp.einsum('dqjgpc,gh->dqjgchp', bf, eye).reshape(ndir, 2, nlc, gpc * gc, gpc * p).astype(BF16)
    cc = jnp.stack([c_re, -c_im], axis=1).astype(F32).reshape(ndir, 2, nlc, gpc, gc, p)
    ccc = jnp.einsum('dqjgcp,gh->dqjgphc', cc, eye).reshape(ndir, 2, nlc, gpc * p, gpc * gc).astype(BF16)
    rows_r, rows_i = [], []
    for part in range(2):
        for d in range(ndir):
            for _ in range(nbatch):
                rows_r.append(ab_re[d].reshape(-1))
                rows_i.append(ab_im[d].reshape(-1) * (-1.0 if part == 0 else 1.0))
    slabs = lambda rows: jnp.stack(rows).reshape(len(rows), -1, LANES).transpose(1, 0, 2)
    return bfc, ccc, slabs(rows_r), slabs(rows_i)


def _post_kernel(*refs, alpha, glu):
    if glu:
        h_ref, yf_ref, yb_ref, mod_ref, dsk_ref, wg_ref, wo_ref, lng_ref, lnb_ref, h1_ref, t_ref = refs
        h = h_ref[...]
        m = h * (1.0 + mod_ref[0, 1:2, :]) + mod_ref[0, 0:1, :]
        z = _gelu(yf_ref[...] + yb_ref[...] + dsk_ref[...] * m)
        z = z * jax.nn.sigmoid(_bdot(z, wg_ref[...]))
    else:
        h_ref, z_ref, mod_ref, wo_ref, lng_ref, lnb_ref, h1_ref, t_ref = refs
        h = h_ref[...]
        z = z_ref[...]
    o = _bdot(z, wo_ref[...])
    h1 = _layer_norm(alpha * h + mod_ref[0, 2:3, :] * o, lng_ref[...], lnb_ref[...])
    h1_ref[...] = h1
    t_ref[...] = h1 * (1.0 + mod_ref[0, 4:5, :]) + mod_ref[0, 3:4, :]


def post_mixer(h, pre, mod_tiles, w_o, ln_g, ln_b, alpha, d_skip=None, w_glu=None):
    n, d = h.shape
    tm = ROW_TILE
    glu = w_glu is not None
    row = pl.BlockSpec((tm, d), lambda i: (i, 0))
    vec = _full((1, d))
    modspec = pl.BlockSpec((1, N_MOD, d), lambda i: (i, 0, 0))
    if glu:
        args = (h, pre[0], pre[1], mod_tiles, d_skip.reshape(1, d), w_glu.astype(BF16), w_o.astype(BF16),
                ln_g.reshape(1, d), ln_b.reshape(1, d))
        in_specs = [row, row, row, modspec, vec, _full((d, d)), _full((d, d)), vec, vec]
    else:
        args = (h, pre[0], mod_tiles, w_o.astype(BF16), ln_g.reshape(1, d), ln_b.reshape(1, d))
        in_specs = [row, pl.BlockSpec((tm, pre[0].shape[1]), lambda i: (i, 0)), modspec,
                    _full(w_o.shape), vec, vec]
    return pl.pallas_call(
        functools.partial(_post_kernel, alpha=alpha, glu=glu),
        grid=(n // tm,),
        in_specs=in_specs,
        out_specs=[row, row],
        out_shape=[jax.ShapeDtypeStruct((n, d), F32)] * 2,
        compiler_params=_cparams("arbitrary"),
        name="post_mixer_glu" if glu else "post_mixer",
    )(*args)


def _res_ln_kernel(h_ref, f_ref, mod_ref, lng_ref, lnb_ref, o_ref, *, alpha):
    o_ref[...] = _layer_norm(alpha * h_ref[...] + mod_ref[0, 5:6, :] * f_ref[...], lng_ref[...], lnb_ref[...])


def residual_ln(h, f, mod_tiles, ln_g, ln_b, alpha):
    n, d = h.shape
    tm = ROW_TILE
    row = pl.BlockSpec((tm, d), lambda i: (i, 0))
    vec = _full((1, d))
    return pl.pallas_call(
        functools.partial(_res_ln_kernel, alpha=alpha),
        grid=(n // tm,),
        in_specs=[row, row, pl.BlockSpec((1, N_MOD, d), lambda i: (i, 0, 0)), vec, vec],
        out_specs=row,
        out_shape=jax.ShapeDtypeStruct((n, d), F32),
        compiler_params=_cparams("arbitrary"),
        name="residual_ln",
    )(h, f, mod_tiles, ln_g.reshape(1, d), ln_b.reshape(1, d))


def _topk_rows(s, k, payload=None):
    nrow = s.shape[0]
    iota = lax.broadcasted_iota(jnp.int32, s.shape, 0).astype(F32)
    vals, idxs, pays = [], [], []
    for _ in range(k):
        m = jnp.max(s, axis=0, keepdims=True)
        i = jnp.min(jnp.where(s == m, iota, float(nrow)), axis=0, keepdims=True)
        hit = iota == i
        vals.append(m)
        idxs.append(i)
        if payload is not None:
            pays.append(jnp.max(jnp.where(hit, payload, -1.0), axis=0, keepdims=True))
        s = jnp.where(hit, -jnp.inf, s)
    return vals, idxs, pays


def _route_kernel(t_ref, wqt_ref, keys_ref, e_ref, g_ref):
    nt = (((1,), (1,)), ((), ()))
    qt = lax.dot_general(wqt_ref[...], t_ref[...], nt, precision=HIGHEST, preferred_element_type=F32)
    half = PEER_DK // 2
    for h in range(PEER_HEADS):
        sv, si = [], []
        for s in range(2):
            row0 = (h * 2 + s) * half
            sc = jnp.dot(keys_ref[h * 2 + s], qt[row0:row0 + half, :], precision=HIGHEST,
                         preferred_element_type=F32)
            v, i, _ = _topk_rows(sc, PEER_TOPK)
            sv.append(v)
            si.append(i)
        v2 = jnp.concatenate(sv[1], axis=0)
        i2 = jnp.concatenate(si[1], axis=0)
        cand = jnp.concatenate([sv[0][a] + v2 for a in range(PEER_TOPK)], axis=0)
        cexp = jnp.concatenate([si[0][a] * float(PEER_NKEYS) + i2 for a in range(PEER_TOPK)], axis=0)
        cv, _, ce = _topk_rows(cand, PEER_TOPK, payload=cexp)
        cvm = jnp.concatenate(cv, axis=0)
        ex = jnp.exp(cvm - cv[0])
        lo = h * PEER_TOPK
        g_ref[lo:lo + PEER_TOPK, :] = ex / jnp.sum(ex, axis=0, keepdims=True)
        e_ref[lo:lo + PEER_TOPK, :] = jnp.concatenate(ce, axis=0).astype(jnp.int32)


def peer_route(t, w_q, keys):
    n, d = t.shape
    tm = ROW_TILE
    hk = PEER_HEADS * PEER_TOPK
    keys2 = keys.reshape(PEER_HEADS * 2, PEER_NKEYS, PEER_DK // 2)
    out = pl.BlockSpec((hk, tm), lambda i: (0, i))
    return pl.pallas_call(
        _route_kernel,
        grid=(n // tm,),
        in_specs=[pl.BlockSpec((tm, d), lambda i: (i, 0)), _full((w_q.shape[1], d)), _full(keys2.shape)],
        out_specs=[out, out],
        out_shape=[jax.ShapeDtypeStruct((hk, n), jnp.int32), jax.ShapeDtypeStruct((hk, n), F32)],
        compiler_params=_cparams("arbitrary"),
        name="peer_route",
    )(t, w_q.T, keys2)


GATHER_STRIDE = PEER_HEADS * PEER_TOPK + SUBLANES


def _split3(x):
    hi = x.astype(BF16)
    r1 = x - hi.astype(F32)
    mid = r1.astype(BF16)
    lo = (r1 - mid.astype(F32)).astype(BF16)
    return hi, mid, lo


def _gather_rows(idx_ref, tab_ref, g_ref, t, nk):
    rows = tab_ref.shape[1]
    for k in range(nk):
        g_ref[pl.ds(k, rows, stride=GATHER_STRIDE), :] = tab_ref[idx_ref[t, k]]


def _chunk_matrix(g_ref, c, nk):
    return pltpu.bitcast(g_ref[GATHER_STRIDE * c:GATHER_STRIDE * c + nk, :], BF16)


def _peer_u_kernel(idx_ref, tab_ref, th_ref, gate_ref, mask_ref, pool_ref, o_ref, g0_ref, g1_ref, s_ref,
                   *, tm, nk):
    nchunk = tab_ref.shape[1]
    nt = (((1,), (1,)), ((), ()))

    def token(t, g_ref):
        _gather_rows(idx_ref, tab_ref, g_ref, t, nk)
        parts = _split3(th_ref[t])
        s = jnp.zeros((SUBLANES, 2 * nk), F32)
        for c in range(nchunk):
            w = _chunk_matrix(g_ref, c, nk)
            out = sum(lax.dot_general(p, w, nt, preferred_element_type=F32) for p in parts)
            s = s + out * mask_ref[c]
        s_ref[pl.ds(t, 1), :] = jnp.sum(s, axis=0, keepdims=True)

    def pair(i, carry):
        token(2 * i, g0_ref)
        token(2 * i + 1, g1_ref)
        return carry

    lax.fori_loop(0, tm // 2, pair, 0)
    act = jnp.dot(s_ref[...], pool_ref[...], precision=HIGHEST, preferred_element_type=F32)
    o_ref[...] = _gelu(act) * gate_ref[...]


def _peer_v_kernel(idx_ref, tab_ref, a_ref, elo_ref, ehi_ref, o_ref, g0_ref, g1_ref, x_ref, *, tm, nk):
    nchunk = tab_ref.shape[1]
    for j, p in enumerate(_split3(a_ref[...])):
        x_ref[j] = jnp.dot(p, elo_ref[...], preferred_element_type=F32)
        x_ref[3 + j] = jnp.dot(p, ehi_ref[...], preferred_element_type=F32)

    def token(t, g_ref):
        _gather_rows(idx_ref, tab_ref, g_ref, t, nk)
        lhs = jnp.concatenate([x_ref[r, pl.ds(t, 1), :] for r in range(6)]
                              + [jnp.zeros((SUBLANES - 6, 2 * nk), F32)], axis=0).astype(BF16)
        for c in range(nchunk):
            out = jnp.dot(lhs, _chunk_matrix(g_ref, c, nk), preferred_element_type=F32)
            o_ref[t, pl.ds(c, 1), :] = out[0:1] + out[1:2] + out[2:3]
            o_ref[t, pl.ds(nchunk + c, 1), :] = out[3:4] + out[4:5] + out[5:6]

    def pair(i, carry):
        token(2 * i, g0_ref)
        token(2 * i + 1, g1_ref)
        return carry

    lax.fori_loop(0, tm // 2, pair, 0)


def _pack_table(tab):
    e, d = tab.shape
    half = d // 2
    b = lax.bitcast_convert_type(tab.astype(BF16), jnp.uint16).astype(jnp.uint32)
    w = b[:, :half] | (b[:, half:] << 16)
    return lax.bitcast_convert_type(w, jnp.int32).reshape(e, half // LANES, LANES)


def peer_experts(idx, gate, th, u_tab, v_tab):
    n, nk = idx.shape
    tm = PEER_TOK_TILE
    nchunk = th.shape[1] // 2
    up, vp = _pack_table(u_tab), _pack_table(v_tab)
    rho = np.arange(2 * nk)
    mask = np.zeros((nchunk, SUBLANES, 2 * nk), np.float32)
    for c in range(nchunk):
        mask[c, c, rho % 2 == 0] = 1.0
        mask[c, nchunk + c, rho % 2 == 1] = 1.0
    pool = (rho[:, None] // 2 == np.arange(nk)[None, :]).astype(np.float32)
    elo = (np.arange(nk)[:, None] * 2 == rho[None, :]).astype(np.float32)
    ehi = (np.arange(nk)[:, None] * 2 + 1 == rho[None, :]).astype(np.float32)
    smem = pl.BlockSpec((tm, nk), lambda i: (i, 0), memory_space=pltpu.SMEM)
    tab_spec = pl.BlockSpec(up.shape, lambda i: (0, 0, 0), pipeline_mode=pl.Buffered(1))
    row = pl.BlockSpec((tm, nk), lambda i: (i, 0))
    th_spec = pl.BlockSpec((tm,) + th.shape[1:], lambda i: (i, 0, 0))
    gather_tiles = [pltpu.VMEM((GATHER_STRIDE * nchunk, LANES), jnp.int32)] * 2
    act = pl.pallas_call(
        functools.partial(_peer_u_kernel, tm=tm, nk=nk),
        grid=(n // tm,),
        in_specs=[smem, tab_spec, th_spec, row, _full(mask.shape), _full(pool.shape)],
        out_specs=row,
        out_shape=jax.ShapeDtypeStruct((n, nk), F32),
        scratch_shapes=gather_tiles + [pltpu.VMEM((tm, 2 * nk), F32)],
        compiler_params=_cparams("arbitrary"),
        name="peer_u",
    )(idx, up, th, gate, jnp.asarray(mask), jnp.asarray(pool))
    return pl.pallas_call(
        functools.partial(_peer_v_kernel, tm=tm, nk=nk),
        grid=(n // tm,),
        in_specs=[smem, tab_spec, row, _full(elo.shape), _full(ehi.shape)],
        out_specs=th_spec,
        out_shape=jax.ShapeDtypeStruct(th.shape, F32),
        scratch_shapes=gather_tiles + [pltpu.VMEM((6, tm, 2 * nk), F32)],
        compiler_params=_cparams("arbitrary"),
        name="peer_v",
    )(idx, vp, act, jnp.asarray(elo, BF16), jnp.asarray(ehi, BF16))


def peer_channel(t, w_q, keys, u_tab, v_tab):
    n, d = t.shape
    e, g = peer_route(t, w_q, keys)
    f = peer_experts(e.T, g.T, t.reshape(n, d // LANES, LANES), u_tab, v_tab)
    return f.reshape(n, d)


def _mla_proj_kernel(h_ref, mod_ref, ck_ref, sk_ref, cq_ref, sq_ref, wdc_ref, wdr_ref, wdrs_ref, kvn_ref,
                     wuk_ref, wuv_ref, wdq_ref, qn_ref, wqn_ref, wqr_ref, wqrs_ref, k_ref, v_ref, q_ref, *, scale):
    m = (h_ref[...] * (1.0 + mod_ref[0, 1:2, :]) + mod_ref[0, 0:1, :]).astype(BF16)
    ckv = _rms_norm(jnp.dot(m, wdc_ref[...], preferred_element_type=F32), kvn_ref[...]).astype(BF16)
    kr = (jnp.dot(m, wdr_ref[...], preferred_element_type=F32) * ck_ref[...]
          + jnp.dot(m, wdrs_ref[...], preferred_element_type=F32) * sk_ref[...]).astype(BF16)
    kn = jnp.dot(ckv, wuk_ref[...], preferred_element_type=F32).astype(BF16)
    v = jnp.dot(ckv, wuv_ref[...], preferred_element_type=F32).astype(BF16)
    cq = _rms_norm(jnp.dot(m, wdq_ref[...], preferred_element_type=F32), qn_ref[...]).astype(BF16)
    qn = (jnp.dot(cq, wqn_ref[...], preferred_element_type=F32) * scale).astype(BF16)
    qr = ((jnp.dot(cq, wqr_ref[...], preferred_element_type=F32) * cq_ref[...]
           + jnp.dot(cq, wqrs_ref[...], preferred_element_type=F32) * sq_ref[...]) * scale).astype(BF16)
    for h in range(MLA_HEADS):
        k_ref[0, h, :, 0:MLA_NOPE] = kn[:, h * MLA_NOPE:(h + 1) * MLA_NOPE]
        k_ref[0, h, :, MLA_NOPE:] = kr
        v_ref[0, h] = v[:, h * MLA_V:(h + 1) * MLA_V]
        q_ref[0, h, :, 0:MLA_NOPE] = qn[:, h * MLA_NOPE:(h + 1) * MLA_NOPE]
        q_ref[0, h, :, MLA_NOPE:] = qr[:, h * MLA_ROPE:(h + 1) * MLA_ROPE]


def _swap_halves(w, width):
    r = w.reshape(w.shape[0], -1, 2, width // 2)
    return r[:, :, ::-1, :].reshape(w.shape)


def mla_project(x, mod_tiles, ctx_len, nbatch, w_dq, q_norm, w_uq, w_dkv, kv_norm, w_ukv):
    n, d = x.shape
    ltot = n // nbatch
    nlat = ltot - ctx_len
    tm = ROW_TILE
    tpb = ltot // tm
    hd = MLA_NOPE + MLA_ROPE
    kvl = kv_norm.shape[0]
    ql = q_norm.shape[0]
    pos = jnp.arange(nlat, dtype=jnp.int32)
    nf = MLA_ROPE // 4
    inv = ROPE_BASE ** (-jnp.arange(nf, dtype=F32) / nf)
    ang = jnp.concatenate([(pos // GRID_W).astype(F32)[:, None] * inv,
                           (pos % GRID_W).astype(F32)[:, None] * inv], axis=-1)
    cos = jnp.concatenate([jnp.ones((ctx_len, MLA_ROPE // 2), F32), jnp.cos(ang)], axis=0)
    sin = jnp.concatenate([jnp.zeros((ctx_len, MLA_ROPE // 2), F32), jnp.sin(ang)], axis=0)
    ck = jnp.concatenate([cos, cos], axis=1)
    sk = jnp.concatenate([-sin, sin], axis=1)
    cq = jnp.tile(ck, (1, MLA_HEADS))
    sq = jnp.tile(sk, (1, MLA_HEADS))
    w_dkv_c = w_dkv[:, :kvl].astype(BF16)
    w_dkv_r = w_dkv[:, kvl:]
    ukv = w_ukv.reshape(kvl, MLA_HEADS, MLA_NOPE + MLA_V)
    w_uk = ukv[:, :, :MLA_NOPE].reshape(kvl, -1).astype(BF16)
    w_uv = ukv[:, :, MLA_NOPE:].reshape(kvl, -1).astype(BF16)
    uq = w_uq.reshape(ql, MLA_HEADS, hd)
    w_qn = uq[:, :, :MLA_NOPE].reshape(ql, -1).astype(BF16)
    w_qr = uq[:, :, MLA_NOPE:].reshape(ql, -1)
    args = (x, mod_tiles, ck, sk, cq, sq, w_dkv_c, w_dkv_r.astype(BF16),
            _swap_halves(w_dkv_r, MLA_ROPE).astype(BF16), kv_norm.reshape(1, kvl), w_uk, w_uv,
            w_dq.astype(BF16), q_norm.reshape(1, ql), w_qn, w_qr.astype(BF16),
            _swap_halves(w_qr, MLA_ROPE).astype(BF16))
    pos_spec = lambda w: pl.BlockSpec((tm, w), lambda i: (i % tpb, 0))
    in_specs = [pl.BlockSpec((tm, d), lambda i: (i, 0)), pl.BlockSpec((1, N_MOD, d), lambda i: (i, 0, 0)),
                pos_spec(MLA_ROPE), pos_spec(MLA_ROPE), pos_spec(MLA_ROPE * MLA_HEADS),
                pos_spec(MLA_ROPE * MLA_HEADS)] + [_full(a.shape) for a in args[6:]]
    head_spec = lambda w: pl.BlockSpec((1, MLA_HEADS, tm, w), lambda i: (i // tpb, 0, i % tpb, 0))
    return pl.pallas_call(
        functools.partial(_mla_proj_kernel, scale=float(hd) ** -0.5),
        grid=(n // tm,),
        in_specs=in_specs,
        out_specs=[head_spec(hd), head_spec(MLA_V), head_spec(hd)],
        out_shape=[jax.ShapeDtypeStruct((nbatch, MLA_HEADS, ltot, hd), BF16),
                   jax.ShapeDtypeStruct((nbatch, MLA_HEADS, ltot, MLA_V), BF16),
                   jax.ShapeDtypeStruct((nbatch, MLA_HEADS, ltot, hd), BF16)],
        compiler_params=_cparams("arbitrary"),
        name="mla_project",
    )(*args)


def _attn_kernel(q_ref, k_ref, v_ref, o_ref, *, tk, nchunks):
    q = q_ref[0, 0]
    tq = q.shape[0]

    def chunk(c, carry):
        m_i, l_i, acc = carry
        off = pl.multiple_of(c * tk, tk)
        k = k_ref[0, 0, pl.ds(off, tk), :]
        s = lax.dot_general(q, k, (((1,), (1,)), ((), ())), preferred_element_type=F32)
        m_new = jnp.maximum(m_i, jnp.max(s, axis=-1, keepdims=True))
        alpha = jnp.exp(m_i - m_new)
        p = jnp.exp(s - m_new)
        l_new = alpha * l_i + jnp.sum(p, axis=-1, keepdims=True)
        acc_new = alpha * acc + jnp.dot(p.astype(BF16), v_ref[0, 0, pl.ds(off, tk), :],
                                        preferred_element_type=F32)
        return m_new, l_new, acc_new

    init = (jnp.full((tq, 1), -jnp.inf, F32), jnp.zeros((tq, 1), F32), jnp.zeros((tq, MLA_V), F32))
    _, l_i, acc = lax.fori_loop(0, nchunks, chunk, init)
    o_ref[0] = acc / l_i


def mla_attend(q, k, v, ctx_len):
    nb, nh, ltot, hd = k.shape
    tq = ROW_TILE
    assert ctx_len % tq == 0
    skip = ctx_len // tq
    nlat = ltot - ctx_len
    tk = ATT_KV_CHUNK
    assert ltot % tk == 0
    return pl.pallas_call(
        functools.partial(_attn_kernel, tk=tk, nchunks=ltot // tk),
        grid=(nb, nh, nlat // tq),
        in_specs=[pl.BlockSpec((1, 1, tq, hd), lambda b, h, i: (b, h, i + skip, 0)),
                  pl.BlockSpec((1, 1, ltot, hd), lambda b, h, i: (b, h, 0, 0)),
                  pl.BlockSpec((1, 1, ltot, MLA_V), lambda b, h, i: (b, h, 0, 0))],
        out_specs=pl.BlockSpec((1, tq, MLA_V), lambda b, h, i: (b, i, h)),
        out_shape=jax.ShapeDtypeStruct((nb, nlat, nh * MLA_V), F32),
        compiler_params=_cparams("arbitrary", "arbitrary", "arbitrary"),
        name="mla_attend",
    )(q, k, v)


def kernel(x, c, ctx, c_ctx, ada_w, ada_b, ln_g, ln_b, s5_a_re, s5_a_im, s5_log_dt, s5_b_re, s5_b_im, s5_c_re, s5_c_im, s5_d, s5_w_glu, s5_w_o, mla_w_dq, mla_q_norm, mla_w_uq, mla_w_dkv, mla_kv_norm, mla_w_ukv, mla_w_o, peer_w_q, peer_keys, peer_u, peer_v):
    nb, nlat, d = x.shape
    nctx = ctx.shape[1]
    ltot = nctx + nlat
    depth = ada_w.shape[0]
    alpha = (2 * depth) ** 0.25
    tm = ROW_TILE

    cvec = jnp.zeros((SUBLANES, d), F32).at[:nb].set(c).at[nb].set(c_ctx)
    tiles_per_batch = ltot // tm
    tile_rows_all = np.array([nb if j < nctx // tm else b for b in range(nb) for j in range(tiles_per_batch)])
    tile_rows_lat = np.array([b for b in range(nb) for _ in range(nlat // tm)])
    seg_rows = np.array([[nb] * nb, list(range(nb))])

    h_all = jnp.concatenate([ctx, x], axis=1).reshape(nb * ltot, d)

    mod = ada_mod(cvec, ada_w[0], ada_b[0]).reshape(SUBLANES, N_MOD, d)
    mod_all = mod[tile_rows_all]
    h3 = h_all.reshape(nb, ltot, d)
    xb = jnp.flip(jnp.concatenate([h3[:, nctx:], h3[:, :nctx]], axis=1), axis=1)
    bfc, ccc, ar, ai = s5_prepare(s5_a_re[0], s5_a_im[0], s5_log_dt[0], s5_b_re[0], s5_b_im[0],
                                  s5_c_re[0], s5_c_im[0], nb)
    yf, yb = s5_scan(h3, xb, mod[seg_rows], bfc, ccc, ar, ai, nctx)
    yb = jnp.flip(yb, axis=1)
    yb = jnp.concatenate([yb[:, nlat:], yb[:, :nlat]], axis=1)
    h1, t = post_mixer(h_all, (yf.reshape(-1, d), yb.reshape(-1, d)), mod_all, s5_w_o[0], ln_g[0, 0], ln_b[0, 0],
                       alpha, d_skip=s5_d[0], w_glu=s5_w_glu[0])
    f = peer_channel(t, peer_w_q[0], peer_keys[0], peer_u[0], peer_v[0])
    h_all = residual_ln(h1, f, mod_all, ln_g[0, 1], ln_b[0, 1], alpha)

    mod = ada_mod(cvec, ada_w[1], ada_b[1]).reshape(SUBLANES, N_MOD, d)
    k, v, q = mla_project(h_all, mod[tile_rows_all], nctx, nb, mla_w_dq[0], mla_q_norm[0], mla_w_uq[0],
                          mla_w_dkv[0], mla_kv_norm[0], mla_w_ukv[0])
    att = mla_attend(q, k, v, nctx).reshape(nb * nlat, -1)
    h_lat = h_all.reshape(nb, ltot, d)[:, nctx:].reshape(nb * nlat, d)
    mod_lat = mod[tile_rows_lat]
    h1, t = post_mixer(h_lat, (att,), mod_lat, mla_w_o[0], ln_g[1, 0], ln_b[1, 0], alpha)
    f = peer_channel(t, peer_w_q[1], peer_keys[1], peer_u[1], peer_v[1])
    out = residual_ln(h1, f, mod_lat, ln_g[1, 1], ln_b[1, 1], alpha)
    return out.reshape(nb, nlat, d).astype(x.dtype)
```

```python
import functools
import math

import jax
import jax.numpy as jnp
import numpy as np
from jax import lax
from jax.experimental import pallas as pl
from jax.experimental.pallas import tpu as pltpu

F32 = jnp.float32
BF16 = jnp.bfloat16
HIGHEST = lax.Precision.HIGHEST

LANES = 128
SUBLANES = 8
VMEM_LIMIT_BYTES = 56 * 1024 * 1024

N_MOD = 6
GRID_W = 64
S5_GROUP = 16
S5_STATE = 64
MLA_HEADS = 8
MLA_NOPE = 128
MLA_ROPE = 64
MLA_V = 128
ROPE_BASE = 10000.0
PEER_HEADS = 8
PEER_NKEYS = 128
PEER_DK = 128
PEER_TOPK = 16
LN_EPS = 1e-5
RMS_EPS = 1e-6

ROW_TILE = 256
S5_CHUNK = 128
PEER_TOK_TILE = 32
ATT_KV_CHUNK = 768


def _cparams(*sem, flags=None):
    return pltpu.CompilerParams(dimension_semantics=sem, vmem_limit_bytes=VMEM_LIMIT_BYTES, flags=flags)


def _full(shape):
    n = len(shape)
    return pl.BlockSpec(shape, lambda *_: (0,) * n)


def _gelu(x):
    return 0.5 * x * (1.0 + lax.erf(x * (1.0 / math.sqrt(2.0))))


def _layer_norm(x, g, b):
    mu = jnp.mean(x, axis=-1, keepdims=True)
    xc = x - mu
    var = jnp.mean(xc * xc, axis=-1, keepdims=True)
    return xc * lax.rsqrt(var + LN_EPS) * g + b


def _rms_norm(x, g):
    return x * lax.rsqrt(jnp.mean(x * x, axis=-1, keepdims=True) + RMS_EPS) * g


def _bdot(a, b):
    return jnp.dot(a.astype(BF16), b, preferred_element_type=F32)


def _ada_kernel(c_ref, w_ref, b_ref, o_ref):
    c = c_ref[...]
    s = c * jax.nn.sigmoid(c)
    o_ref[...] = jnp.dot(s, w_ref[...], precision=HIGHEST, preferred_element_type=F32) + b_ref[...]


def ada_mod(cvec, w, b):
    d = cvec.shape[1]
    n = w.shape[1]
    tn = 1024
    return pl.pallas_call(
        _ada_kernel,
        grid=(n // tn,),
        in_specs=[_full((SUBLANES, d)), pl.BlockSpec((d, tn), lambda j: (0, j)),
                  pl.BlockSpec((1, tn), lambda j: (0, j))],
        out_specs=pl.BlockSpec((SUBLANES, tn), lambda j: (0, j)),
        out_shape=jax.ShapeDtypeStruct((SUBLANES, n), F32),
        compiler_params=_cparams("arbitrary"),
        name="ada_mod",
    )(cvec, w, b.reshape(1, n))


def _s5_kernel(xf_ref, xb_ref, mod_ref, bf_ref, cc_ref, ar_ref, ai_ref, yf_ref, yb_ref, bu_ref, x_ref,
               *, chunk, nbatch):
    nlc = bf_ref.shape[2]
    spc = bf_ref.shape[4] // LANES

    @pl.when(pl.program_id(0) == 0)
    def _():
        x_ref[...] = jnp.zeros_like(x_ref)

    for q in range(2 * nbatch):
        d, b = divmod(q, nbatch)
        src = xf_ref if d == 0 else xb_ref
        u = (src[b] * (1.0 + mod_ref[0, b, 1:2, :]) + mod_ref[0, b, 0:1, :]).astype(BF16)
        for part in range(2):
            r = q + 2 * nbatch * part
            for j in range(nlc):
                res = jnp.dot(u[:, LANES * j:LANES * (j + 1)], bf_ref[d, part, j], preferred_element_type=F32)
                for k in range(spc):
                    bu_ref[spc * j + k, pl.ds(r, chunk, stride=SUBLANES), :] = res[:, LANES * k:LANES * (k + 1)]

    ar = ar_ref[...]
    ai = ai_ref[...]

    def step(s, x):
        off = pl.multiple_of(s * SUBLANES, SUBLANES)
        xn = ar * x + ai * pltpu.roll(x, SUBLANES // 2, axis=1) + bu_ref[:, pl.ds(off, SUBLANES), :]
        bu_ref[:, pl.ds(off, SUBLANES), :] = xn
        return xn

    x_ref[...] = lax.fori_loop(0, chunk, step, x_ref[...])

    def state_rows(r, j):
        return jnp.concatenate([bu_ref[spc * j + k, pl.ds(r, chunk, stride=SUBLANES), :] for k in range(spc)],
                               axis=1).astype(BF16)

    for q in range(2 * nbatch):
        d, b = divmod(q, nbatch)
        out = yf_ref if d == 0 else yb_ref
        for j in range(nlc):
            y = (jnp.dot(state_rows(q, j), cc_ref[d, 0, j], preferred_element_type=F32)
                 + jnp.dot(state_rows(q + 2 * nbatch, j), cc_ref[d, 1, j], preferred_element_type=F32))
            out[b, :, LANES * j:LANES * (j + 1)] = y


def s5_scan(xf, xb, mod_seg, bfc, ccc, ar, ai, ctx_len):
    nb, ltot, d = xf.shape
    assert 2 * nb * 2 == SUBLANES, "state rows must fill one sublane tile"
    chunk = S5_CHUNK
    assert ltot % chunk == 0 and ctx_len % chunk == 0
    ctx_chunks = ctx_len // chunk
    nslab = ar.shape[0]
    seq_spec = pl.BlockSpec((nb, chunk, d), lambda c: (0, c, 0))
    return pl.pallas_call(
        functools.partial(_s5_kernel, chunk=chunk, nbatch=nb),
        grid=(ltot // chunk,),
        in_specs=[seq_spec, seq_spec,
                  pl.BlockSpec((1, nb, N_MOD, d), lambda c: (jnp.minimum(c // ctx_chunks, 1), 0, 0, 0)),
                  _full(bfc.shape), _full(ccc.shape), _full(ar.shape), _full(ai.shape)],
        out_specs=[seq_spec, seq_spec],
        out_shape=[jax.ShapeDtypeStruct(xf.shape, F32)] * 2,
        scratch_shapes=[pltpu.VMEM((nslab, SUBLANES * chunk, LANES), F32),
                        pltpu.VMEM((nslab, SUBLANES, LANES), F32)],
        compiler_params=_cparams("arbitrary"),
        name="s5_scan",
    )(xf, xb, mod_seg, bfc, ccc, ar, ai)


def s5_prepare(a_re, a_im, log_dt, b_re, b_im, c_re, c_im, nbatch):
    ndir, g, p = a_re.shape
    gc = b_re.shape[-1]
    gpc = LANES // gc
    nlc = g // gpc
    dt = jnp.exp(log_dt.astype(F32))[..., None]
    a_re = a_re.astype(F32)
    a_im = a_im.astype(F32)
    mag = jnp.exp(a_re * dt)
    ab_re = mag * jnp.cos(a_im * dt)
    ab_im = mag * jnp.sin(a_im * dt)
    nr, ni = ab_re - 1.0, ab_im
    den = a_re * a_re + a_im * a_im
    f_re = (nr * a_re + ni * a_im) / den
    f_im = (ni * a_re - nr * a_im) / den
    bf_re = f_re[..., None] * b_re - f_im[..., None] * b_im
    bf_im = f_re[..., None] * b_im + f_im[..., None] * b_re
    bf = jnp.stack([bf_re, bf_im], axis=1).reshape(ndir, 2, nlc, gpc, p, gc)
    eye = jnp.eye(gpc, dtype=F32)
    bfc = jnp.einsum('dqjgpc,gh->dqjgchp', bf, eye).reshape(ndir, 2, nlc, gpc * gc, gpc * p).astype(BF16)
    cc = jnp.stack([c_re, -c_im], axis=1).astype(F32).reshape(ndir, 2, nlc, gpc, gc, p)
    ccc = jnp.einsum('dqjgcp,gh->dqjgphc', cc, eye).reshape(ndir, 2, nlc, gpc * p, gpc * gc).astype(BF16)
    rows_r, rows_i = [], []
    for part in range(2):
        for d in range(ndir):
            for _ in range(nbatch):
                rows_r.append(ab_re[d].reshape(-1))
                rows_i.append(ab_im[d].reshape(-1) * (-1.0 if part == 0 else 1.0))
    slabs = lambda rows: jnp.stack(rows).reshape(len(rows), -1, LANES).transpose(1, 0, 2)
    return bfc, ccc, slabs(rows_r), slabs(rows_i)


def _post_kernel(*refs, alpha, glu):
    if glu:
        h_ref, yf_ref, yb_ref, mod_ref, dsk_ref, wg_ref, wo_ref, lng_ref, lnb_ref, h1_ref, t_ref = refs
        h = h_ref[...]
        m = h * (1.0 + mod_ref[0, 1:2, :]) + mod_ref[0, 0:1, :]
        z = _gelu(yf_ref[...] + yb_ref[...] + dsk_ref[...] * m)
        z = z * jax.nn.sigmoid(_bdot(z, wg_ref[...]))
    else:
        h_ref, z_ref, mod_ref, wo_ref, lng_ref, lnb_ref, h1_ref, t_ref = refs
        h = h_ref[...]
        z = z_ref[...]
    o = _bdot(z, wo_ref[...])
    h1 = _layer_norm(alpha * h + mod_ref[0, 2:3, :] * o, lng_ref[...], lnb_ref[...])
    h1_ref[...] = h1
    t_ref[...] = h1 * (1.0 + mod_ref[0, 4:5, :]) + mod_ref[0, 3:4, :]


def post_mixer(h, pre, mod_tiles, w_o, ln_g, ln_b, alpha, d_skip=None, w_glu=None):
    n, d = h.shape
    tm = ROW_TILE
    glu = w_glu is not None
    row = pl.BlockSpec((tm, d), lambda i: (i, 0))
    vec = _full((1, d))
    modspec = pl.BlockSpec((1, N_MOD, d), lambda i: (i, 0, 0))
    if glu:
        args = (h, pre[0], pre[1], mod_tiles, d_skip.reshape(1, d), w_glu.astype(BF16), w_o.astype(BF16),
                ln_g.reshape(1, d), ln_b.reshape(1, d))
        in_specs = [row, row, row, modspec, vec, _full((d, d)), _full((d, d)), vec, vec]
    else:
        args = (h, pre[0], mod_tiles, w_o.astype(BF16), ln_g.reshape(1, d), ln_b.reshape(1, d))
        in_specs = [row, pl.BlockSpec((tm, pre[0].shape[1]), lambda i: (i, 0)), modspec,
                    _full(w_o.shape), vec, vec]
    return pl.pallas_call(
        functools.partial(_post_kernel, alpha=alpha, glu=glu),
        grid=(n // tm,),
        in_specs=in_specs,
        out_specs=[row, row],
        out_shape=[jax.ShapeDtypeStruct((n, d), F32)] * 2,
        compiler_params=_cparams("arbitrary"),
        name="post_mixer_glu" if glu else "post_mixer",
    )(*args)


def _res_ln_kernel(h_ref, f_ref, mod_ref, lng_ref, lnb_ref, o_ref, *, alpha):
    o_ref[...] = _layer_norm(alpha * h_ref[...] + mod_ref[0, 5:6, :] * f_ref[...], lng_ref[...], lnb_ref[...])


def residual_ln(h, f, mod_tiles, ln_g, ln_b, alpha):
    n, d = h.shape
    tm = ROW_TILE
    row = pl.BlockSpec((tm, d), lambda i: (i, 0))
    vec = _full((1, d))
    return pl.pallas_call(
        functools.partial(_res_ln_kernel, alpha=alpha),
        grid=(n // tm,),
        in_specs=[row, row, pl.BlockSpec((1, N_MOD, d), lambda i: (i, 0, 0)), vec, vec],
        out_specs=row,
        out_shape=jax.ShapeDtypeStruct((n, d), F32),
        compiler_params=_cparams("arbitrary"),
        name="residual_ln",
    )(h, f, mod_tiles, ln_g.reshape(1, d), ln_b.reshape(1, d))


def _topk_rows(s, k, payload=None):
    nrow = s.shape[0]
    iota = lax.broadcasted_iota(jnp.int32, s.shape, 0).astype(F32)
    vals, idxs, pays = [], [], []
    for _ in range(k):
        m = jnp.max(s, axis=0, keepdims=True)
        i = jnp.min(jnp.where(s == m, iota, float(nrow)), axis=0, keepdims=True)
        hit = iota == i
        vals.append(m)
        idxs.append(i)
        if payload is not None:
            pays.append(jnp.max(jnp.where(hit, payload, -1.0), axis=0, keepdims=True))
        s = jnp.where(hit, -jnp.inf, s)
    return vals, idxs, pays


def _route_kernel(t_ref, wqt_ref, keys_ref, e_ref, g_ref):
    nt = (((1,), (1,)), ((), ()))
    qt = lax.dot_general(wqt_ref[...], t_ref[...], nt, precision=HIGHEST, preferred_element_type=F32)
    half = PEER_DK // 2
    for h in range(PEER_HEADS):
        sv, si = [], []
        for s in range(2):
            row0 = (h * 2 + s) * half
            sc = jnp.dot(keys_ref[h * 2 + s], qt[row0:row0 + half, :], precision=HIGHEST,
                         preferred_element_type=F32)
            v, i, _ = _topk_rows(sc, PEER_TOPK)
            sv.append(v)
            si.append(i)
        v2 = jnp.concatenate(sv[1], axis=0)
        i2 = jnp.concatenate(si[1], axis=0)
        cand = jnp.concatenate([sv[0][a] + v2 for a in range(PEER_TOPK)], axis=0)
        cexp = jnp.concatenate([si[0][a] * float(PEER_NKEYS) + i2 for a in range(PEER_TOPK)], axis=0)
        cv, _, ce = _topk_rows(cand, PEER_TOPK, payload=cexp)
        cvm = jnp.concatenate(cv, axis=0)
        ex = jnp.exp(cvm - cv[0])
        lo = h * PEER_TOPK
        g_ref[lo:lo + PEER_TOPK, :] = ex / jnp.sum(ex, axis=0, keepdims=True)
        e_ref[lo:lo + PEER_TOPK, :] = jnp.concatenate(ce, axis=0).astype(jnp.int32)


def peer_route(t, w_q, keys):
    n, d = t.shape
    tm = ROW_TILE
    hk = PEER_HEADS * PEER_TOPK
    keys2 = keys.reshape(PEER_HEADS * 2, PEER_NKEYS, PEER_DK // 2)
    out = pl.BlockSpec((hk, tm), lambda i: (0, i))
    return pl.pallas_call(
        _route_kernel,
        grid=(n // tm,),
        in_specs=[pl.BlockSpec((tm, d), lambda i: (i, 0)), _full((w_q.shape[1], d)), _full(keys2.shape)],
        out_specs=[out, out],
        out_shape=[jax.ShapeDtypeStruct((hk, n), jnp.int32), jax.ShapeDtypeStruct((hk, n), F32)],
        compiler_params=_cparams("arbitrary"),
        name="peer_route",
    )(t, w_q.T, keys2)


GATHER_STRIDE = PEER_HEADS * PEER_TOPK + SUBLANES


PEER_ORDER = (("D", 0), ("G", 0), ("D", 1), ("G", 1), ("D", 2), ("G", 2), ("D", 3), ("G", 3),
              ("G", 4), ("G", 5), ("G", 6), ("G", 7))


def _split3(x):
    hi = x.astype(BF16)
    r1 = x - hi.astype(F32)
    mid = r1.astype(BF16)
    lo = (r1 - mid.astype(F32)).astype(BF16)
    return hi, mid, lo


def _gather_rows(idx_ref, tab_ref, g_ref, t, k0, k1):
    rows = tab_ref.shape[1]
    ids = idx_ref.at[t]
    for k in range(k0, k1):
        g_ref[pl.ds(k, rows, stride=GATHER_STRIDE), :] = tab_ref[ids[k]]


def _chunk_matrix(g_ref, c, nk):
    return pltpu.bitcast(g_ref[GATHER_STRIDE * c:GATHER_STRIDE * c + nk, :], BF16)


def _token_pipeline(idx_ref, tab_ref, tm, nk, compute_chunk, finish, g0_ref, g1_ref):
    nchunk = tab_ref.shape[1]
    nparts = len([o for o in PEER_ORDER if o[0] == "G"])
    per = nk // nparts
    _gather_rows(idx_ref, tab_ref, g0_ref, 0, 0, nk)

    def token(t, cur_ref, nxt_ref):
        nxt = jnp.minimum(t + 1, tm - 1)
        acc = None
        for kind, j in PEER_ORDER:
            if kind == "D":
                acc = compute_chunk(t, j, _chunk_matrix(cur_ref, j, nk), acc)
            else:
                _gather_rows(idx_ref, tab_ref, nxt_ref, nxt, j * per, (j + 1) * per)
        finish(t, acc)

    def step(t, carry):
        @pl.when(t % 2 == 0)
        def _():
            token(t, g0_ref, g1_ref)

        @pl.when(t % 2 == 1)
        def _():
            token(t, g1_ref, g0_ref)

        return carry

    lax.fori_loop(0, tm, step, 0)


def _peer_u_kernel(idx_ref, tab_ref, th_ref, gate_ref, mask_ref, pool_ref, o_ref, g0_ref, g1_ref, s_ref,
                   *, tm, nk):
    nchunk = tab_ref.shape[1]
    nt = (((1,), (1,)), ((), ()))

    def compute_chunk(t, c, w, acc):
        if acc is None:
            x = th_ref[t]
            hi = x.astype(BF16).astype(F32)
            lhs = jnp.concatenate([hi, x - hi], axis=0).astype(BF16)
            acc = (lhs, jnp.zeros((SUBLANES, 2 * nk), F32))
        lhs, s = acc
        out = lax.dot_general(lhs, w, nt, preferred_element_type=F32)
        return lhs, s + (out[:SUBLANES] + out[SUBLANES:]) * mask_ref[c]

    def finish(t, acc):
        s_ref[pl.ds(t, 1), :] = jnp.sum(acc[1], axis=0, keepdims=True)

    _token_pipeline(idx_ref, tab_ref, tm, nk, compute_chunk, finish, g0_ref, g1_ref)
    act = jnp.dot(s_ref[...], pool_ref[...], precision=HIGHEST, preferred_element_type=F32)
    o_ref[...] = _gelu(act) * gate_ref[...]


def _peer_v_kernel(idx_ref, tab_ref, a_ref, elo_ref, ehi_ref, o_ref, g0_ref, g1_ref, x_ref, *, tm, nk):
    nchunk = tab_ref.shape[1]
    for j, p in enumerate(_split3(a_ref[...])):
        x_ref[j] = jnp.dot(p, elo_ref[...], preferred_element_type=F32)
        x_ref[3 + j] = jnp.dot(p, ehi_ref[...], preferred_element_type=F32)

    def compute_chunk(t, c, w, lhs):
        if lhs is None:
            lhs = jnp.concatenate([x_ref[r, pl.ds(t, 1), :] for r in range(6)]
                                  + [jnp.zeros((SUBLANES - 6, 2 * nk), F32)], axis=0).astype(BF16)
        out = jnp.dot(lhs, w, preferred_element_type=F32)
        o_ref[t, pl.ds(c, 1), :] = out[0:1] + out[1:2] + out[2:3]
        o_ref[t, pl.ds(nchunk + c, 1), :] = out[3:4] + out[4:5] + out[5:6]
        return lhs

    _token_pipeline(idx_ref, tab_ref, tm, nk, compute_chunk, lambda t, lhs: None, g0_ref, g1_ref)


def _pack_table(tab):
    e, d = tab.shape
    half = d // 2
    b = lax.bitcast_convert_type(tab.astype(BF16), jnp.uint16).astype(jnp.uint32)
    w = b[:, :half] | (b[:, half:] << 16)
    return lax.bitcast_convert_type(w, jnp.int32).reshape(e, half // LANES, LANES)


def peer_experts(idx, gate, th, u_tab, v_tab):
    n, nk = idx.shape
    tm = PEER_TOK_TILE
    nchunk = th.shape[1] // 2
    up, vp = _pack_table(u_tab), _pack_table(v_tab)
    rho = np.arange(2 * nk)
    mask = np.zeros((nchunk, SUBLANES, 2 * nk), np.float32)
    for c in range(nchunk):
        mask[c, c, rho % 2 == 0] = 1.0
        mask[c, nchunk + c, rho % 2 == 1] = 1.0
    pool = (rho[:, None] // 2 == np.arange(nk)[None, :]).astype(np.float32)
    elo = (np.arange(nk)[:, None] * 2 == rho[None, :]).astype(np.float32)
    ehi = (np.arange(nk)[:, None] * 2 + 1 == rho[None, :]).astype(np.float32)
    smem = pl.BlockSpec((tm, nk), lambda i: (i, 0), memory_space=pltpu.SMEM)
    tab_spec = pl.BlockSpec(up.shape, lambda i: (0, 0, 0), pipeline_mode=pl.Buffered(1))
    row = pl.BlockSpec((tm, nk), lambda i: (i, 0))
    th_spec = pl.BlockSpec((tm,) + th.shape[1:], lambda i: (i, 0, 0))
    gather_tiles = [pltpu.VMEM((GATHER_STRIDE * nchunk, LANES), jnp.int32)] * 2
    act = pl.pallas_call(
        functools.partial(_peer_u_kernel, tm=tm, nk=nk),
        grid=(n // tm,),
        in_specs=[smem, tab_spec, th_spec, row, _full(mask.shape), _full(pool.shape)],
        out_specs=row,
        out_shape=jax.ShapeDtypeStruct((n, nk), F32),
        scratch_shapes=gather_tiles + [pltpu.VMEM((tm, 2 * nk), F32)],
        compiler_params=_cparams("arbitrary"),
        name="peer_u",
    )(idx, up, th, gate, jnp.asarray(mask), jnp.asarray(pool))
    return pl.pallas_call(
        functools.partial(_peer_v_kernel, tm=tm, nk=nk),
        grid=(n // tm,),
        in_specs=[smem, tab_spec, row, _full(elo.shape), _full(ehi.shape)],
        out_specs=th_spec,
        out_shape=jax.ShapeDtypeStruct(th.shape, F32),
        scratch_shapes=gather_tiles + [pltpu.VMEM((6, tm, 2 * nk), F32)],
        compiler_params=_cparams("arbitrary"),
        name="peer_v",
    )(idx, vp, act, jnp.asarray(elo, BF16), jnp.asarray(ehi, BF16))


def peer_channel(t, w_q, keys, u_tab, v_tab):
    n, d = t.shape
    e, g = peer_route(t, w_q, keys)
    f = peer_experts(e.T, g.T, t.reshape(n, d // LANES, LANES), u_tab, v_tab)
    return f.reshape(n, d)


def _mla_proj_kernel(h_ref, mod_ref, ck_ref, sk_ref, cq_ref, sq_ref, wdc_ref, wdr_ref, wdrs_ref, kvn_ref,
                     wuk_ref, wuv_ref, wdq_ref, qn_ref, wqn_ref, wqr_ref, wqrs_ref, k_ref, v_ref, q_ref, *, scale):
    m = (h_ref[...] * (1.0 + mod_ref[0, 1:2, :]) + mod_ref[0, 0:1, :]).astype(BF16)
    ckv = _rms_norm(jnp.dot(m, wdc_ref[...], preferred_element_type=F32), kvn_ref[...]).astype(BF16)
    kr = (jnp.dot(m, wdr_ref[...], preferred_element_type=F32) * ck_ref[...]
          + jnp.dot(m, wdrs_ref[...], preferred_element_type=F32) * sk_ref[...]).astype(BF16)
    kn = jnp.dot(ckv, wuk_ref[...], preferred_element_type=F32).astype(BF16)
    v = jnp.dot(ckv, wuv_ref[...], preferred_element_type=F32).astype(BF16)
    cq = _rms_norm(jnp.dot(m, wdq_ref[...], preferred_element_type=F32), qn_ref[...]).astype(BF16)
    qn = (jnp.dot(cq, wqn_ref[...], preferred_element_type=F32) * scale).astype(BF16)
    qr = ((jnp.dot(cq, wqr_ref[...], preferred_element_type=F32) * cq_ref[...]
           + jnp.dot(cq, wqrs_ref[...], preferred_element_type=F32) * sq_ref[...]) * scale).astype(BF16)
    for h in range(MLA_HEADS):
        k_ref[0, h, :, 0:MLA_NOPE] = kn[:, h * MLA_NOPE:(h + 1) * MLA_NOPE]
        k_ref[0, h, :, MLA_NOPE:] = kr
        v_ref[0, h] = v[:, h * MLA_V:(h + 1) * MLA_V]
        q_ref[0, h, :, 0:MLA_NOPE] = qn[:, h * MLA_NOPE:(h + 1) * MLA_NOPE]
        q_ref[0, h, :, MLA_NOPE:] = qr[:, h * MLA_ROPE:(h + 1) * MLA_ROPE]


def _swap_halves(w, width):
    r = w.reshape(w.shape[0], -1, 2, width // 2)
    return r[:, :, ::-1, :].reshape(w.shape)


def mla_project(x, mod_tiles, ctx_len, nbatch, w_dq, q_norm, w_uq, w_dkv, kv_norm, w_ukv):
    n, d = x.shape
    ltot = n // nbatch
    nlat = ltot - ctx_len
    tm = ROW_TILE
    tpb = ltot // tm
    hd = MLA_NOPE + MLA_ROPE
    kvl = kv_norm.shape[0]
    ql = q_norm.shape[0]
    pos = jnp.arange(nlat, dtype=jnp.int32)
    nf = MLA_ROPE // 4
    inv = ROPE_BASE ** (-jnp.arange(nf, dtype=F32) / nf)
    ang = jnp.concatenate([(pos // GRID_W).astype(F32)[:, None] * inv,
                           (pos % GRID_W).astype(F32)[:, None] * inv], axis=-1)
    cos = jnp.concatenate([jnp.ones((ctx_len, MLA_ROPE // 2), F32), jnp.cos(ang)], axis=0)
    sin = jnp.concatenate([jnp.zeros((ctx_len, MLA_ROPE // 2), F32), jnp.sin(ang)], axis=0)
    ck = jnp.concatenate([cos, cos], axis=1)
    sk = jnp.concatenate([-sin, sin], axis=1)
    cq = jnp.tile(ck, (1, MLA_HEADS))
    sq = jnp.tile(sk, (1, MLA_HEADS))
    w_dkv_c = w_dkv[:, :kvl].astype(BF16)
    w_dkv_r = w_dkv[:, kvl:]
    ukv = w_ukv.reshape(kvl, MLA_HEADS, MLA_NOPE + MLA_V)
    w_uk = ukv[:, :, :MLA_NOPE].reshape(kvl, -1).astype(BF16)
    w_uv = ukv[:, :, MLA_NOPE:].reshape(kvl, -1).astype(BF16)
    uq = w_uq.reshape(ql, MLA_HEADS, hd)
    w_qn = uq[:, :, :MLA_NOPE].reshape(ql, -1).astype(BF16)
    w_qr = uq[:, :, MLA_NOPE:].reshape(ql, -1)
    args = (x, mod_tiles, ck, sk, cq, sq, w_dkv_c, w_dkv_r.astype(BF16),
            _swap_halves(w_dkv_r, MLA_ROPE).astype(BF16), kv_norm.reshape(1, kvl), w_uk, w_uv,
            w_dq.astype(BF16), q_norm.reshape(1, ql), w_qn, w_qr.astype(BF16),
            _swap_halves(w_qr, MLA_ROPE).astype(BF16))
    pos_spec = lambda w: pl.BlockSpec((tm, w), lambda i: (i % tpb, 0))
    in_specs = [pl.BlockSpec((tm, d), lambda i: (i, 0)), pl.BlockSpec((1, N_MOD, d), lambda i: (i, 0, 0)),
                pos_spec(MLA_ROPE), pos_spec(MLA_ROPE), pos_spec(MLA_ROPE * MLA_HEADS),
                pos_spec(MLA_ROPE * MLA_HEADS)] + [_full(a.shape) for a in args[6:]]
    head_spec = lambda w: pl.BlockSpec((1, MLA_HEADS, tm, w), lambda i: (i // tpb, 0, i % tpb, 0))
    return pl.pallas_call(
        functools.partial(_mla_proj_kernel, scale=float(hd) ** -0.5),
        grid=(n // tm,),
        in_specs=in_specs,
        out_specs=[head_spec(hd), head_spec(MLA_V), head_spec(hd)],
        out_shape=[jax.ShapeDtypeStruct((nbatch, MLA_HEADS, ltot, hd), BF16),
                   jax.ShapeDtypeStruct((nbatch, MLA_HEADS, ltot, MLA_V), BF16),
                   jax.ShapeDtypeStruct((nbatch, MLA_HEADS, ltot, hd), BF16)],
        compiler_params=_cparams("arbitrary"),
        name="mla_project",
    )(*args)


def _attn_kernel(q_ref, k_ref, v_ref, o_ref, *, tk, nchunks):
    q = q_ref[0, 0]
    tq = q.shape[0]

    def chunk(c, carry):
        m_i, l_i, acc = carry
        off = pl.multiple_of(c * tk, tk)
        k = k_ref[0, 0, pl.ds(off, tk), :]
        s = lax.dot_general(q, k, (((1,), (1,)), ((), ())), preferred_element_type=F32)
        m_new = jnp.maximum(m_i, jnp.max(s, axis=-1, keepdims=True))
        alpha = jnp.exp(m_i - m_new)
        p = jnp.exp(s - m_new)
        l_new = alpha * l_i + jnp.sum(p, axis=-1, keepdims=True)
        acc_new = alpha * acc + jnp.dot(p.astype(BF16), v_ref[0, 0, pl.ds(off, tk), :],
                                        preferred_element_type=F32)
        return m_new, l_new, acc_new

    init = (jnp.full((tq, 1), -jnp.inf, F32), jnp.zeros((tq, 1), F32), jnp.zeros((tq, MLA_V), F32))
    _, l_i, acc = lax.fori_loop(0, nchunks, chunk, init)
    o_ref[0] = acc / l_i


def mla_attend(q, k, v, ctx_len):
    nb, nh, ltot, hd = k.shape
    tq = ROW_TILE
    assert ctx_len % tq == 0
    skip = ctx_len // tq
    nlat = ltot - ctx_len
    tk = ATT_KV_CHUNK
    assert ltot % tk == 0
    return pl.pallas_call(
        functools.partial(_attn_kernel, tk=tk, nchunks=ltot // tk),
        grid=(nb, nh, nlat // tq),
        in_specs=[pl.BlockSpec((1, 1, tq, hd), lambda b, h, i: (b, h, i + skip, 0)),
                  pl.BlockSpec((1, 1, ltot, hd), lambda b, h, i: (b, h, 0, 0)),
                  pl.BlockSpec((1, 1, ltot, MLA_V), lambda b, h, i: (b, h, 0, 0))],
        out_specs=pl.BlockSpec((1, tq, MLA_V), lambda b, h, i: (b, i, h)),
        out_shape=jax.ShapeDtypeStruct((nb, nlat, nh * MLA_V), F32),
        compiler_params=_cparams("arbitrary", "arbitrary", "arbitrary"),
        name="mla_attend",
    )(q, k, v)


def kernel(x, c, ctx, c_ctx, ada_w, ada_b, ln_g, ln_b, s5_a_re, s5_a_im, s5_log_dt, s5_b_re, s5_b_im, s5_c_re, s5_c_im, s5_d, s5_w_glu, s5_w_o, mla_w_dq, mla_q_norm, mla_w_uq, mla_w_dkv, mla_kv_norm, mla_w_ukv, mla_w_o, peer_w_q, peer_keys, peer_u, peer_v):
    nb, nlat, d = x.shape
    nctx = ctx.shape[1]
    ltot = nctx + nlat
    depth = ada_w.shape[0]
    alpha = (2 * depth) ** 0.25
    tm = ROW_TILE

    cvec = jnp.zeros((SUBLANES, d), F32).at[:nb].set(c).at[nb].set(c_ctx)
    tiles_per_batch = ltot // tm
    tile_rows_all = np.array([nb if j < nctx // tm else b for b in range(nb) for j in range(tiles_per_batch)])
    tile_rows_lat = np.array([b for b in range(nb) for _ in range(nlat // tm)])
    seg_rows = np.array([[nb] * nb, list(range(nb))])

    h_all = jnp.concatenate([ctx, x], axis=1).reshape(nb * ltot, d)

    mod = ada_mod(cvec, ada_w[0], ada_b[0]).reshape(SUBLANES, N_MOD, d)
    mod_all = mod[tile_rows_all]
    h3 = h_all.reshape(nb, ltot, d)
    xb = jnp.flip(jnp.concatenate([h3[:, nctx:], h3[:, :nctx]], axis=1), axis=1)
    bfc, ccc, ar, ai = s5_prepare(s5_a_re[0], s5_a_im[0], s5_log_dt[0], s5_b_re[0], s5_b_im[0],
                                  s5_c_re[0], s5_c_im[0], nb)
    yf, yb = s5_scan(h3, xb, mod[seg_rows], bfc, ccc, ar, ai, nctx)
    yb = jnp.flip(yb, axis=1)
    yb = jnp.concatenate([yb[:, nlat:], yb[:, :nlat]], axis=1)
    h1, t = post_mixer(h_all, (yf.reshape(-1, d), yb.reshape(-1, d)), mod_all, s5_w_o[0], ln_g[0, 0], ln_b[0, 0],
                       alpha, d_skip=s5_d[0], w_glu=s5_w_glu[0])
    f = peer_channel(t, peer_w_q[0], peer_keys[0], peer_u[0], peer_v[0])
    h_all = residual_ln(h1, f, mod_all, ln_g[0, 1], ln_b[0, 1], alpha)

    mod = ada_mod(cvec, ada_w[1], ada_b[1]).reshape(SUBLANES, N_MOD, d)
    k, v, q = mla_project(h_all, mod[tile_rows_all], nctx, nb, mla_w_dq[0], mla_q_norm[0], mla_w_uq[0],
                          mla_w_dkv[0], mla_kv_norm[0], mla_w_ukv[0])
    att = mla_attend(q, k, v, nctx).reshape(nb * nlat, -1)
    h_lat = h_all.reshape(nb, ltot, d)[:, nctx:].reshape(nb * nlat, d)
    mod_lat = mod[tile_rows_lat]
    h1, t = post_mixer(h_lat, (att,), mod_lat, mla_w_o[0], ln_g[1, 0], ln_b[1, 0], alpha)
    f = peer_channel(t, peer_w_q[1], peer_keys[1], peer_u[1], peer_v[1])
    out = residual_ln(h1, f, mod_lat, ln_g[1, 1], ln_b[1, 1], alpha)
    return out.reshape(nb, nlat, d).astype(x.dtype)
```

```python
import functools
import math

import jax
import jax.numpy as jnp
import numpy as np
from jax import lax
from jax.experimental import pallas as pl
from jax.experimental.pallas import tpu as pltpu

F32 = jnp.float32
BF16 = jnp.bfloat16
HIGHEST = lax.Precision.HIGHEST

LANES = 128
SUBLANES = 8
VMEM_LIMIT_BYTES = 56 * 1024 * 1024

N_MOD = 6
GRID_W = 64
S5_GROUP = 16
S5_STATE = 64
MLA_HEADS = 8
MLA_NOPE = 128
MLA_ROPE = 64
MLA_V = 128
ROPE_BASE = 10000.0
PEER_HEADS = 8
PEER_NKEYS = 128
PEER_DK = 128
PEER_TOPK = 16
LN_EPS = 1e-5
RMS_EPS = 1e-6

ROW_TILE = 256
S5_CHUNK = 128
PEER_TOK_TILE = 32
ATT_KV_CHUNK = 768
V_PAD_ROWS = 16


def _cparams(*sem, flags=None):
    return pltpu.CompilerParams(dimension_semantics=sem, vmem_limit_bytes=VMEM_LIMIT_BYTES, flags=flags)


def _full(shape):
    n = len(shape)
    return pl.BlockSpec(shape, lambda *_: (0,) * n)


def _gelu(x):
    return 0.5 * x * (1.0 + lax.erf(x * (1.0 / math.sqrt(2.0))))


def _layer_norm(x, g, b):
    mu = jnp.mean(x, axis=-1, keepdims=True)
    xc = x - mu
    var = jnp.mean(xc * xc, axis=-1, keepdims=True)
    return xc * lax.rsqrt(var + LN_EPS) * g + b


def _rms_norm(x, g):
    return x * lax.rsqrt(jnp.mean(x * x, axis=-1, keepdims=True) + RMS_EPS) * g


def _bdot(a, b):
    return jnp.dot(a.astype(BF16), b, preferred_element_type=F32)


def _ada_kernel(c_ref, w_ref, b_ref, o_ref):
    c = c_ref[...]
    s = c * jax.nn.sigmoid(c)
    o_ref[...] = jnp.dot(s, w_ref[...], precision=HIGHEST, preferred_element_type=F32) + b_ref[...]


def ada_mod(cvec, w, b):
    d = cvec.shape[1]
    n = w.shape[1]
    tn = 1024
    return pl.pallas_call(
        _ada_kernel,
        grid=(n // tn,),
        in_specs=[_full((SUBLANES, d)), pl.BlockSpec((d, tn), lambda j: (0, j)),
                  pl.BlockSpec((1, tn), lambda j: (0, j))],
        out_specs=pl.BlockSpec((SUBLANES, tn), lambda j: (0, j)),
        out_shape=jax.ShapeDtypeStruct((SUBLANES, n), F32),
        compiler_params=_cparams("arbitrary"),
        name="ada_mod",
    )(cvec, w, b.reshape(1, n))


def _s5_kernel(xf_ref, xb_ref, mod_ref, bf_ref, cc_ref, ar_ref, ai_ref, yf_ref, yb_ref, bu_ref, x_ref,
               *, chunk, nbatch):
    nlc = bf_ref.shape[2]
    spc = bf_ref.shape[4] // LANES

    @pl.when(pl.program_id(0) == 0)
    def _():
        x_ref[...] = jnp.zeros_like(x_ref)

    for q in range(2 * nbatch):
        d, b = divmod(q, nbatch)
        src = xf_ref if d == 0 else xb_ref
        u = (src[b] * (1.0 + mod_ref[0, b, 1:2, :]) + mod_ref[0, b, 0:1, :]).astype(BF16)
        for part in range(2):
            r = q + 2 * nbatch * part
            for j in range(nlc):
                res = jnp.dot(u[:, LANES * j:LANES * (j + 1)], bf_ref[d, part, j], preferred_element_type=F32)
                for k in range(spc):
                    bu_ref[spc * j + k, pl.ds(r, chunk, stride=SUBLANES), :] = res[:, LANES * k:LANES * (k + 1)]

    ar = ar_ref[...]
    ai = ai_ref[...]
    row = lax.broadcasted_iota(jnp.int32, x_ref.shape, 1)
    is_bwd = (row % (2 * nbatch)) >= nbatch

    def step(s, x):
        off_f = pl.multiple_of(s * SUBLANES, SUBLANES)
        off_b = pl.multiple_of((chunk - 1 - s) * SUBLANES, SUBLANES)
        slab_f = bu_ref[:, pl.ds(off_f, SUBLANES), :]
        slab_b = bu_ref[:, pl.ds(off_b, SUBLANES), :]
        xn = ar * x + ai * pltpu.roll(x, SUBLANES // 2, axis=1) + jnp.where(is_bwd, slab_b, slab_f)
        bu_ref[:, pl.ds(off_f, SUBLANES), :] = jnp.where(is_bwd, slab_f, xn)
        bu_ref[:, pl.ds(off_b, SUBLANES), :] = jnp.where(is_bwd, xn, slab_b)
        return xn

    x_ref[...] = lax.fori_loop(0, chunk, step, x_ref[...])

    def state_rows(r, j):
        return jnp.concatenate([bu_ref[spc * j + k, pl.ds(r, chunk, stride=SUBLANES), :] for k in range(spc)],
                               axis=1).astype(BF16)

    for q in range(2 * nbatch):
        d, b = divmod(q, nbatch)
        out = yf_ref if d == 0 else yb_ref
        for j in range(nlc):
            y = (jnp.dot(state_rows(q, j), cc_ref[d, 0, j], preferred_element_type=F32)
                 + jnp.dot(state_rows(q + 2 * nbatch, j), cc_ref[d, 1, j], preferred_element_type=F32))
            out[b, :, LANES * j:LANES * (j + 1)] = y


def s5_scan(x, mod_seg, bfc, ccc, ar, ai, ctx_len):
    nb, ltot, d = x.shape
    assert 2 * nb * 2 == SUBLANES, "state rows must fill one sublane tile"
    chunk = S5_CHUNK
    assert ltot % chunk == 0 and ctx_len % chunk == 0
    ctx_chunks = ctx_len // chunk
    nchunks = ltot // chunk
    nslab = ar.shape[0]
    fwd_spec = pl.BlockSpec((nb, chunk, d), lambda c: (0, c, 0))
    bwd_spec = pl.BlockSpec((nb, chunk, d), lambda c: (0, (nchunks - 1 - c + ctx_chunks) % nchunks, 0))
    return pl.pallas_call(
        functools.partial(_s5_kernel, chunk=chunk, nbatch=nb),
        grid=(nchunks,),
        in_specs=[fwd_spec, bwd_spec,
                  pl.BlockSpec((1, nb, N_MOD, d), lambda c: (jnp.minimum(c // ctx_chunks, 1), 0, 0, 0)),
                  _full(bfc.shape), _full(ccc.shape), _full(ar.shape), _full(ai.shape)],
        out_specs=[fwd_spec, bwd_spec],
        out_shape=[jax.ShapeDtypeStruct(x.shape, F32)] * 2,
        scratch_shapes=[pltpu.VMEM((nslab, SUBLANES * chunk, LANES), F32),
                        pltpu.VMEM((nslab, SUBLANES, LANES), F32)],
        compiler_params=_cparams("arbitrary"),
        name="s5_scan",
    )(x, x, mod_seg, bfc, ccc, ar, ai)


def s5_prepare(a_re, a_im, log_dt, b_re, b_im, c_re, c_im, nbatch):
    ndir, g, p = a_re.shape
    gc = b_re.shape[-1]
    gpc = LANES // gc
    nlc = g // gpc
    dt = jnp.exp(log_dt.astype(F32))[..., None]
    a_re = a_re.astype(F32)
    a_im = a_im.astype(F32)
    mag = jnp.exp(a_re * dt)
    ab_re = mag * jnp.cos(a_im * dt)
    ab_im = mag * jnp.sin(a_im * dt)
    nr, ni = ab_re - 1.0, ab_im
    den = a_re * a_re + a_im * a_im
    f_re = (nr * a_re + ni * a_im) / den
    f_im = (ni * a_re - nr * a_im) / den
    bf_re = f_re[..., None] * b_re - f_im[..., None] * b_im
    bf_im = f_re[..., None] * b_im + f_im[..., None] * b_re
    bf = jnp.stack([bf_re, bf_im], axis=1).reshape(ndir, 2, nlc, gpc, p, gc)
    eye = jnp.eye(gpc, dtype=F32)
    bfc = jnp.einsum('dqjgpc,gh->dqjgchp', bf, eye).reshape(ndir, 2, nlc, gpc * gc, gpc * p).astype(BF16)
    cc = jnp.stack([c_re, -c_im], axis=1).astype(F32).reshape(ndir, 2, nlc, gpc, gc, p)
    ccc = jnp.einsum('dqjgcp,gh->dqjgphc', cc, eye).reshape(ndir, 2, nlc, gpc * p, gpc * gc).astype(BF16)
    rows_r, rows_i = [], []
    for part in range(2):
        for d in range(ndir):
            for _ in range(nbatch):
                rows_r.append(ab_re[d].reshape(-1))
                rows_i.append(ab_im[d].reshape(-1) * (-1.0 if part == 0 else 1.0))
    slabs = lambda rows: jnp.stack(rows).reshape(len(rows), -1, LANES).transpose(1, 0, 2)
    return bfc, ccc, slabs(rows_r), slabs(rows_i)


def _post_kernel(*refs, alpha, glu):
    if glu:
        h_ref, yf_ref, yb_ref, mod_ref, dsk_ref, wg_ref, wo_ref, lng_ref, lnb_ref, h1_ref, t_ref = refs
        h = h_ref[...]
        m = h * (1.0 + mod_ref[0, 1:2, :]) + mod_ref[0, 0:1, :]
        z = _gelu(yf_ref[...] + yb_ref[...] + dsk_ref[...] * m)
        z = z * jax.nn.sigmoid(_bdot(z, wg_ref[...]))
    else:
        h_ref, z_ref, mod_ref, wo_ref, lng_ref, lnb_ref, h1_ref, t_ref = refs
        h = h_ref[...]
        z = z_ref[...]
    o = _bdot(z, wo_ref[...])
    h1 = _layer_norm(alpha * h + mod_ref[0, 2:3, :] * o, lng_ref[...], lnb_ref[...])
    h1_ref[...] = h1
    t_ref[...] = h1 * (1.0 + mod_ref[0, 4:5, :]) + mod_ref[0, 3:4, :]


def post_mixer(h, pre, mod_tiles, w_o, ln_g, ln_b, alpha, d_skip=None, w_glu=None):
    n, d = h.shape
    tm = ROW_TILE
    glu = w_glu is not None
    row = pl.BlockSpec((tm, d), lambda i: (i, 0))
    vec = _full((1, d))
    modspec = pl.BlockSpec((1, N_MOD, d), lambda i: (i, 0, 0))
    if glu:
        args = (h, pre[0], pre[1], mod_tiles, d_skip.reshape(1, d), w_glu.astype(BF16), w_o.astype(BF16),
                ln_g.reshape(1, d), ln_b.reshape(1, d))
        in_specs = [row, row, row, modspec, vec, _full((d, d)), _full((d, d)), vec, vec]
    else:
        args = (h, pre[0], mod_tiles, w_o.astype(BF16), ln_g.reshape(1, d), ln_b.reshape(1, d))
        in_specs = [row, pl.BlockSpec((tm, pre[0].shape[1]), lambda i: (i, 0)), modspec,
                    _full(w_o.shape), vec, vec]
    return pl.pallas_call(
        functools.partial(_post_kernel, alpha=alpha, glu=glu),
        grid=(n // tm,),
        in_specs=in_specs,
        out_specs=[row, row],
        out_shape=[jax.ShapeDtypeStruct((n, d), F32)] * 2,
        compiler_params=_cparams("arbitrary"),
        name="post_mixer_glu" if glu else "post_mixer",
    )(*args)


def _res_ln_kernel(h_ref, f_ref, mod_ref, lng_ref, lnb_ref, o_ref, *, alpha):
    o_ref[...] = _layer_norm(alpha * h_ref[...] + mod_ref[0, 5:6, :] * f_ref[...], lng_ref[...], lnb_ref[...])


def residual_ln(h, f, mod_tiles, ln_g, ln_b, alpha):
    n, d = h.shape
    tm = ROW_TILE
    row = pl.BlockSpec((tm, d), lambda i: (i, 0))
    vec = _full((1, d))
    return pl.pallas_call(
        functools.partial(_res_ln_kernel, alpha=alpha),
        grid=(n // tm,),
        in_specs=[row, row, pl.BlockSpec((1, N_MOD, d), lambda i: (i, 0, 0)), vec, vec],
        out_specs=row,
        out_shape=jax.ShapeDtypeStruct((n, d), F32),
        compiler_params=_cparams("arbitrary"),
        name="residual_ln",
    )(h, f, mod_tiles, ln_g.reshape(1, d), ln_b.reshape(1, d))


def _topk_rows(s, k, payload=None):
    nrow = s.shape[0]
    iota = lax.broadcasted_iota(jnp.int32, s.shape, 0).astype(F32)
    vals, idxs, pays = [], [], []
    for _ in range(k):
        m = jnp.max(s, axis=0, keepdims=True)
        i = jnp.min(jnp.where(s == m, iota, float(nrow)), axis=0, keepdims=True)
        hit = iota == i
        vals.append(m)
        idxs.append(i)
        if payload is not None:
            pays.append(jnp.max(jnp.where(hit, payload, -1.0), axis=0, keepdims=True))
        s = jnp.where(hit, -jnp.inf, s)
    return vals, idxs, pays


def _route_kernel(t_ref, wqt_ref, keys_ref, e_ref, g_ref):
    nt = (((1,), (1,)), ((), ()))
    qt = lax.dot_general(wqt_ref[...], t_ref[...], nt, precision=HIGHEST, preferred_element_type=F32)
    half = PEER_DK // 2
    for h in range(PEER_HEADS):
        sv, si = [], []
        for s in range(2):
            row0 = (h * 2 + s) * half
            sc = jnp.dot(keys_ref[h * 2 + s], qt[row0:row0 + half, :], precision=HIGHEST,
                         preferred_element_type=F32)
            v, i, _ = _topk_rows(sc, PEER_TOPK)
            sv.append(v)
            si.append(i)
        v2 = jnp.concatenate(sv[1], axis=0)
        i2 = jnp.concatenate(si[1], axis=0)
        width = [PEER_TOPK // (a + 1) for a in range(PEER_TOPK)]
        npad = -sum(width) % SUBLANES
        tm = v2.shape[1]
        cand = jnp.concatenate([sv[0][a] + v2[:width[a]] for a in range(PEER_TOPK)]
                               + [jnp.full((npad, tm), -jnp.inf, F32)], axis=0)
        cexp = jnp.concatenate([si[0][a] * float(PEER_NKEYS) + i2[:width[a]] for a in range(PEER_TOPK)]
                               + [jnp.full((npad, tm), -1.0, F32)], axis=0)
        cv, _, ce = _topk_rows(cand, PEER_TOPK, payload=cexp)
        cvm = jnp.concatenate(cv, axis=0)
        ex = jnp.exp(cvm - cv[0])
        lo = h * PEER_TOPK
        g_ref[lo:lo + PEER_TOPK, :] = ex / jnp.sum(ex, axis=0, keepdims=True)
        e_ref[lo:lo + PEER_TOPK, :] = jnp.concatenate(ce, axis=0).astype(jnp.int32)


def peer_route(t, w_q, keys):
    n, d = t.shape
    tm = ROW_TILE
    hk = PEER_HEADS * PEER_TOPK
    keys2 = keys.reshape(PEER_HEADS * 2, PEER_NKEYS, PEER_DK // 2)
    out = pl.BlockSpec((hk, tm), lambda i: (0, i))
    return pl.pallas_call(
        _route_kernel,
        grid=(n // tm,),
        in_specs=[pl.BlockSpec((tm, d), lambda i: (i, 0)), _full((w_q.shape[1], d)), _full(keys2.shape)],
        out_specs=[out, out],
        out_shape=[jax.ShapeDtypeStruct((hk, n), jnp.int32), jax.ShapeDtypeStruct((hk, n), F32)],
        compiler_params=_cparams("arbitrary"),
        name="peer_route",
    )(t, w_q.T, keys2)


GATHER_STRIDE = PEER_HEADS * PEER_TOPK + SUBLANES


PEER_ORDER = (("D", 0), ("G", 0), ("D", 1), ("G", 1), ("D", 2), ("G", 2), ("D", 3), ("G", 3),
              ("G", 4), ("G", 5), ("G", 6), ("G", 7))


def _split3(x):
    hi = x.astype(BF16)
    r1 = x - hi.astype(F32)
    mid = r1.astype(BF16)
    lo = (r1 - mid.astype(F32)).astype(BF16)
    return hi, mid, lo


def _gather_rows(idx_ref, tab_ref, g_ref, t, k0, k1):
    rows = tab_ref.shape[1]
    ids = idx_ref.at[t]
    for k in range(k0, k1):
        g_ref[pl.ds(k, rows, stride=GATHER_STRIDE), :] = tab_ref[ids[k]]


def _chunk_matrix(g_ref, c, nk):
    return pltpu.bitcast(g_ref[GATHER_STRIDE * c:GATHER_STRIDE * c + nk, :], BF16)


def _token_pipeline(idx_ref, tab_ref, tm, nk, compute_chunk, finish, g0_ref, g1_ref):
    nchunk = tab_ref.shape[1]
    nparts = len([o for o in PEER_ORDER if o[0] == "G"])
    per = nk // nparts
    _gather_rows(idx_ref, tab_ref, g0_ref, 0, 0, nk)

    def token(t, cur_ref, nxt_ref):
        nxt = jnp.minimum(t + 1, tm - 1)
        acc = None
        for kind, j in PEER_ORDER:
            if kind == "D":
                acc = compute_chunk(t, j, _chunk_matrix(cur_ref, j, nk), acc)
            else:
                _gather_rows(idx_ref, tab_ref, nxt_ref, nxt, j * per, (j + 1) * per)
        finish(t, acc)

    def step(t, carry):
        @pl.when(t % 2 == 0)
        def _():
            token(t, g0_ref, g1_ref)

        @pl.when(t % 2 == 1)
        def _():
            token(t, g1_ref, g0_ref)

        return carry

    lax.fori_loop(0, tm, step, 0)


def _peer_u_kernel(idx_ref, tab_ref, th_ref, gate_ref, mask_ref, pool_ref, o_ref, g0_ref, g1_ref, s_ref,
                   *, tm, nk):
    nchunk = tab_ref.shape[1]
    nt = (((1,), (1,)), ((), ()))

    def compute_chunk(t, c, w, acc):
        if acc is None:
            x = th_ref[t]
            hi = x.astype(BF16).astype(F32)
            lhs = jnp.concatenate([hi, x - hi], axis=0).astype(BF16)
            acc = (lhs, jnp.zeros((SUBLANES, 2 * nk), F32))
        lhs, s = acc
        out = lax.dot_general(lhs, w, nt, preferred_element_type=F32)
        return lhs, s + (out[:SUBLANES] + out[SUBLANES:]) * mask_ref[c]

    def finish(t, acc):
        s_ref[pl.ds(t, 1), :] = jnp.sum(acc[1], axis=0, keepdims=True)

    _token_pipeline(idx_ref, tab_ref, tm, nk, compute_chunk, finish, g0_ref, g1_ref)
    act = jnp.dot(s_ref[...], pool_ref[...], precision=HIGHEST, preferred_element_type=F32)
    o_ref[...] = _gelu(act) * gate_ref[...]


def _peer_v_kernel(idx_ref, tab_ref, a_ref, elo_ref, ehi_ref, o_ref, g0_ref, g1_ref, x_ref, *, tm, nk):
    nchunk = tab_ref.shape[1]
    for j, p in enumerate(_split3(a_ref[...])):
        x_ref[j] = jnp.dot(p, elo_ref[...], preferred_element_type=F32)
        x_ref[3 + j] = jnp.dot(p, ehi_ref[...], preferred_element_type=F32)

    def compute_chunk(t, c, w, lhs):
        if lhs is None:
            lhs = jnp.concatenate([x_ref[r, pl.ds(t, 1), :] for r in range(6)]
                                  + [jnp.zeros((SUBLANES - 6, 2 * nk), F32)], axis=0).astype(BF16)
        out = jnp.dot(lhs, w, preferred_element_type=F32)
        o_ref[t, pl.ds(c, 1), :] = out[0:1] + out[1:2] + out[2:3]
        o_ref[t, pl.ds(nchunk + c, 1), :] = out[3:4] + out[4:5] + out[5:6]
        return lhs

    _token_pipeline(idx_ref, tab_ref, tm, nk, compute_chunk, lambda t, lhs: None, g0_ref, g1_ref)


def _pack_table(tab):
    e, d = tab.shape
    half = d // 2
    b = lax.bitcast_convert_type(tab.astype(BF16), jnp.uint16).astype(jnp.uint32)
    w = b[:, :half] | (b[:, half:] << 16)
    return lax.bitcast_convert_type(w, jnp.int32).reshape(e, half // LANES, LANES)


def peer_experts(idx, gate, th, u_tab, v_tab):
    n, nk = idx.shape
    tm = PEER_TOK_TILE
    nchunk = th.shape[1] // 2
    up, vp = _pack_table(u_tab), _pack_table(v_tab)
    rho = np.arange(2 * nk)
    mask = np.zeros((nchunk, SUBLANES, 2 * nk), np.float32)
    for c in range(nchunk):
        mask[c, c, rho % 2 == 0] = 1.0
        mask[c, nchunk + c, rho % 2 == 1] = 1.0
    pool = (rho[:, None] // 2 == np.arange(nk)[None, :]).astype(np.float32)
    elo = (np.arange(nk)[:, None] * 2 == rho[None, :]).astype(np.float32)
    ehi = (np.arange(nk)[:, None] * 2 + 1 == rho[None, :]).astype(np.float32)
    smem = pl.BlockSpec((tm, nk), lambda i: (i, 0), memory_space=pltpu.SMEM)
    tab_spec = pl.BlockSpec(up.shape, lambda i: (0, 0, 0), pipeline_mode=pl.Buffered(1))
    row = pl.BlockSpec((tm, nk), lambda i: (i, 0))
    th_spec = pl.BlockSpec((tm,) + th.shape[1:], lambda i: (i, 0, 0))
    gather_tiles = [pltpu.VMEM((GATHER_STRIDE * nchunk, LANES), jnp.int32)] * 2
    act = pl.pallas_call(
        functools.partial(_peer_u_kernel, tm=tm, nk=nk),
        grid=(n // tm,),
        in_specs=[smem, tab_spec, th_spec, row, _full(mask.shape), _full(pool.shape)],
        out_specs=row,
        out_shape=jax.ShapeDtypeStruct((n, nk), F32),
        scratch_shapes=gather_tiles + [pltpu.VMEM((tm, 2 * nk), F32)],
        compiler_params=_cparams("arbitrary"),
        name="peer_u",
    )(idx, up, th, gate, jnp.asarray(mask), jnp.asarray(pool))
    return pl.pallas_call(
        functools.partial(_peer_v_kernel, tm=tm, nk=nk),
        grid=(n // tm,),
        in_specs=[smem, tab_spec, row, _full(elo.shape), _full(ehi.shape)],
        out_specs=th_spec,
        out_shape=jax.ShapeDtypeStruct(th.shape, F32),
        scratch_shapes=gather_tiles + [pltpu.VMEM((6, tm, 2 * nk), F32)],
        compiler_params=_cparams("arbitrary"),
        name="peer_v",
    )(idx, vp, act, jnp.asarray(elo, BF16), jnp.asarray(ehi, BF16))


def peer_channel(t, w_q, keys, u_tab, v_tab):
    n, d = t.shape
    e, g = peer_route(t, w_q, keys)
    f = peer_experts(e.T, g.T, t.reshape(n, d // LANES, LANES), u_tab, v_tab)
    return f.reshape(n, d)


def _mla_proj_kernel(h_ref, mod_ref, ck_ref, sk_ref, cq_ref, sq_ref, wdc_ref, wdr_ref, wdrs_ref, kvn_ref,
                     wuk_ref, wuv_ref, wdq_ref, qn_ref, wqn_ref, wqr_ref, wqrs_ref, k_ref, v_ref, q_ref, *, scale):
    m = (h_ref[...] * (1.0 + mod_ref[0, 1:2, :]) + mod_ref[0, 0:1, :]).astype(BF16)
    ckv = _rms_norm(jnp.dot(m, wdc_ref[...], preferred_element_type=F32), kvn_ref[...]).astype(BF16)
    kr = (jnp.dot(m, wdr_ref[...], preferred_element_type=F32) * ck_ref[...]
          + jnp.dot(m, wdrs_ref[...], preferred_element_type=F32) * sk_ref[...]).astype(BF16)
    kn = jnp.dot(ckv, wuk_ref[...], preferred_element_type=F32).astype(BF16)
    vt = lax.dot_general(wuv_ref[...], ckv, (((1,), (1,)), ((), ())),
                         preferred_element_type=F32).astype(BF16)
    cq = _rms_norm(jnp.dot(m, wdq_ref[...], preferred_element_type=F32), qn_ref[...]).astype(BF16)
    qn = (jnp.dot(cq, wqn_ref[...], preferred_element_type=F32) * scale).astype(BF16)
    qr = ((jnp.dot(cq, wqr_ref[...], preferred_element_type=F32) * cq_ref[...]
           + jnp.dot(cq, wqrs_ref[...], preferred_element_type=F32) * sq_ref[...]) * scale).astype(BF16)
    for h in range(MLA_HEADS):
        k_ref[0, h, :, 0:MLA_NOPE] = kn[:, h * MLA_NOPE:(h + 1) * MLA_NOPE]
        k_ref[0, h, :, MLA_NOPE:] = kr
        v_ref[0, h, 0:MLA_V, :] = vt[h * MLA_V:(h + 1) * MLA_V, :]
        v_ref[0, h, MLA_V:, :] = (lax.broadcasted_iota(jnp.int32, (V_PAD_ROWS, vt.shape[1]), 0) == 0).astype(BF16)
        q_ref[0, h, :, 0:MLA_NOPE] = qn[:, h * MLA_NOPE:(h + 1) * MLA_NOPE]
        q_ref[0, h, :, MLA_NOPE:] = qr[:, h * MLA_ROPE:(h + 1) * MLA_ROPE]


def _swap_halves(w, width):
    r = w.reshape(w.shape[0], -1, 2, width // 2)
    return r[:, :, ::-1, :].reshape(w.shape)


def mla_project(x, mod_tiles, ctx_len, nbatch, w_dq, q_norm, w_uq, w_dkv, kv_norm, w_ukv):
    n, d = x.shape
    ltot = n // nbatch
    nlat = ltot - ctx_len
    tm = ROW_TILE
    tpb = ltot // tm
    hd = MLA_NOPE + MLA_ROPE
    kvl = kv_norm.shape[0]
    ql = q_norm.shape[0]
    pos = jnp.arange(nlat, dtype=jnp.int32)
    nf = MLA_ROPE // 4
    inv = ROPE_BASE ** (-jnp.arange(nf, dtype=F32) / nf)
    ang = jnp.concatenate([(pos // GRID_W).astype(F32)[:, None] * inv,
                           (pos % GRID_W).astype(F32)[:, None] * inv], axis=-1)
    cos = jnp.concatenate([jnp.ones((ctx_len, MLA_ROPE // 2), F32), jnp.cos(ang)], axis=0)
    sin = jnp.concatenate([jnp.zeros((ctx_len, MLA_ROPE // 2), F32), jnp.sin(ang)], axis=0)
    ck = jnp.concatenate([cos, cos], axis=1)
    sk = jnp.concatenate([-sin, sin], axis=1)
    cq = jnp.tile(ck, (1, MLA_HEADS))
    sq = jnp.tile(sk, (1, MLA_HEADS))
    w_dkv_c = w_dkv[:, :kvl].astype(BF16)
    w_dkv_r = w_dkv[:, kvl:]
    ukv = w_ukv.reshape(kvl, MLA_HEADS, MLA_NOPE + MLA_V)
    w_uk = ukv[:, :, :MLA_NOPE].reshape(kvl, -1).astype(BF16)
    w_uv = ukv[:, :, MLA_NOPE:].reshape(kvl, -1).T.astype(BF16)
    uq = w_uq.reshape(ql, MLA_HEADS, hd)
    w_qn = uq[:, :, :MLA_NOPE].reshape(ql, -1).astype(BF16)
    w_qr = uq[:, :, MLA_NOPE:].reshape(ql, -1)
    args = (x, mod_tiles, ck, sk, cq, sq, w_dkv_c, w_dkv_r.astype(BF16),
            _swap_halves(w_dkv_r, MLA_ROPE).astype(BF16), kv_norm.reshape(1, kvl), w_uk, w_uv,
            w_dq.astype(BF16), q_norm.reshape(1, ql), w_qn, w_qr.astype(BF16),
            _swap_halves(w_qr, MLA_ROPE).astype(BF16))
    pos_spec = lambda w: pl.BlockSpec((tm, w), lambda i: (i % tpb, 0))
    in_specs = [pl.BlockSpec((tm, d), lambda i: (i, 0)), pl.BlockSpec((1, N_MOD, d), lambda i: (i, 0, 0)),
                pos_spec(MLA_ROPE), pos_spec(MLA_ROPE), pos_spec(MLA_ROPE * MLA_HEADS),
                pos_spec(MLA_ROPE * MLA_HEADS)] + [_full(a.shape) for a in args[6:]]
    head_spec = lambda w: pl.BlockSpec((1, MLA_HEADS, tm, w), lambda i: (i // tpb, 0, i % tpb, 0))
    return pl.pallas_call(
        functools.partial(_mla_proj_kernel, scale=float(hd) ** -0.5 * math.log2(math.e)),
        grid=(n // tm,),
        in_specs=in_specs,
        out_specs=[head_spec(hd),
                   pl.BlockSpec((1, MLA_HEADS, MLA_V + V_PAD_ROWS, tm), lambda i: (i // tpb, 0, 0, i % tpb)),
                   head_spec(hd)],
        out_shape=[jax.ShapeDtypeStruct((nbatch, MLA_HEADS, ltot, hd), BF16),
                   jax.ShapeDtypeStruct((nbatch, MLA_HEADS, MLA_V + V_PAD_ROWS, ltot), BF16),
                   jax.ShapeDtypeStruct((nbatch, MLA_HEADS, ltot, hd), BF16)],
        compiler_params=_cparams("arbitrary"),
        name="mla_project",
    )(*args)


def _attn_kernel(q_ref, k_ref, vt_ref, o_ref, *, tk, nchunks):
    q = q_ref[0, 0]
    tq = q.shape[0]
    nt = (((1,), (1,)), ((), ()))

    def scores(c):
        return lax.dot_general(k_ref[0, 0, c * tk:(c + 1) * tk, :], q, nt, preferred_element_type=F32)

    m_i = jnp.full((1, tq), -jnp.inf, F32)
    acc = jnp.zeros((vt_ref.shape[2], tq), F32)
    s = scores(0)
    for c in range(nchunks):
        s_next = scores(c + 1) if c + 1 < nchunks else None
        m_new = jnp.maximum(m_i, jnp.max(s, axis=0, keepdims=True))
        p = jnp.exp2(s - m_new).astype(BF16)
        acc = jnp.exp2(m_i - m_new) * acc + jnp.dot(vt_ref[0, 0, :, c * tk:(c + 1) * tk], p,
                                                    preferred_element_type=F32)
        m_i, s = m_new, s_next
    o_ref[0] = (acc[:MLA_V] / acc[MLA_V:MLA_V + 1]).T


def mla_attend(q, k, vt, ctx_len):
    nb, nh, ltot, hd = k.shape
    tq = ROW_TILE
    assert ctx_len % tq == 0
    skip = ctx_len // tq
    nlat = ltot - ctx_len
    tk = ATT_KV_CHUNK
    assert ltot % tk == 0
    return pl.pallas_call(
        functools.partial(_attn_kernel, tk=tk, nchunks=ltot // tk),
        grid=(nb, nh, nlat // tq),
        in_specs=[pl.BlockSpec((1, 1, tq, hd), lambda b, h, i: (b, h, i + skip, 0)),
                  pl.BlockSpec((1, 1, ltot, hd), lambda b, h, i: (b, h, 0, 0)),
                  pl.BlockSpec((1, 1, MLA_V + V_PAD_ROWS, ltot), lambda b, h, i: (b, h, 0, 0))],
        out_specs=pl.BlockSpec((1, tq, MLA_V), lambda b, h, i: (b, i, h)),
        out_shape=jax.ShapeDtypeStruct((nb, nlat, nh * MLA_V), F32),
        compiler_params=_cparams("arbitrary", "arbitrary", "arbitrary"),
        name="mla_attend",
    )(q, k, vt)


def kernel(x, c, ctx, c_ctx, ada_w, ada_b, ln_g, ln_b, s5_a_re, s5_a_im, s5_log_dt, s5_b_re, s5_b_im, s5_c_re, s5_c_im, s5_d, s5_w_glu, s5_w_o, mla_w_dq, mla_q_norm, mla_w_uq, mla_w_dkv, mla_kv_norm, mla_w_ukv, mla_w_o, peer_w_q, peer_keys, peer_u, peer_v):
    nb, nlat, d = x.shape
    nctx = ctx.shape[1]
    ltot = nctx + nlat
    depth = ada_w.shape[0]
    alpha = (2 * depth) ** 0.25
    tm = ROW_TILE

    cvec = jnp.zeros((SUBLANES, d), F32).at[:nb].set(c).at[nb].set(c_ctx)
    tiles_per_batch = ltot // tm
    tile_rows_all = np.array([nb if j < nctx // tm else b for b in range(nb) for j in range(tiles_per_batch)])
    tile_rows_lat = np.array([b for b in range(nb) for _ in range(nlat // tm)])
    seg_rows = np.array([[nb] * nb, list(range(nb))])

    h_all = jnp.concatenate([ctx, x], axis=1).reshape(nb * ltot, d)

    mod = ada_mod(cvec, ada_w[0], ada_b[0]).reshape(SUBLANES, N_MOD, d)
    mod_all = mod[tile_rows_all]
    bfc, ccc, ar, ai = s5_prepare(s5_a_re[0], s5_a_im[0], s5_log_dt[0], s5_b_re[0], s5_b_im[0],
                                  s5_c_re[0], s5_c_im[0], nb)
    yf, yb = s5_scan(h_all.reshape(nb, ltot, d), mod[seg_rows], bfc, ccc, ar, ai, nctx)
    h1, t = post_mixer(h_all, (yf.reshape(-1, d), yb.reshape(-1, d)), mod_all, s5_w_o[0], ln_g[0, 0], ln_b[0, 0],
                       alpha, d_skip=s5_d[0], w_glu=s5_w_glu[0])
    f = peer_channel(t, peer_w_q[0], peer_keys[0], peer_u[0], peer_v[0])
    h_all = residual_ln(h1, f, mod_all, ln_g[0, 1], ln_b[0, 1], alpha)

    mod = ada_mod(cvec, ada_w[1], ada_b[1]).reshape(SUBLANES, N_MOD, d)
    k, v, q = mla_project(h_all, mod[tile_rows_all], nctx, nb, mla_w_dq[0], mla_q_norm[0], mla_w_uq[0],
                          mla_w_dkv[0], mla_kv_norm[0], mla_w_ukv[0])
    att = mla_attend(q, k, v, nctx).reshape(nb * nlat, -1)
    h_lat = h_all.reshape(nb, ltot, d)[:, nctx:].reshape(nb * nlat, d)
    mod_lat = mod[tile_rows_lat]
    h1, t = post_mixer(h_lat, (att,), mod_lat, mla_w_o[0], ln_g[1, 0], ln_b[1, 0], alpha)
    f = peer_channel(t, peer_w_q[1], peer_keys[1], peer_u[1], peer_v[1])
    out = residual_ln(h1, f, mod_lat, ln_g[1, 1], ln_b[1, 1], alpha)
    return out.reshape(nb, nlat, d).astype(x.dtype)
```

```python
import functools
import math

import jax
import jax.numpy as jnp
import numpy as np
from jax import lax
from jax.experimental import pallas as pl
from jax.experimental.pallas import tpu as pltpu

F32 = jnp.float32
BF16 = jnp.bfloat16
HIGHEST = lax.Precision.HIGHEST

LANES = 128
SUBLANES = 8
VMEM_LIMIT_BYTES = 56 * 1024 * 1024

N_MOD = 6
GRID_W = 64
S5_GROUP = 16
S5_STATE = 64
MLA_HEADS = 8
MLA_NOPE = 128
MLA_ROPE = 64
MLA_V = 128
ROPE_BASE = 10000.0
PEER_HEADS = 8
PEER_NKEYS = 128
PEER_DK = 128
PEER_TOPK = 16
LN_EPS = 1e-5
RMS_EPS = 1e-6

ROW_TILE = 256
S5_CHUNK = 128
PEER_TOK_TILE = 64
ATT_KV_CHUNK = 768
V_PAD_ROWS = 16


def _cparams(*sem, flags=None):
    return pltpu.CompilerParams(dimension_semantics=sem, vmem_limit_bytes=VMEM_LIMIT_BYTES, flags=flags)


def _full(shape):
    n = len(shape)
    return pl.BlockSpec(shape, lambda *_: (0,) * n)


def _gelu(x):
    return 0.5 * x * (1.0 + lax.erf(x * (1.0 / math.sqrt(2.0))))


def _layer_norm(x, g, b):
    mu = jnp.mean(x, axis=-1, keepdims=True)
    xc = x - mu
    var = jnp.mean(xc * xc, axis=-1, keepdims=True)
    return xc * lax.rsqrt(var + LN_EPS) * g + b


def _rms_norm(x, g):
    return x * lax.rsqrt(jnp.mean(x * x, axis=-1, keepdims=True) + RMS_EPS) * g


def _bdot(a, b):
    return jnp.dot(a.astype(BF16), b, preferred_element_type=F32)


def _ada_kernel(c_ref, w_ref, b_ref, o_ref):
    c = c_ref[...]
    s = c * jax.nn.sigmoid(c)
    o_ref[...] = jnp.dot(s, w_ref[...], precision=HIGHEST, preferred_element_type=F32) + b_ref[...]


def ada_mod(cvec, w, b):
    d = cvec.shape[1]
    n = w.shape[1]
    tn = 1024
    return pl.pallas_call(
        _ada_kernel,
        grid=(n // tn,),
        in_specs=[_full((SUBLANES, d)), pl.BlockSpec((d, tn), lambda j: (0, j)),
                  pl.BlockSpec((1, tn), lambda j: (0, j))],
        out_specs=pl.BlockSpec((SUBLANES, tn), lambda j: (0, j)),
        out_shape=jax.ShapeDtypeStruct((SUBLANES, n), F32),
        compiler_params=_cparams("arbitrary"),
        name="ada_mod",
    )(cvec, w, b.reshape(1, n))


def _s5_kernel(xf_ref, xb_ref, mod_ref, bf_ref, cc_ref, ar_ref, ai_ref, yf_ref, yb_ref, bu_ref, x_ref,
               *, chunk, nbatch):
    nlc = bf_ref.shape[1]
    spc = bf_ref.shape[3] // (2 * LANES)

    @pl.when(pl.program_id(0) == 0)
    def _():
        x_ref[...] = jnp.zeros_like(x_ref)

    for d, src in enumerate((xf_ref, xb_ref)):
        u = jnp.concatenate([(src[b] * (1.0 + mod_ref[0, b, 1:2, :]) + mod_ref[0, b, 0:1, :]).astype(BF16)
                             for b in range(nbatch)], axis=0)
        for j in range(nlc):
            res = jnp.dot(u[:, LANES * j:LANES * (j + 1)], bf_ref[d, j], preferred_element_type=F32)
            for b in range(nbatch):
                for part in range(2):
                    r = d * nbatch + b + 2 * nbatch * part
                    for k in range(spc):
                        col = (part * spc + k) * LANES
                        bu_ref[spc * j + k, pl.ds(r, chunk, stride=SUBLANES), :] = (
                            res[b * chunk:(b + 1) * chunk, col:col + LANES])

    ar = ar_ref[...]
    ai = ai_ref[...]
    row = lax.broadcasted_iota(jnp.int32, x_ref.shape, 1)
    is_bwd = (row % (2 * nbatch)) >= nbatch

    def step(s, x):
        off_f = pl.multiple_of(s * SUBLANES, SUBLANES)
        off_b = pl.multiple_of((chunk - 1 - s) * SUBLANES, SUBLANES)
        slab_f = bu_ref[:, pl.ds(off_f, SUBLANES), :]
        slab_b = bu_ref[:, pl.ds(off_b, SUBLANES), :]
        xn = ar * x + ai * pltpu.roll(x, SUBLANES // 2, axis=1) + jnp.where(is_bwd, slab_b, slab_f)
        bu_ref[:, pl.ds(off_f, SUBLANES), :] = jnp.where(is_bwd, slab_f, xn)
        bu_ref[:, pl.ds(off_b, SUBLANES), :] = jnp.where(is_bwd, xn, slab_b)
        return xn

    x_ref[...] = lax.fori_loop(0, chunk, step, x_ref[...])

    def state_rows(r, j):
        return jnp.concatenate([bu_ref[spc * j + k, pl.ds(r, chunk, stride=SUBLANES), :] for k in range(spc)],
                               axis=1).astype(BF16)

    for d, out in enumerate((yf_ref, yb_ref)):
        for j in range(nlc):
            h = jnp.concatenate(
                [jnp.concatenate([state_rows(d * nbatch + b, j), state_rows(d * nbatch + b + 2 * nbatch, j)], axis=1)
                 for b in range(nbatch)], axis=0)
            y = jnp.dot(h, cc_ref[d, j], preferred_element_type=F32)
            for b in range(nbatch):
                out[b, :, LANES * j:LANES * (j + 1)] = y[b * chunk:(b + 1) * chunk]


def s5_scan(x, mod_seg, bfc, ccc, ar, ai, ctx_len):
    nb, ltot, d = x.shape
    assert 2 * nb * 2 == SUBLANES, "state rows must fill one sublane tile"
    chunk = S5_CHUNK
    assert ltot % chunk == 0 and ctx_len % chunk == 0
    ctx_chunks = ctx_len // chunk
    nchunks = ltot // chunk
    nslab = ar.shape[0]
    fwd_spec = pl.BlockSpec((nb, chunk, d), lambda c: (0, c, 0))
    bwd_spec = pl.BlockSpec((nb, chunk, d), lambda c: (0, (nchunks - 1 - c + ctx_chunks) % nchunks, 0))
    return pl.pallas_call(
        functools.partial(_s5_kernel, chunk=chunk, nbatch=nb),
        grid=(nchunks,),
        in_specs=[fwd_spec, bwd_spec,
                  pl.BlockSpec((1, nb, N_MOD, d), lambda c: (jnp.minimum(c // ctx_chunks, 1), 0, 0, 0)),
                  _full(bfc.shape), _full(ccc.shape), _full(ar.shape), _full(ai.shape)],
        out_specs=[fwd_spec, bwd_spec],
        out_shape=[jax.ShapeDtypeStruct(x.shape, F32)] * 2,
        scratch_shapes=[pltpu.VMEM((nslab, SUBLANES * chunk, LANES), F32),
                        pltpu.VMEM((nslab, SUBLANES, LANES), F32)],
        compiler_params=_cparams("arbitrary"),
        name="s5_scan",
    )(x, x, mod_seg, bfc, ccc, ar, ai)


def s5_prepare(a_re, a_im, log_dt, b_re, b_im, c_re, c_im, nbatch):
    ndir, g, p = a_re.shape
    gc = b_re.shape[-1]
    gpc = LANES // gc
    nlc = g // gpc
    dt = jnp.exp(log_dt.astype(F32))[..., None]
    a_re = a_re.astype(F32)
    a_im = a_im.astype(F32)
    mag = jnp.exp(a_re * dt)
    ab_re = mag * jnp.cos(a_im * dt)
    ab_im = mag * jnp.sin(a_im * dt)
    nr, ni = ab_re - 1.0, ab_im
    den = a_re * a_re + a_im * a_im
    f_re = (nr * a_re + ni * a_im) / den
    f_im = (ni * a_re - nr * a_im) / den
    bf_re = f_re[..., None] * b_re - f_im[..., None] * b_im
    bf_im = f_re[..., None] * b_im + f_im[..., None] * b_re
    bf = jnp.stack([bf_re, bf_im], axis=1).reshape(ndir, 2, nlc, gpc, p, gc)
    eye = jnp.eye(gpc, dtype=F32)
    bfc = jnp.einsum('dqjgpc,gh->djgcqhp', bf, eye).reshape(ndir, nlc, gpc * gc, 2 * gpc * p).astype(BF16)
    cc = jnp.stack([c_re, -c_im], axis=1).astype(F32).reshape(ndir, 2, nlc, gpc, gc, p)
    ccc = jnp.einsum('dqjgcp,gh->djqgphc', cc, eye).reshape(ndir, nlc, 2 * gpc * p, gpc * gc).astype(BF16)
    rows_r, rows_i = [], []
    for part in range(2):
        for d in range(ndir):
            for _ in range(nbatch):
                rows_r.append(ab_re[d].reshape(-1))
                rows_i.append(ab_im[d].reshape(-1) * (-1.0 if part == 0 else 1.0))
    slabs = lambda rows: jnp.stack(rows).reshape(len(rows), -1, LANES).transpose(1, 0, 2)
    return bfc, ccc, slabs(rows_r), slabs(rows_i)


def _post_kernel(*refs, alpha, glu):
    if glu:
        h_ref, yf_ref, yb_ref, mod_ref, dsk_ref, wg_ref, wo_ref, lng_ref, lnb_ref, h1_ref, t_ref = refs
        h = h_ref[...]
        m = h * (1.0 + mod_ref[0, 1:2, :]) + mod_ref[0, 0:1, :]
        z = _gelu(yf_ref[...] + yb_ref[...] + dsk_ref[...] * m)
        z = z * jax.nn.sigmoid(_bdot(z, wg_ref[...]))
    else:
        h_ref, z_ref, mod_ref, wo_ref, lng_ref, lnb_ref, h1_ref, t_ref = refs
        h = h_ref[...]
        z = z_ref[...]
    o = _bdot(z, wo_ref[...])
    h1 = _layer_norm(alpha * h + mod_ref[0, 2:3, :] * o, lng_ref[...], lnb_ref[...])
    h1_ref[...] = h1
    t_ref[...] = h1 * (1.0 + mod_ref[0, 4:5, :]) + mod_ref[0, 3:4, :]


def post_mixer(h, pre, mod_tiles, w_o, ln_g, ln_b, alpha, d_skip=None, w_glu=None):
    n, d = h.shape
    tm = ROW_TILE
    glu = w_glu is not None
    row = pl.BlockSpec((tm, d), lambda i: (i, 0))
    vec = _full((1, d))
    modspec = pl.BlockSpec((1, N_MOD, d), lambda i: (i, 0, 0))
    if glu:
        args = (h, pre[0], pre[1], mod_tiles, d_skip.reshape(1, d), w_glu.astype(BF16), w_o.astype(BF16),
                ln_g.reshape(1, d), ln_b.reshape(1, d))
        in_specs = [row, row, row, modspec, vec, _full((d, d)), _full((d, d)), vec, vec]
    else:
        args = (h, pre[0], mod_tiles, w_o.astype(BF16), ln_g.reshape(1, d), ln_b.reshape(1, d))
        in_specs = [row, pl.BlockSpec((tm, pre[0].shape[1]), lambda i: (i, 0)), modspec,
                    _full(w_o.shape), vec, vec]
    return pl.pallas_call(
        functools.partial(_post_kernel, alpha=alpha, glu=glu),
        grid=(n // tm,),
        in_specs=in_specs,
        out_specs=[row, row],
        out_shape=[jax.ShapeDtypeStruct((n, d), F32)] * 2,
        compiler_params=_cparams("arbitrary"),
        name="post_mixer_glu" if glu else "post_mixer",
    )(*args)


def _res_ln_kernel(h_ref, f_ref, mod_ref, lng_ref, lnb_ref, o_ref, *, alpha):
    o_ref[...] = _layer_norm(alpha * h_ref[...] + mod_ref[0, 5:6, :] * f_ref[...], lng_ref[...], lnb_ref[...])


def residual_ln(h, f, mod_tiles, ln_g, ln_b, alpha):
    n, d = h.shape
    tm = ROW_TILE
    row = pl.BlockSpec((tm, d), lambda i: (i, 0))
    vec = _full((1, d))
    return pl.pallas_call(
        functools.partial(_res_ln_kernel, alpha=alpha),
        grid=(n // tm,),
        in_specs=[row, row, pl.BlockSpec((1, N_MOD, d), lambda i: (i, 0, 0)), vec, vec],
        out_specs=row,
        out_shape=jax.ShapeDtypeStruct((n, d), F32),
        compiler_params=_cparams("arbitrary"),
        name="residual_ln",
    )(h, f, mod_tiles, ln_g.reshape(1, d), ln_b.reshape(1, d))


def _topk_rows(s, k, payload=None):
    nrow = s.shape[0]
    iota = lax.broadcasted_iota(jnp.int32, s.shape, 0).astype(F32)
    vals, idxs, pays = [], [], []
    for _ in range(k):
        m = jnp.max(s, axis=0, keepdims=True)
        i = jnp.min(jnp.where(s == m, iota, float(nrow)), axis=0, keepdims=True)
        hit = iota == i
        vals.append(m)
        idxs.append(i)
        if payload is not None:
            pays.append(jnp.max(jnp.where(hit, payload, -1.0), axis=0, keepdims=True))
        s = jnp.where(hit, -jnp.inf, s)
    return vals, idxs, pays


def _route_kernel(t_ref, wqt_ref, keys_ref, e_ref, g_ref):
    nt = (((1,), (1,)), ((), ()))
    qt = lax.dot_general(wqt_ref[...], t_ref[...], nt, precision=HIGHEST, preferred_element_type=F32)
    half = PEER_DK // 2
    for h in range(PEER_HEADS):
        sv, si = [], []
        for s in range(2):
            row0 = (h * 2 + s) * half
            sc = jnp.dot(keys_ref[h * 2 + s], qt[row0:row0 + half, :], precision=HIGHEST,
                         preferred_element_type=F32)
            v, i, _ = _topk_rows(sc, PEER_TOPK)
            sv.append(v)
            si.append(i)
        v2 = jnp.concatenate(sv[1], axis=0)
        i2 = jnp.concatenate(si[1], axis=0)
        width = [PEER_TOPK // (a + 1) for a in range(PEER_TOPK)]
        npad = -sum(width) % SUBLANES
        tm = v2.shape[1]
        cand = jnp.concatenate([sv[0][a] + v2[:width[a]] for a in range(PEER_TOPK)]
                               + [jnp.full((npad, tm), -jnp.inf, F32)], axis=0)
        cexp = jnp.concatenate([si[0][a] * float(PEER_NKEYS) + i2[:width[a]] for a in range(PEER_TOPK)]
                               + [jnp.full((npad, tm), -1.0, F32)], axis=0)
        cv, _, ce = _topk_rows(cand, PEER_TOPK, payload=cexp)
        cvm = jnp.concatenate(cv, axis=0)
        ex = jnp.exp(cvm - cv[0])
        lo = h * PEER_TOPK
        g_ref[lo:lo + PEER_TOPK, :] = ex / jnp.sum(ex, axis=0, keepdims=True)
        e_ref[lo:lo + PEER_TOPK, :] = jnp.concatenate(ce, axis=0).astype(jnp.int32)


def peer_route(t, w_q, keys):
    n, d = t.shape
    tm = ROW_TILE
    hk = PEER_HEADS * PEER_TOPK
    keys2 = keys.reshape(PEER_HEADS * 2, PEER_NKEYS, PEER_DK // 2)
    out = pl.BlockSpec((hk, tm), lambda i: (0, i))
    return pl.pallas_call(
        _route_kernel,
        grid=(n // tm,),
        in_specs=[pl.BlockSpec((tm, d), lambda i: (i, 0)), _full((w_q.shape[1], d)), _full(keys2.shape)],
        out_specs=[out, out],
        out_shape=[jax.ShapeDtypeStruct((hk, n), jnp.int32), jax.ShapeDtypeStruct((hk, n), F32)],
        compiler_params=_cparams("arbitrary"),
        name="peer_route",
    )(t, w_q.T, keys2)


GATHER_STRIDE = PEER_HEADS * PEER_TOPK + SUBLANES


PEER_GROUP = 4
PEER_GATHER_PARTS = 16
PEER_ANCHOR_LAG = 12


def _split3(x):
    hi = x.astype(BF16)
    r1 = x - hi.astype(F32)
    mid = r1.astype(BF16)
    lo = (r1 - mid.astype(F32)).astype(BF16)
    return hi, mid, lo


def _gather_rows(idx_ref, tab_ref, g_ref, t, k0, k1, after=None):
    rows = tab_ref.shape[1]
    ids = idx_ref.at[t]
    for k in range(k0, k1):
        row = tab_ref[ids[k]]
        g_ref[pl.ds(k, rows, stride=GATHER_STRIDE), :] = row if after is None else row + after


def _zero_after(x):
    bits = pltpu.bitcast(x, jnp.uint32)
    return pltpu.bitcast(lax.shift_right_logical(bits, jnp.full_like(bits, 32)), jnp.int32)


def _chunk_matrix(g_ref, c, nk):
    return pltpu.bitcast(g_ref[GATHER_STRIDE * c:GATHER_STRIDE * c + nk, :], BF16)


def _token_pipeline(idx_ref, tab_ref, tm, nk, prepare, compute_chunk, finish, anchor, tiles):
    nchunk = tab_ref.shape[1]
    grp = PEER_GROUP
    per = nk // PEER_GATHER_PARTS
    set_a, set_b = tiles[:grp], tiles[grp:]
    for i in range(grp):
        _gather_rows(idx_ref, tab_ref, set_a[i], i, 0, nk)

    def group(t0, cur, nxt):
        dots = [(i, c) for i in range(grp) for c in range(nchunk)]
        parts = [(i, j) for i in range(grp) for j in range(PEER_GATHER_PARTS)]
        acc = [prepare(t0 + i) for i in range(grp)]
        done = []
        for n in range(max(len(dots), len(parts))):
            if n < len(dots):
                i, c = dots[n]
                acc[i] = compute_chunk(t0 + i, c, _chunk_matrix(cur[i], c, nk), acc[i])
                done.append(anchor(acc[i]))
                if c == nchunk - 1:
                    finish(t0 + i, acc[i])
            if n < len(parts):
                i, j = parts[n]
                t_next = jnp.minimum(t0 + grp + i, tm - 1)
                m = min(n - PEER_ANCHOR_LAG, len(dots) - 1)
                after = _zero_after(done[m]) if m >= 0 and (m < len(dots) - 1 or n == len(parts) - 1) else None
                _gather_rows(idx_ref, tab_ref, nxt[i], t_next, j * per, (j + 1) * per, after)

    def step(g, carry):
        @pl.when(g % 2 == 0)
        def _():
            group(g * grp, set_a, set_b)

        @pl.when(g % 2 == 1)
        def _():
            group(g * grp, set_b, set_a)

        return carry

    lax.fori_loop(0, tm // grp, step, 0)


def _peer_u_kernel(idx_ref, tab_ref, th_ref, gate_ref, mask_ref, pool_ref, o_ref, s_ref, *tiles, tm, nk):
    nchunk = tab_ref.shape[1]
    nt = (((1,), (1,)), ((), ()))

    def prepare(t):
        x = th_ref[t]
        hi = x.astype(BF16).astype(F32)
        lhs = jnp.concatenate([hi, x - hi], axis=0).astype(BF16)
        return lhs, jnp.zeros((SUBLANES, 2 * nk), F32)

    def compute_chunk(t, c, w, acc):
        lhs, s = acc
        out = lax.dot_general(lhs, w, nt, preferred_element_type=F32)
        return lhs, s + (out[:SUBLANES] + out[SUBLANES:]) * mask_ref[c]

    def finish(t, acc):
        s_ref[pl.ds(t, 1), :] = jnp.sum(acc[1], axis=0, keepdims=True)

    _token_pipeline(idx_ref, tab_ref, tm, nk, prepare, compute_chunk, finish,
                    lambda acc: acc[1][:nchunk, :LANES], tiles)
    act = jnp.dot(s_ref[...], pool_ref[...], precision=HIGHEST, preferred_element_type=F32)
    o_ref[...] = _gelu(act) * gate_ref[...]


def _peer_v_kernel(idx_ref, tab_ref, a_ref, elo_ref, ehi_ref, o_ref, x_ref, *tiles, tm, nk):
    nchunk = tab_ref.shape[1]
    for j, p in enumerate(_split3(a_ref[...])):
        x_ref[j] = jnp.dot(p, elo_ref[...], preferred_element_type=F32)
        x_ref[3 + j] = jnp.dot(p, ehi_ref[...], preferred_element_type=F32)

    def prepare(t):
        return (jnp.concatenate([x_ref[r, pl.ds(t, 1), :] for r in range(6)]
                                + [jnp.zeros((SUBLANES - 6, 2 * nk), F32)], axis=0).astype(BF16), None)

    def compute_chunk(t, c, w, state):
        lhs = state[0]
        out = jnp.dot(lhs, w, preferred_element_type=F32)
        o_ref[t, pl.ds(c, 1), :] = out[0:1] + out[1:2] + out[2:3]
        o_ref[t, pl.ds(nchunk + c, 1), :] = out[3:4] + out[4:5] + out[5:6]
        return lhs, out

    _token_pipeline(idx_ref, tab_ref, tm, nk, prepare, compute_chunk, lambda t, state: None,
                    lambda state: state[1][:nchunk], tiles)


def _pack_table(tab):
    e, d = tab.shape
    half = d // 2
    b = lax.bitcast_convert_type(tab.astype(BF16), jnp.uint16).astype(jnp.uint32)
    w = b[:, :half] | (b[:, half:] << 16)
    return lax.bitcast_convert_type(w, jnp.int32).reshape(e, half // LANES, LANES)


def peer_experts(idx, gate, th, u_tab, v_tab):
    n, nk = idx.shape
    tm = PEER_TOK_TILE
    nchunk = th.shape[1] // 2
    up, vp = _pack_table(u_tab), _pack_table(v_tab)
    rho = np.arange(2 * nk)
    mask = np.zeros((nchunk, SUBLANES, 2 * nk), np.float32)
    for c in range(nchunk):
        mask[c, c, rho % 2 == 0] = 1.0
        mask[c, nchunk + c, rho % 2 == 1] = 1.0
    pool = (rho[:, None] // 2 == np.arange(nk)[None, :]).astype(np.float32)
    elo = (np.arange(nk)[:, None] * 2 == rho[None, :]).astype(np.float32)
    ehi = (np.arange(nk)[:, None] * 2 + 1 == rho[None, :]).astype(np.float32)
    smem = pl.BlockSpec((tm, nk), lambda i: (i, 0), memory_space=pltpu.SMEM)
    tab_spec = pl.BlockSpec(up.shape, lambda i: (0, 0, 0), pipeline_mode=pl.Buffered(1))
    row = pl.BlockSpec((tm, nk), lambda i: (i, 0))
    th_spec = pl.BlockSpec((tm,) + th.shape[1:], lambda i: (i, 0, 0))
    gather_tiles = [pltpu.VMEM((GATHER_STRIDE * nchunk, LANES), jnp.int32)] * (2 * PEER_GROUP)
    act = pl.pallas_call(
        functools.partial(_peer_u_kernel, tm=tm, nk=nk),
        grid=(n // tm,),
        in_specs=[smem, tab_spec, th_spec, row, _full(mask.shape), _full(pool.shape)],
        out_specs=row,
        out_shape=jax.ShapeDtypeStruct((n, nk), F32),
        scratch_shapes=[pltpu.VMEM((tm, 2 * nk), F32)] + gather_tiles,
        compiler_params=_cparams("arbitrary"),
        name="peer_u",
    )(idx, up, th, gate, jnp.asarray(mask), jnp.asarray(pool))
    return pl.pallas_call(
        functools.partial(_peer_v_kernel, tm=tm, nk=nk),
        grid=(n // tm,),
        in_specs=[smem, tab_spec, row, _full(elo.shape), _full(ehi.shape)],
        out_specs=th_spec,
        out_shape=jax.ShapeDtypeStruct(th.shape, F32),
        scratch_shapes=[pltpu.VMEM((6, tm, 2 * nk), F32)] + gather_tiles,
        compiler_params=_cparams("arbitrary"),
        name="peer_v",
    )(idx, vp, act, jnp.asarray(elo, BF16), jnp.asarray(ehi, BF16))


def peer_channel(t, w_q, keys, u_tab, v_tab):
    n, d = t.shape
    e, g = peer_route(t, w_q, keys)
    f = peer_experts(e.T, g.T, t.reshape(n, d // LANES, LANES), u_tab, v_tab)
    return f.reshape(n, d)


def _mla_proj_kernel(h_ref, mod_ref, ck_ref, sk_ref, cq_ref, sq_ref, wdc_ref, wdr_ref, wdrs_ref, kvn_ref,
                     wuk_ref, wuv_ref, wdq_ref, qn_ref, wqn_ref, wqr_ref, wqrs_ref, k_ref, v_ref, q_ref, *, scale):
    m = (h_ref[...] * (1.0 + mod_ref[0, 1:2, :]) + mod_ref[0, 0:1, :]).astype(BF16)
    ckv = _rms_norm(jnp.dot(m, wdc_ref[...], preferred_element_type=F32), kvn_ref[...]).astype(BF16)
    kr = (jnp.dot(m, wdr_ref[...], preferred_element_type=F32) * ck_ref[...]
          + jnp.dot(m, wdrs_ref[...], preferred_element_type=F32) * sk_ref[...]).astype(BF16)
    kn = jnp.dot(ckv, wuk_ref[...], preferred_element_type=F32).astype(BF16)
    vt = lax.dot_general(wuv_ref[...], ckv, (((1,), (1,)), ((), ())),
                         preferred_element_type=F32).astype(BF16)
    cq = _rms_norm(jnp.dot(m, wdq_ref[...], preferred_element_type=F32), qn_ref[...]).astype(BF16)
    qn = (jnp.dot(cq, wqn_ref[...], preferred_element_type=F32) * scale).astype(BF16)
    qr = ((jnp.dot(cq, wqr_ref[...], preferred_element_type=F32) * cq_ref[...]
           + jnp.dot(cq, wqrs_ref[...], preferred_element_type=F32) * sq_ref[...]) * scale).astype(BF16)
    for h in range(MLA_HEADS):
        k_ref[0, h, :, 0:MLA_NOPE] = kn[:, h * MLA_NOPE:(h + 1) * MLA_NOPE]
        k_ref[0, h, :, MLA_NOPE:] = kr
        v_ref[0, h, 0:MLA_V, :] = vt[h * MLA_V:(h + 1) * MLA_V, :]
        v_ref[0, h, MLA_V:, :] = (lax.broadcasted_iota(jnp.int32, (V_PAD_ROWS, vt.shape[1]), 0) == 0).astype(BF16)
        q_ref[0, h, :, 0:MLA_NOPE] = qn[:, h * MLA_NOPE:(h + 1) * MLA_NOPE]
        q_ref[0, h, :, MLA_NOPE:] = qr[:, h * MLA_ROPE:(h + 1) * MLA_ROPE]


def _swap_halves(w, width):
    r = w.reshape(w.shape[0], -1, 2, width // 2)
    return r[:, :, ::-1, :].reshape(w.shape)


def mla_project(x, mod_tiles, ctx_len, nbatch, w_dq, q_norm, w_uq, w_dkv, kv_norm, w_ukv):
    n, d = x.shape
    ltot = n // nbatch
    nlat = ltot - ctx_len
    tm = ROW_TILE
    tpb = ltot // tm
    hd = MLA_NOPE + MLA_ROPE
    kvl = kv_norm.shape[0]
    ql = q_norm.shape[0]
    pos = jnp.arange(nlat, dtype=jnp.int32)
    nf = MLA_ROPE // 4
    inv = ROPE_BASE ** (-jnp.arange(nf, dtype=F32) / nf)
    ang = jnp.concatenate([(pos // GRID_W).astype(F32)[:, None] * inv,
                           (pos % GRID_W).astype(F32)[:, None] * inv], axis=-1)
    cos = jnp.concatenate([jnp.ones((ctx_len, MLA_ROPE // 2), F32), jnp.cos(ang)], axis=0)
    sin = jnp.concatenate([jnp.zeros((ctx_len, MLA_ROPE // 2), F32), jnp.sin(ang)], axis=0)
    ck = jnp.concatenate([cos, cos], axis=1)
    sk = jnp.concatenate([-sin, sin], axis=1)
    cq = jnp.tile(ck, (1, MLA_HEADS))
    sq = jnp.tile(sk, (1, MLA_HEADS))
    w_dkv_c = w_dkv[:, :kvl].astype(BF16)
    w_dkv_r = w_dkv[:, kvl:]
    ukv = w_ukv.reshape(kvl, MLA_HEADS, MLA_NOPE + MLA_V)
    w_uk = ukv[:, :, :MLA_NOPE].reshape(kvl, -1).astype(BF16)
    w_uv = ukv[:, :, MLA_NOPE:].reshape(kvl, -1).T.astype(BF16)
    uq = w_uq.reshape(ql, MLA_HEADS, hd)
    w_qn = uq[:, :, :MLA_NOPE].reshape(ql, -1).astype(BF16)
    w_qr = uq[:, :, MLA_NOPE:].reshape(ql, -1)
    args = (x, mod_tiles, ck, sk, cq, sq, w_dkv_c, w_dkv_r.astype(BF16),
            _swap_halves(w_dkv_r, MLA_ROPE).astype(BF16), kv_norm.reshape(1, kvl), w_uk, w_uv,
            w_dq.astype(BF16), q_norm.reshape(1, ql), w_qn, w_qr.astype(BF16),
            _swap_halves(w_qr, MLA_ROPE).astype(BF16))
    pos_spec = lambda w: pl.BlockSpec((tm, w), lambda i: (i % tpb, 0))
    in_specs = [pl.BlockSpec((tm, d), lambda i: (i, 0)), pl.BlockSpec((1, N_MOD, d), lambda i: (i, 0, 0)),
                pos_spec(MLA_ROPE), pos_spec(MLA_ROPE), pos_spec(MLA_ROPE * MLA_HEADS),
                pos_spec(MLA_ROPE * MLA_HEADS)] + [_full(a.shape) for a in args[6:]]
    head_spec = lambda w: pl.BlockSpec((1, MLA_HEADS, tm, w), lambda i: (i // tpb, 0, i % tpb, 0))
    return pl.pallas_call(
        functools.partial(_mla_proj_kernel, scale=float(hd) ** -0.5 * math.log2(math.e)),
        grid=(n // tm,),
        in_specs=in_specs,
        out_specs=[head_spec(hd),
                   pl.BlockSpec((1, MLA_HEADS, MLA_V + V_PAD_ROWS, tm), lambda i: (i // tpb, 0, 0, i % tpb)),
                   head_spec(hd)],
        out_shape=[jax.ShapeDtypeStruct((nbatch, MLA_HEADS, ltot, hd), BF16),
                   jax.ShapeDtypeStruct((nbatch, MLA_HEADS, MLA_V + V_PAD_ROWS, ltot), BF16),
                   jax.ShapeDtypeStruct((nbatch, MLA_HEADS, ltot, hd), BF16)],
        compiler_params=_cparams("arbitrary"),
        name="mla_project",
    )(*args)


def _attn_kernel(q_ref, k_ref, vt_ref, o_ref, *, tk, nchunks):
    q = q_ref[0, 0]
    tq = q.shape[0]
    nt = (((1,), (1,)), ((), ()))

    def scores(c):
        return lax.dot_general(k_ref[0, 0, c * tk:(c + 1) * tk, :], q, nt, preferred_element_type=F32)

    m_i = jnp.full((1, tq), -jnp.inf, F32)
    acc = jnp.zeros((vt_ref.shape[2], tq), F32)
    s = scores(0)
    for c in range(nchunks):
        s_next = scores(c + 1) if c + 1 < nchunks else None
        m_new = jnp.maximum(m_i, jnp.max(s, axis=0, keepdims=True))
        p = jnp.exp2(s - m_new).astype(BF16)
        acc = jnp.exp2(m_i - m_new) * acc + jnp.dot(vt_ref[0, 0, :, c * tk:(c + 1) * tk], p,
                                                    preferred_element_type=F32)
        m_i, s = m_new, s_next
    o_ref[0] = (acc[:MLA_V] / acc[MLA_V:MLA_V + 1]).T


def mla_attend(q, k, vt, ctx_len):
    nb, nh, ltot, hd = k.shape
    tq = ROW_TILE
    assert ctx_len % tq == 0
    skip = ctx_len // tq
    nlat = ltot - ctx_len
    tk = ATT_KV_CHUNK
    assert ltot % tk == 0
    return pl.pallas_call(
        functools.partial(_attn_kernel, tk=tk, nchunks=ltot // tk),
        grid=(nb, nh, nlat // tq),
        in_specs=[pl.BlockSpec((1, 1, tq, hd), lambda b, h, i: (b, h, i + skip, 0)),
                  pl.BlockSpec((1, 1, ltot, hd), lambda b, h, i: (b, h, 0, 0)),
                  pl.BlockSpec((1, 1, MLA_V + V_PAD_ROWS, ltot), lambda b, h, i: (b, h, 0, 0))],
        out_specs=pl.BlockSpec((1, tq, MLA_V), lambda b, h, i: (b, i, h)),
        out_shape=jax.ShapeDtypeStruct((nb, nlat, nh * MLA_V), F32),
        compiler_params=_cparams("arbitrary", "arbitrary", "arbitrary"),
        name="mla_attend",
    )(q, k, vt)


def kernel(x, c, ctx, c_ctx, ada_w, ada_b, ln_g, ln_b, s5_a_re, s5_a_im, s5_log_dt, s5_b_re, s5_b_im, s5_c_re, s5_c_im, s5_d, s5_w_glu, s5_w_o, mla_w_dq, mla_q_norm, mla_w_uq, mla_w_dkv, mla_kv_norm, mla_w_ukv, mla_w_o, peer_w_q, peer_keys, peer_u, peer_v):
    nb, nlat, d = x.shape
    nctx = ctx.shape[1]
    ltot = nctx + nlat
    depth = ada_w.shape[0]
    alpha = (2 * depth) ** 0.25
    tm = ROW_TILE

    cvec = jnp.zeros((SUBLANES, d), F32).at[:nb].set(c).at[nb].set(c_ctx)
    tiles_per_batch = ltot // tm
    tile_rows_all = np.array([nb if j < nctx // tm else b for b in range(nb) for j in range(tiles_per_batch)])
    tile_rows_lat = np.array([b for b in range(nb) for _ in range(nlat // tm)])
    seg_rows = np.array([[nb] * nb, list(range(nb))])

    h_all = jnp.concatenate([ctx, x], axis=1).reshape(nb * ltot, d)

    mod = ada_mod(cvec, ada_w[0], ada_b[0]).reshape(SUBLANES, N_MOD, d)
    mod_all = mod[tile_rows_all]
    bfc, ccc, ar, ai = s5_prepare(s5_a_re[0], s5_a_im[0], s5_log_dt[0], s5_b_re[0], s5_b_im[0],
                                  s5_c_re[0], s5_c_im[0], nb)
    yf, yb = s5_scan(h_all.reshape(nb, ltot, d), mod[seg_rows], bfc, ccc, ar, ai, nctx)
    h1, t = post_mixer(h_all, (yf.reshape(-1, d), yb.reshape(-1, d)), mod_all, s5_w_o[0], ln_g[0, 0], ln_b[0, 0],
                       alpha, d_skip=s5_d[0], w_glu=s5_w_glu[0])
    f = peer_channel(t, peer_w_q[0], peer_keys[0], peer_u[0], peer_v[0])
    h_all = residual_ln(h1, f, mod_all, ln_g[0, 1], ln_b[0, 1], alpha)

    mod = ada_mod(cvec, ada_w[1], ada_b[1]).reshape(SUBLANES, N_MOD, d)
    k, v, q = mla_project(h_all, mod[tile_rows_all], nctx, nb, mla_w_dq[0], mla_q_norm[0], mla_w_uq[0],
                          mla_w_dkv[0], mla_kv_norm[0], mla_w_ukv[0])
    att = mla_attend(q, k, v, nctx).reshape(nb * nlat, -1)
    h_lat = h_all.reshape(nb, ltot, d)[:, nctx:].reshape(nb * nlat, d)
    mod_lat = mod[tile_rows_lat]
    h1, t = post_mixer(h_lat, (att,), mod_lat, mla_w_o[0], ln_g[1, 0], ln_b[1, 0], alpha)
    f = peer_channel(t, peer_w_q[1], peer_keys[1], peer_u[1], peer_v[1])
    out = residual_ln(h1, f, mod_lat, ln_g[1, 1], ln_b[1, 1], alpha)
    return out.reshape(nb, nlat, d).astype(x.dtype)
```

```python
import functools
import math

import jax
import jax.numpy as jnp
import numpy as np
from jax import lax
from jax.experimental import pallas as pl
from jax.experimental.pallas import tpu as pltpu

F32 = jnp.float32
BF16 = jnp.bfloat16
HIGHEST = lax.Precision.HIGHEST

LANES = 128
SUBLANES = 8
VMEM_LIMIT_BYTES = 56 * 1024 * 1024

N_MOD = 6
GRID_W = 64
S5_GROUP = 16
S5_STATE = 64
MLA_HEADS = 8
MLA_NOPE = 128
MLA_ROPE = 64
MLA_V = 128
ROPE_BASE = 10000.0
PEER_HEADS = 8
PEER_NKEYS = 128
PEER_DK = 128
PEER_TOPK = 16
LN_EPS = 1e-5
RMS_EPS = 1e-6

ROW_TILE = 256
S5_CHUNK = 128
PEER_TOK_TILE = 64
ATT_KV_CHUNK = 768
V_PAD_ROWS = 16


def _cparams(*sem, flags=None):
    return pltpu.CompilerParams(dimension_semantics=sem, vmem_limit_bytes=VMEM_LIMIT_BYTES, flags=flags)


def _full(shape):
    n = len(shape)
    return pl.BlockSpec(shape, lambda *_: (0,) * n)


def _gelu(x):
    return 0.5 * x * (1.0 + lax.erf(x * (1.0 / math.sqrt(2.0))))


def _layer_norm(x, g, b):
    mu = jnp.mean(x, axis=-1, keepdims=True)
    xc = x - mu
    var = jnp.mean(xc * xc, axis=-1, keepdims=True)
    return xc * lax.rsqrt(var + LN_EPS) * g + b


def _rms_norm(x, g):
    return x * lax.rsqrt(jnp.mean(x * x, axis=-1, keepdims=True) + RMS_EPS) * g


def _bdot(a, b):
    return jnp.dot(a.astype(BF16), b, preferred_element_type=F32)


def _ada_kernel(c_ref, w_ref, b_ref, o_ref):
    c = c_ref[...]
    s = c * jax.nn.sigmoid(c)
    o_ref[...] = jnp.dot(s, w_ref[...], precision=HIGHEST, preferred_element_type=F32) + b_ref[...]


def ada_mod(cvec, w, b):
    d = cvec.shape[1]
    n = w.shape[1]
    tn = 1024
    return pl.pallas_call(
        _ada_kernel,
        grid=(n // tn,),
        in_specs=[_full((SUBLANES, d)), pl.BlockSpec((d, tn), lambda j: (0, j)),
                  pl.BlockSpec((1, tn), lambda j: (0, j))],
        out_specs=pl.BlockSpec((SUBLANES, tn), lambda j: (0, j)),
        out_shape=jax.ShapeDtypeStruct((SUBLANES, n), F32),
        compiler_params=_cparams("arbitrary"),
        name="ada_mod",
    )(cvec, w, b.reshape(1, n))


def _s5_kernel(xf_ref, xb_ref, mod_ref, bf_ref, cc_ref, ar_ref, ai_ref, yf_ref, yb_ref, bu_ref, x_ref,
               *, chunk, nbatch):
    nlc = bf_ref.shape[1]
    spc = bf_ref.shape[3] // (2 * LANES)

    @pl.when(pl.program_id(0) == 0)
    def _():
        x_ref[...] = jnp.zeros_like(x_ref)

    for d, src in enumerate((xf_ref, xb_ref)):
        u = jnp.concatenate([(src[b] * (1.0 + mod_ref[0, b, 1:2, :]) + mod_ref[0, b, 0:1, :]).astype(BF16)
                             for b in range(nbatch)], axis=0)
        for j in range(nlc):
            res = jnp.dot(u[:, LANES * j:LANES * (j + 1)], bf_ref[d, j], preferred_element_type=F32)
            for b in range(nbatch):
                for part in range(2):
                    r = d * nbatch + b + 2 * nbatch * part
                    for k in range(spc):
                        col = (part * spc + k) * LANES
                        bu_ref[spc * j + k, pl.ds(r, chunk, stride=SUBLANES), :] = (
                            res[b * chunk:(b + 1) * chunk, col:col + LANES])

    ar = ar_ref[...]
    ai = ai_ref[...]
    row = lax.broadcasted_iota(jnp.int32, x_ref.shape, 1)
    is_bwd = (row % (2 * nbatch)) >= nbatch

    def step(s, x):
        off_f = pl.multiple_of(s * SUBLANES, SUBLANES)
        off_b = pl.multiple_of((chunk - 1 - s) * SUBLANES, SUBLANES)
        slab_f = bu_ref[:, pl.ds(off_f, SUBLANES), :]
        slab_b = bu_ref[:, pl.ds(off_b, SUBLANES), :]
        xn = ar * x + ai * pltpu.roll(x, SUBLANES // 2, axis=1) + jnp.where(is_bwd, slab_b, slab_f)
        bu_ref[:, pl.ds(off_f, SUBLANES), :] = jnp.where(is_bwd, slab_f, xn)
        bu_ref[:, pl.ds(off_b, SUBLANES), :] = jnp.where(is_bwd, xn, slab_b)
        return xn

    x_ref[...] = lax.fori_loop(0, chunk, step, x_ref[...])

    def state_rows(r, j):
        return jnp.concatenate([bu_ref[spc * j + k, pl.ds(r, chunk, stride=SUBLANES), :] for k in range(spc)],
                               axis=1).astype(BF16)

    for d, out in enumerate((yf_ref, yb_ref)):
        for j in range(nlc):
            h = jnp.concatenate(
                [jnp.concatenate([state_rows(d * nbatch + b, j), state_rows(d * nbatch + b + 2 * nbatch, j)], axis=1)
                 for b in range(nbatch)], axis=0)
            y = jnp.dot(h, cc_ref[d, j], preferred_element_type=F32)
            for b in range(nbatch):
                out[b, :, LANES * j:LANES * (j + 1)] = y[b * chunk:(b + 1) * chunk]


def s5_scan(x, mod_seg, bfc, ccc, ar, ai, ctx_len):
    nb, ltot, d = x.shape
    assert 2 * nb * 2 == SUBLANES, "state rows must fill one sublane tile"
    chunk = S5_CHUNK
    assert ltot % chunk == 0 and ctx_len % chunk == 0
    ctx_chunks = ctx_len // chunk
    nchunks = ltot // chunk
    nslab = ar.shape[0]
    fwd_spec = pl.BlockSpec((nb, chunk, d), lambda c: (0, c, 0))
    bwd_spec = pl.BlockSpec((nb, chunk, d), lambda c: (0, (nchunks - 1 - c + ctx_chunks) % nchunks, 0))
    return pl.pallas_call(
        functools.partial(_s5_kernel, chunk=chunk, nbatch=nb),
        grid=(nchunks,),
        in_specs=[fwd_spec, bwd_spec,
                  pl.BlockSpec((1, nb, N_MOD, d), lambda c: (jnp.minimum(c // ctx_chunks, 1), 0, 0, 0)),
                  _full(bfc.shape), _full(ccc.shape), _full(ar.shape), _full(ai.shape)],
        out_specs=[fwd_spec, bwd_spec],
        out_shape=[jax.ShapeDtypeStruct(x.shape, F32)] * 2,
        scratch_shapes=[pltpu.VMEM((nslab, SUBLANES * chunk, LANES), F32),
                        pltpu.VMEM((nslab, SUBLANES, LANES), F32)],
        compiler_params=_cparams("arbitrary"),
        name="s5_scan",
    )(x, x, mod_seg, bfc, ccc, ar, ai)


def s5_prepare(a_re, a_im, log_dt, b_re, b_im, c_re, c_im, nbatch):
    ndir, g, p = a_re.shape
    gc = b_re.shape[-1]
    gpc = LANES // gc
    nlc = g // gpc
    dt = jnp.exp(log_dt.astype(F32))[..., None]
    a_re = a_re.astype(F32)
    a_im = a_im.astype(F32)
    mag = jnp.exp(a_re * dt)
    ab_re = mag * jnp.cos(a_im * dt)
    ab_im = mag * jnp.sin(a_im * dt)
    nr, ni = ab_re - 1.0, ab_im
    den = a_re * a_re + a_im * a_im
    f_re = (nr * a_re + ni * a_im) / den
    f_im = (ni * a_re - nr * a_im) / den
    bf_re = f_re[..., None] * b_re - f_im[..., None] * b_im
    bf_im = f_re[..., None] * b_im + f_im[..., None] * b_re
    bf = jnp.stack([bf_re, bf_im], axis=1).reshape(ndir, 2, nlc, gpc, p, gc)
    eye = jnp.eye(gpc, dtype=F32)
    bfc = jnp.einsum('dqjgpc,gh->djgcqhp', bf, eye).reshape(ndir, nlc, gpc * gc, 2 * gpc * p).astype(BF16)
    cc = jnp.stack([c_re, -c_im], axis=1).astype(F32).reshape(ndir, 2, nlc, gpc, gc, p)
    ccc = jnp.einsum('dqjgcp,gh->djqgphc', cc, eye).reshape(ndir, nlc, 2 * gpc * p, gpc * gc).astype(BF16)
    rows_r, rows_i = [], []
    for part in range(2):
        for d in range(ndir):
            for _ in range(nbatch):
                rows_r.append(ab_re[d].reshape(-1))
                rows_i.append(ab_im[d].reshape(-1) * (-1.0 if part == 0 else 1.0))
    slabs = lambda rows: jnp.stack(rows).reshape(len(rows), -1, LANES).transpose(1, 0, 2)
    return bfc, ccc, slabs(rows_r), slabs(rows_i)


def _post_kernel(*refs, alpha, glu):
    if glu:
        h_ref, yf_ref, yb_ref, mod_ref, dsk_ref, wg_ref, wo_ref, lng_ref, lnb_ref, h1_ref, t_ref = refs
        h = h_ref[...]
        m = h * (1.0 + mod_ref[0, 1:2, :]) + mod_ref[0, 0:1, :]
        z = _gelu(yf_ref[...] + yb_ref[...] + dsk_ref[...] * m)
        z = z * jax.nn.sigmoid(_bdot(z, wg_ref[...]))
    else:
        h_ref, z_ref, mod_ref, wo_ref, lng_ref, lnb_ref, h1_ref, t_ref = refs
        h = h_ref[...]
        z = z_ref[...]
    o = _bdot(z, wo_ref[...])
    h1 = _layer_norm(alpha * h + mod_ref[0, 2:3, :] * o, lng_ref[...], lnb_ref[...])
    h1_ref[...] = h1
    t_ref[...] = h1 * (1.0 + mod_ref[0, 4:5, :]) + mod_ref[0, 3:4, :]


def post_mixer(h, pre, mod_tiles, w_o, ln_g, ln_b, alpha, d_skip=None, w_glu=None):
    n, d = h.shape
    tm = ROW_TILE
    glu = w_glu is not None
    row = pl.BlockSpec((tm, d), lambda i: (i, 0))
    vec = _full((1, d))
    modspec = pl.BlockSpec((1, N_MOD, d), lambda i: (i, 0, 0))
    if glu:
        args = (h, pre[0], pre[1], mod_tiles, d_skip.reshape(1, d), w_glu.astype(BF16), w_o.astype(BF16),
                ln_g.reshape(1, d), ln_b.reshape(1, d))
        in_specs = [row, row, row, modspec, vec, _full((d, d)), _full((d, d)), vec, vec]
    else:
        args = (h, pre[0], mod_tiles, w_o.astype(BF16), ln_g.reshape(1, d), ln_b.reshape(1, d))
        in_specs = [row, pl.BlockSpec((tm, pre[0].shape[1]), lambda i: (i, 0)), modspec,
                    _full(w_o.shape), vec, vec]
    return pl.pallas_call(
        functools.partial(_post_kernel, alpha=alpha, glu=glu),
        grid=(n // tm,),
        in_specs=in_specs,
        out_specs=[row, row],
        out_shape=[jax.ShapeDtypeStruct((n, d), F32)] * 2,
        compiler_params=_cparams("arbitrary"),
        name="post_mixer_glu" if glu else "post_mixer",
    )(*args)


def _res_ln_kernel(h_ref, f_ref, mod_ref, lng_ref, lnb_ref, o_ref, *, alpha):
    o_ref[...] = _layer_norm(alpha * h_ref[...] + mod_ref[0, 5:6, :] * f_ref[...], lng_ref[...], lnb_ref[...])


def residual_ln(h, f, mod_tiles, ln_g, ln_b, alpha):
    n, d = h.shape
    tm = ROW_TILE
    row = pl.BlockSpec((tm, d), lambda i: (i, 0))
    vec = _full((1, d))
    return pl.pallas_call(
        functools.partial(_res_ln_kernel, alpha=alpha),
        grid=(n // tm,),
        in_specs=[row, row, pl.BlockSpec((1, N_MOD, d), lambda i: (i, 0, 0)), vec, vec],
        out_specs=row,
        out_shape=jax.ShapeDtypeStruct((n, d), F32),
        compiler_params=_cparams("arbitrary"),
        name="residual_ln",
    )(h, f, mod_tiles, ln_g.reshape(1, d), ln_b.reshape(1, d))


def _topk_rows(s, k, payload=None):
    nrow = s.shape[0]
    iota = lax.broadcasted_iota(jnp.int32, s.shape, 0).astype(F32)
    vals, idxs, pays = [], [], []
    for _ in range(k):
        m = jnp.max(s, axis=0, keepdims=True)
        i = jnp.min(jnp.where(s == m, iota, float(nrow)), axis=0, keepdims=True)
        hit = iota == i
        vals.append(m)
        idxs.append(i)
        if payload is not None:
            pays.append(jnp.max(jnp.where(hit, payload, -1.0), axis=0, keepdims=True))
        s = jnp.where(hit, -jnp.inf, s)
    return vals, idxs, pays


def _route_kernel(t_ref, wqt_ref, keys_ref, e_ref, g_ref):
    nt = (((1,), (1,)), ((), ()))
    qt = lax.dot_general(wqt_ref[...], t_ref[...], nt, precision=HIGHEST, preferred_element_type=F32)
    half = PEER_DK // 2
    for h in range(PEER_HEADS):
        sv, si = [], []
        for s in range(2):
            row0 = (h * 2 + s) * half
            sc = jnp.dot(keys_ref[h * 2 + s], qt[row0:row0 + half, :], precision=HIGHEST,
                         preferred_element_type=F32)
            v, i, _ = _topk_rows(sc, PEER_TOPK)
            sv.append(v)
            si.append(i)
        v2 = jnp.concatenate(sv[1], axis=0)
        i2 = jnp.concatenate(si[1], axis=0)
        width = [PEER_TOPK // (a + 1) for a in range(PEER_TOPK)]
        npad = -sum(width) % SUBLANES
        tm = v2.shape[1]
        cand = jnp.concatenate([sv[0][a] + v2[:width[a]] for a in range(PEER_TOPK)]
                               + [jnp.full((npad, tm), -jnp.inf, F32)], axis=0)
        cexp = jnp.concatenate([si[0][a] * float(PEER_NKEYS) + i2[:width[a]] for a in range(PEER_TOPK)]
                               + [jnp.full((npad, tm), -1.0, F32)], axis=0)
        cv, _, ce = _topk_rows(cand, PEER_TOPK, payload=cexp)
        cvm = jnp.concatenate(cv, axis=0)
        ex = jnp.exp(cvm - cv[0])
        lo = h * PEER_TOPK
        g_ref[lo:lo + PEER_TOPK, :] = ex / jnp.sum(ex, axis=0, keepdims=True)
        e_ref[lo:lo + PEER_TOPK, :] = jnp.concatenate(ce, axis=0).astype(jnp.int32)


def peer_route(t, w_q, keys):
    n, d = t.shape
    tm = ROW_TILE
    hk = PEER_HEADS * PEER_TOPK
    keys2 = keys.reshape(PEER_HEADS * 2, PEER_NKEYS, PEER_DK // 2)
    out = pl.BlockSpec((hk, tm), lambda i: (0, i))
    return pl.pallas_call(
        _route_kernel,
        grid=(n // tm,),
        in_specs=[pl.BlockSpec((tm, d), lambda i: (i, 0)), _full((w_q.shape[1], d)), _full(keys2.shape)],
        out_specs=[out, out],
        out_shape=[jax.ShapeDtypeStruct((hk, n), jnp.int32), jax.ShapeDtypeStruct((hk, n), F32)],
        compiler_params=_cparams("arbitrary"),
        name="peer_route",
    )(t, w_q.T, keys2)


GATHER_STRIDE = PEER_HEADS * PEER_TOPK + SUBLANES


PEER_GROUP = 4
PEER_GATHER_PARTS = 16
PEER_ANCHOR_LAG = 12


def _split3(x):
    hi = x.astype(BF16)
    r1 = x - hi.astype(F32)
    mid = r1.astype(BF16)
    lo = (r1 - mid.astype(F32)).astype(BF16)
    return hi, mid, lo


def _gather_rows(idx_ref, tab_ref, g_ref, t, k0, k1, after=None):
    rows = tab_ref.shape[1]
    ids = idx_ref.at[t]
    for k in range(k0, k1):
        row = tab_ref[ids[k]]
        g_ref[pl.ds(k, rows, stride=GATHER_STRIDE), :] = row if after is None else row + after


def _zero_after(x):
    bits = pltpu.bitcast(x, jnp.uint32)
    return pltpu.bitcast(lax.shift_right_logical(bits, jnp.full_like(bits, 32)), jnp.int32)


def _chunk_matrix(g_ref, c, nk):
    return pltpu.bitcast(g_ref[GATHER_STRIDE * c:GATHER_STRIDE * c + nk, :], BF16)


def _token_pipeline(idx_ref, idx_next_ref, tab_ref, tm, nk, prepare, compute_chunk, finish, anchor, tiles):
    nchunk = tab_ref.shape[1]
    grp = PEER_GROUP
    per = nk // PEER_GATHER_PARTS
    set_a, set_b = tiles[:grp], tiles[grp:]
    ngroups = tm // grp
    assert ngroups % 2 == 0, "every token block must start on tile set A"

    @pl.when(pl.program_id(0) == 0)
    def _():
        for i in range(grp):
            _gather_rows(idx_ref, tab_ref, set_a[i], i, 0, nk)

    def group(t0, cur, nxt, ids_ref, t_ids):
        dots = [(i, c) for i in range(grp) for c in range(nchunk)]
        parts = [(i, j) for i in range(grp) for j in range(PEER_GATHER_PARTS)]
        acc = [prepare(t0 + i) for i in range(grp)]
        done = []
        for n in range(max(len(dots), len(parts))):
            if n < len(dots):
                i, c = dots[n]
                acc[i] = compute_chunk(t0 + i, c, _chunk_matrix(cur[i], c, nk), acc[i])
                done.append(anchor(acc[i]))
                if c == nchunk - 1:
                    finish(t0 + i, acc[i])
            if n < len(parts):
                i, j = parts[n]
                m = min(n - PEER_ANCHOR_LAG, len(dots) - 1)
                after = _zero_after(done[m]) if m >= 0 and (m < len(dots) - 1 or n == len(parts) - 1) else None
                _gather_rows(ids_ref, tab_ref, nxt[i], t_ids + i, j * per, (j + 1) * per, after)

    def step(g, carry):
        @pl.when(g % 2 == 0)
        def _():
            group(g * grp, set_a, set_b, idx_ref, (g + 1) * grp)

        @pl.when(g % 2 == 1)
        def _():
            group(g * grp, set_b, set_a, idx_ref, (g + 1) * grp)

        return carry

    lax.fori_loop(0, ngroups - 1, step, 0)
    group((ngroups - 1) * grp, set_b, set_a, idx_next_ref, 0)


def _peer_u_kernel(idx_ref, idx_next_ref, tab_ref, th_ref, gate_ref, mask_ref, pool_ref, o_ref, s_ref, *tiles,
                   tm, nk):
    nchunk = tab_ref.shape[1]
    nt = (((1,), (1,)), ((), ()))

    def prepare(t):
        x = th_ref[t]
        hi = x.astype(BF16).astype(F32)
        lhs = jnp.concatenate([hi, x - hi], axis=0).astype(BF16)
        return lhs, jnp.zeros((SUBLANES, 2 * nk), F32)

    def compute_chunk(t, c, w, acc):
        lhs, s = acc
        out = lax.dot_general(lhs, w, nt, preferred_element_type=F32)
        return lhs, s + (out[:SUBLANES] + out[SUBLANES:]) * mask_ref[c]

    def finish(t, acc):
        s_ref[pl.ds(t, 1), :] = jnp.sum(acc[1], axis=0, keepdims=True)

    _token_pipeline(idx_ref, idx_next_ref, tab_ref, tm, nk, prepare, compute_chunk, finish,
                    lambda acc: acc[1][:nchunk, :LANES], tiles)
    act = jnp.dot(s_ref[...], pool_ref[...], precision=HIGHEST, preferred_element_type=F32)
    o_ref[...] = _gelu(act) * gate_ref[...]


def _peer_v_kernel(idx_ref, idx_next_ref, tab_ref, a_ref, elo_ref, ehi_ref, o_ref, x_ref, *tiles, tm, nk):
    nchunk = tab_ref.shape[1]
    for j, p in enumerate(_split3(a_ref[...])):
        x_ref[j] = jnp.dot(p, elo_ref[...], preferred_element_type=F32)
        x_ref[3 + j] = jnp.dot(p, ehi_ref[...], preferred_element_type=F32)

    def prepare(t):
        return (jnp.concatenate([x_ref[r, pl.ds(t, 1), :] for r in range(6)]
                                + [jnp.zeros((SUBLANES - 6, 2 * nk), F32)], axis=0).astype(BF16), None)

    def compute_chunk(t, c, w, state):
        lhs = state[0]
        out = jnp.dot(lhs, w, preferred_element_type=F32)
        o_ref[t, pl.ds(c, 1), :] = out[0:1] + out[1:2] + out[2:3]
        o_ref[t, pl.ds(nchunk + c, 1), :] = out[3:4] + out[4:5] + out[5:6]
        return lhs, out

    _token_pipeline(idx_ref, idx_next_ref, tab_ref, tm, nk, prepare, compute_chunk, lambda t, state: None,
                    lambda state: state[1][:nchunk], tiles)


def _pack_table(tab):
    e, d = tab.shape
    half = d // 2
    b = lax.bitcast_convert_type(tab.astype(BF16), jnp.uint16).astype(jnp.uint32)
    w = b[:, :half] | (b[:, half:] << 16)
    return lax.bitcast_convert_type(w, jnp.int32).reshape(e, half // LANES, LANES)


def peer_experts(idx, gate, th, u_tab, v_tab):
    n, nk = idx.shape
    tm = PEER_TOK_TILE
    nchunk = th.shape[1] // 2
    up, vp = _pack_table(u_tab), _pack_table(v_tab)
    rho = np.arange(2 * nk)
    mask = np.zeros((nchunk, SUBLANES, 2 * nk), np.float32)
    for c in range(nchunk):
        mask[c, c, rho % 2 == 0] = 1.0
        mask[c, nchunk + c, rho % 2 == 1] = 1.0
    pool = (rho[:, None] // 2 == np.arange(nk)[None, :]).astype(np.float32)
    elo = (np.arange(nk)[:, None] * 2 == rho[None, :]).astype(np.float32)
    ehi = (np.arange(nk)[:, None] * 2 + 1 == rho[None, :]).astype(np.float32)
    smem = pl.BlockSpec((tm, nk), lambda i: (i, 0), memory_space=pltpu.SMEM)
    assert PEER_GROUP <= SUBLANES
    smem_next = pl.BlockSpec((SUBLANES, nk), lambda i: (jnp.minimum((i + 1) * (tm // SUBLANES), n // SUBLANES - 1), 0),
                             memory_space=pltpu.SMEM)
    tab_spec = pl.BlockSpec(up.shape, lambda i: (0, 0, 0), pipeline_mode=pl.Buffered(1))
    row = pl.BlockSpec((tm, nk), lambda i: (i, 0))
    th_spec = pl.BlockSpec((tm,) + th.shape[1:], lambda i: (i, 0, 0))
    gather_tiles = [pltpu.VMEM((GATHER_STRIDE * nchunk, LANES), jnp.int32)] * (2 * PEER_GROUP)
    act = pl.pallas_call(
        functools.partial(_peer_u_kernel, tm=tm, nk=nk),
        grid=(n // tm,),
        in_specs=[smem, smem_next, tab_spec, th_spec, row, _full(mask.shape), _full(pool.shape)],
        out_specs=row,
        out_shape=jax.ShapeDtypeStruct((n, nk), F32),
        scratch_shapes=[pltpu.VMEM((tm, 2 * nk), F32)] + gather_tiles,
        compiler_params=_cparams("arbitrary"),
        name="peer_u",
    )(idx, idx, up, th, gate, jnp.asarray(mask), jnp.asarray(pool))
    return pl.pallas_call(
        functools.partial(_peer_v_kernel, tm=tm, nk=nk),
        grid=(n // tm,),
        in_specs=[smem, smem_next, tab_spec, row, _full(elo.shape), _full(ehi.shape)],
        out_specs=th_spec,
        out_shape=jax.ShapeDtypeStruct(th.shape, F32),
        scratch_shapes=[pltpu.VMEM((6, tm, 2 * nk), F32)] + gather_tiles,
        compiler_params=_cparams("arbitrary"),
        name="peer_v",
    )(idx, idx, vp, act, jnp.asarray(elo, BF16), jnp.asarray(ehi, BF16))


def peer_channel(t, w_q, keys, u_tab, v_tab):
    n, d = t.shape
    e, g = peer_route(t, w_q, keys)
    f = peer_experts(e.T, g.T, t.reshape(n, d // LANES, LANES), u_tab, v_tab)
    return f.reshape(n, d)


def _mla_proj_kernel(h_ref, mod_ref, ck_ref, sk_ref, cq_ref, sq_ref, wdc_ref, wdr_ref, wdrs_ref, kvn_ref,
                     wuk_ref, wuv_ref, wdq_ref, qn_ref, wqn_ref, wqr_ref, wqrs_ref, k_ref, v_ref, q_ref, *, scale):
    m = (h_ref[...] * (1.0 + mod_ref[0, 1:2, :]) + mod_ref[0, 0:1, :]).astype(BF16)
    ckv = _rms_norm(jnp.dot(m, wdc_ref[...], preferred_element_type=F32), kvn_ref[...]).astype(BF16)
    kr = (jnp.dot(m, wdr_ref[...], preferred_element_type=F32) * ck_ref[...]
          + jnp.dot(m, wdrs_ref[...], preferred_element_type=F32) * sk_ref[...]).astype(BF16)
    kn = jnp.dot(ckv, wuk_ref[...], preferred_element_type=F32).astype(BF16)
    vt = lax.dot_general(wuv_ref[...], ckv, (((1,), (1,)), ((), ())),
                         preferred_element_type=F32).astype(BF16)
    cq = _rms_norm(jnp.dot(m, wdq_ref[...], preferred_element_type=F32), qn_ref[...]).astype(BF16)
    qn = (jnp.dot(cq, wqn_ref[...], preferred_element_type=F32) * scale).astype(BF16)
    qr = ((jnp.dot(cq, wqr_ref[...], preferred_element_type=F32) * cq_ref[...]
           + jnp.dot(cq, wqrs_ref[...], preferred_element_type=F32) * sq_ref[...]) * scale).astype(BF16)
    for h in range(MLA_HEADS):
        k_ref[0, h, :, 0:MLA_NOPE] = kn[:, h * MLA_NOPE:(h + 1) * MLA_NOPE]
        k_ref[0, h, :, MLA_NOPE:] = kr
        v_ref[0, h, 0:MLA_V, :] = vt[h * MLA_V:(h + 1) * MLA_V, :]
        v_ref[0, h, MLA_V:, :] = (lax.broadcasted_iota(jnp.int32, (V_PAD_ROWS, vt.shape[1]), 0) == 0).astype(BF16)
        q_ref[0, h, :, 0:MLA_NOPE] = qn[:, h * MLA_NOPE:(h + 1) * MLA_NOPE]
        q_ref[0, h, :, MLA_NOPE:] = qr[:, h * MLA_ROPE:(h + 1) * MLA_ROPE]


def _swap_halves(w, width):
    r = w.reshape(w.shape[0], -1, 2, width // 2)
    return r[:, :, ::-1, :].reshape(w.shape)


def mla_project(x, mod_tiles, ctx_len, nbatch, w_dq, q_norm, w_uq, w_dkv, kv_norm, w_ukv):
    n, d = x.shape
    ltot = n // nbatch
    nlat = ltot - ctx_len
    tm = ROW_TILE
    tpb = ltot // tm
    hd = MLA_NOPE + MLA_ROPE
    kvl = kv_norm.shape[0]
    ql = q_norm.shape[0]
    pos = jnp.arange(nlat, dtype=jnp.int32)
    nf = MLA_ROPE // 4
    inv = ROPE_BASE ** (-jnp.arange(nf, dtype=F32) / nf)
    ang = jnp.concatenate([(pos // GRID_W).astype(F32)[:, None] * inv,
                           (pos % GRID_W).astype(F32)[:, None] * inv], axis=-1)
    cos = jnp.concatenate([jnp.ones((ctx_len, MLA_ROPE // 2), F32), jnp.cos(ang)], axis=0)
    sin = jnp.concatenate([jnp.zeros((ctx_len, MLA_ROPE // 2), F32), jnp.sin(ang)], axis=0)
    ck = jnp.concatenate([cos, cos], axis=1)
    sk = jnp.concatenate([-sin, sin], axis=1)
    cq = jnp.tile(ck, (1, MLA_HEADS))
    sq = jnp.tile(sk, (1, MLA_HEADS))
    w_dkv_c = w_dkv[:, :kvl].astype(BF16)
    w_dkv_r = w_dkv[:, kvl:]
    ukv = w_ukv.reshape(kvl, MLA_HEADS, MLA_NOPE + MLA_V)
    w_uk = ukv[:, :, :MLA_NOPE].reshape(kvl, -1).astype(BF16)
    w_uv = ukv[:, :, MLA_NOPE:].reshape(kvl, -1).T.astype(BF16)
    uq = w_uq.reshape(ql, MLA_HEADS, hd)
    w_qn = uq[:, :, :MLA_NOPE].reshape(ql, -1).astype(BF16)
    w_qr = uq[:, :, MLA_NOPE:].reshape(ql, -1)
    args = (x, mod_tiles, ck, sk, cq, sq, w_dkv_c, w_dkv_r.astype(BF16),
            _swap_halves(w_dkv_r, MLA_ROPE).astype(BF16), kv_norm.reshape(1, kvl), w_uk, w_uv,
            w_dq.astype(BF16), q_norm.reshape(1, ql), w_qn, w_qr.astype(BF16),
            _swap_halves(w_qr, MLA_ROPE).astype(BF16))
    pos_spec = lambda w: pl.BlockSpec((tm, w), lambda i: (i % tpb, 0))
    in_specs = [pl.BlockSpec((tm, d), lambda i: (i, 0)), pl.BlockSpec((1, N_MOD, d), lambda i: (i, 0, 0)),
                pos_spec(MLA_ROPE), pos_spec(MLA_ROPE), pos_spec(MLA_ROPE * MLA_HEADS),
                pos_spec(MLA_ROPE * MLA_HEADS)] + [_full(a.shape) for a in args[6:]]
    head_spec = lambda w: pl.BlockSpec((1, MLA_HEADS, tm, w), lambda i: (i // tpb, 0, i % tpb, 0))
    return pl.pallas_call(
        functools.partial(_mla_proj_kernel, scale=float(hd) ** -0.5 * math.log2(math.e)),
        grid=(n // tm,),
        in_specs=in_specs,
        out_specs=[head_spec(hd),
                   pl.BlockSpec((1, MLA_HEADS, MLA_V + V_PAD_ROWS, tm), lambda i: (i // tpb, 0, 0, i % tpb)),
                   head_spec(hd)],
        out_shape=[jax.ShapeDtypeStruct((nbatch, MLA_HEADS, ltot, hd), BF16),
                   jax.ShapeDtypeStruct((nbatch, MLA_HEADS, MLA_V + V_PAD_ROWS, ltot), BF16),
                   jax.ShapeDtypeStruct((nbatch, MLA_HEADS, ltot, hd), BF16)],
        compiler_params=_cparams("arbitrary"),
        name="mla_project",
    )(*args)


def _attn_kernel(q_ref, k_ref, vt_ref, o_ref, *, tk, nchunks):
    q = q_ref[0, 0]
    tq = q.shape[0]
    nt = (((1,), (1,)), ((), ()))

    def scores(c):
        return lax.dot_general(k_ref[0, 0, c * tk:(c + 1) * tk, :], q, nt, preferred_element_type=F32)

    m_i = jnp.full((1, tq), -jnp.inf, F32)
    acc = jnp.zeros((vt_ref.shape[2], tq), F32)
    s = scores(0)
    for c in range(nchunks):
        s_next = scores(c + 1) if c + 1 < nchunks else None
        m_new = jnp.maximum(m_i, jnp.max(s, axis=0, keepdims=True))
        p = jnp.exp2(s - m_new).astype(BF16)
        acc = jnp.exp2(m_i - m_new) * acc + jnp.dot(vt_ref[0, 0, :, c * tk:(c + 1) * tk], p,
                                                    preferred_element_type=F32)
        m_i, s = m_new, s_next
    o_ref[0] = (acc[:MLA_V] / acc[MLA_V:MLA_V + 1]).T


def mla_attend(q, k, vt, ctx_len):
    nb, nh, ltot, hd = k.shape
    tq = ROW_TILE
    assert ctx_len % tq == 0
    skip = ctx_len // tq
    nlat = ltot - ctx_len
    tk = ATT_KV_CHUNK
    assert ltot % tk == 0
    return pl.pallas_call(
        functools.partial(_attn_kernel, tk=tk, nchunks=ltot // tk),
        grid=(nb, nh, nlat // tq),
        in_specs=[pl.BlockSpec((1, 1, tq, hd), lambda b, h, i: (b, h, i + skip, 0)),
                  pl.BlockSpec((1, 1, ltot, hd), lambda b, h, i: (b, h, 0, 0)),
                  pl.BlockSpec((1, 1, MLA_V + V_PAD_ROWS, ltot), lambda b, h, i: (b, h, 0, 0))],
        out_specs=pl.BlockSpec((1, tq, MLA_V), lambda b, h, i: (b, i, h)),
        out_shape=jax.ShapeDtypeStruct((nb, nlat, nh * MLA_V), F32),
        compiler_params=_cparams("arbitrary", "arbitrary", "arbitrary"),
        name="mla_attend",
    )(q, k, vt)


def kernel(x, c, ctx, c_ctx, ada_w, ada_b, ln_g, ln_b, s5_a_re, s5_a_im, s5_log_dt, s5_b_re, s5_b_im, s5_c_re, s5_c_im, s5_d, s5_w_glu, s5_w_o, mla_w_dq, mla_q_norm, mla_w_uq, mla_w_dkv, mla_kv_norm, mla_w_ukv, mla_w_o, peer_w_q, peer_keys, peer_u, peer_v):
    nb, nlat, d = x.shape
    nctx = ctx.shape[1]
    ltot = nctx + nlat
    depth = ada_w.shape[0]
    alpha = (2 * depth) ** 0.25
    tm = ROW_TILE

    cvec = jnp.zeros((SUBLANES, d), F32).at[:nb].set(c).at[nb].set(c_ctx)
    tiles_per_batch = ltot // tm
    tile_rows_all = np.array([nb if j < nctx // tm else b for b in range(nb) for j in range(tiles_per_batch)])
    tile_rows_lat = np.array([b for b in range(nb) for _ in range(nlat // tm)])
    seg_rows = np.array([[nb] * nb, list(range(nb))])

    h_all = jnp.concatenate([ctx, x], axis=1).reshape(nb * ltot, d)

    mod = ada_mod(cvec, ada_w[0], ada_b[0]).reshape(SUBLANES, N_MOD, d)
    mod_all = mod[tile_rows_all]
    bfc, ccc, ar, ai = s5_prepare(s5_a_re[0], s5_a_im[0], s5_log_dt[0], s5_b_re[0], s5_b_im[0],
                                  s5_c_re[0], s5_c_im[0], nb)
    yf, yb = s5_scan(h_all.reshape(nb, ltot, d), mod[seg_rows], bfc, ccc, ar, ai, nctx)
    h1, t = post_mixer(h_all, (yf.reshape(-1, d), yb.reshape(-1, d)), mod_all, s5_w_o[0], ln_g[0, 0], ln_b[0, 0],
                       alpha, d_skip=s5_d[0], w_glu=s5_w_glu[0])
    f = peer_channel(t, peer_w_q[0], peer_keys[0], peer_u[0], peer_v[0])
    h_all = residual_ln(h1, f, mod_all, ln_g[0, 1], ln_b[0, 1], alpha)

    mod = ada_mod(cvec, ada_w[1], ada_b[1]).reshape(SUBLANES, N_MOD, d)
    k, v, q = mla_project(h_all, mod[tile_rows_all], nctx, nb, mla_w_dq[0], mla_q_norm[0], mla_w_uq[0],
                          mla_w_dkv[0], mla_kv_norm[0], mla_w_ukv[0])
    att = mla_attend(q, k, v, nctx).reshape(nb * nlat, -1)
    h_lat = h_all.reshape(nb, ltot, d)[:, nctx:].reshape(nb * nlat, d)
    mod_lat = mod[tile_rows_lat]
    h1, t = post_mixer(h_lat, (att,), mod_lat, mla_w_o[0], ln_g[1, 0], ln_b[1, 0], alpha)
    f = peer_channel(t, peer_w_q[1], peer_keys[1], peer_u[1], peer_v[1])
    out = residual_ln(h1, f, mod_lat, ln_g[1, 1], ln_b[1, 1], alpha)
    return out.reshape(nb, nlat, d).astype(x.dtype)
```

```python
import functools
import math

import jax
import jax.numpy as jnp
import numpy as np
from jax import lax
from jax.experimental import pallas as pl
from jax.experimental.pallas import tpu as pltpu

F32 = jnp.float32
BF16 = jnp.bfloat16
HIGHEST = lax.Precision.HIGHEST

LANES = 128
SUBLANES = 8
VMEM_LIMIT_BYTES = 56 * 1024 * 1024

N_MOD = 6
GRID_W = 64
S5_GROUP = 16
S5_STATE = 64
MLA_HEADS = 8
MLA_NOPE = 128
MLA_ROPE = 64
MLA_V = 128
ROPE_BASE = 10000.0
PEER_HEADS = 8
PEER_NKEYS = 128
PEER_DK = 128
PEER_TOPK = 16
LN_EPS = 1e-5
RMS_EPS = 1e-6

ROW_TILE = 256
S5_CHUNK = 128
PEER_TOK_TILE = 64
ATT_KV_CHUNK = 2816
V_PAD_ROWS = 16


def _cparams(*sem, flags=None):
    return pltpu.CompilerParams(dimension_semantics=sem, vmem_limit_bytes=VMEM_LIMIT_BYTES, flags=flags)


def _full(shape):
    n = len(shape)
    return pl.BlockSpec(shape, lambda *_: (0,) * n)


def _gelu(x):
    return 0.5 * x * (1.0 + lax.erf(x * (1.0 / math.sqrt(2.0))))


def _layer_norm(x, g, b):
    mu = jnp.mean(x, axis=-1, keepdims=True)
    xc = x - mu
    var = jnp.mean(xc * xc, axis=-1, keepdims=True)
    return xc * lax.rsqrt(var + LN_EPS) * g + b


def _rms_norm(x, g):
    return x * lax.rsqrt(jnp.mean(x * x, axis=-1, keepdims=True) + RMS_EPS) * g


def _bdot(a, b):
    return jnp.dot(a.astype(BF16), b, preferred_element_type=F32)


def _ada_kernel(c_ref, w_ref, b_ref, o_ref):
    c = c_ref[...]
    s = c * jax.nn.sigmoid(c)
    o_ref[...] = jnp.dot(s, w_ref[...], precision=HIGHEST, preferred_element_type=F32) + b_ref[...]


def ada_mod(cvec, w, b):
    d = cvec.shape[1]
    n = w.shape[1]
    tn = 1024
    return pl.pallas_call(
        _ada_kernel,
        grid=(n // tn,),
        in_specs=[_full((SUBLANES, d)), pl.BlockSpec((d, tn), lambda j: (0, j)),
                  pl.BlockSpec((1, tn), lambda j: (0, j))],
        out_specs=pl.BlockSpec((SUBLANES, tn), lambda j: (0, j)),
        out_shape=jax.ShapeDtypeStruct((SUBLANES, n), F32),
        compiler_params=_cparams("arbitrary"),
        name="ada_mod",
    )(cvec, w, b.reshape(1, n))


def _s5_kernel(xf_ref, xb_ref, mod_ref, bf_ref, cc_ref, ar_ref, ai_ref, yf_ref, yb_ref, bu_ref, x_ref,
               *, chunk, nbatch):
    nlc = bf_ref.shape[1]
    spc = bf_ref.shape[3] // (2 * LANES)

    @pl.when(pl.program_id(0) == 0)
    def _():
        x_ref[...] = jnp.zeros_like(x_ref)

    for d, src in enumerate((xf_ref, xb_ref)):
        u = jnp.concatenate([(src[b] * (1.0 + mod_ref[0, b, 1:2, :]) + mod_ref[0, b, 0:1, :]).astype(BF16)
                             for b in range(nbatch)], axis=0)
        for j in range(nlc):
            res = jnp.dot(u[:, LANES * j:LANES * (j + 1)], bf_ref[d, j], preferred_element_type=F32)
            for b in range(nbatch):
                for part in range(2):
                    r = d * nbatch + b + 2 * nbatch * part
                    for k in range(spc):
                        col = (part * spc + k) * LANES
                        bu_ref[spc * j + k, pl.ds(r, chunk, stride=SUBLANES), :] = (
                            res[b * chunk:(b + 1) * chunk, col:col + LANES])

    ar = ar_ref[...]
    ai = ai_ref[...]
    row = lax.broadcasted_iota(jnp.int32, x_ref.shape, 1)
    is_bwd = (row % (2 * nbatch)) >= nbatch

    def step(s, x):
        off_f = pl.multiple_of(s * SUBLANES, SUBLANES)
        off_b = pl.multiple_of((chunk - 1 - s) * SUBLANES, SUBLANES)
        slab_f = bu_ref[:, pl.ds(off_f, SUBLANES), :]
        slab_b = bu_ref[:, pl.ds(off_b, SUBLANES), :]
        xn = ar * x + ai * pltpu.roll(x, SUBLANES // 2, axis=1) + jnp.where(is_bwd, slab_b, slab_f)
        bu_ref[:, pl.ds(off_f, SUBLANES), :] = jnp.where(is_bwd, slab_f, xn)
        bu_ref[:, pl.ds(off_b, SUBLANES), :] = jnp.where(is_bwd, xn, slab_b)
        return xn

    x_ref[...] = lax.fori_loop(0, chunk, step, x_ref[...])

    def state_rows(r, j):
        return jnp.concatenate([bu_ref[spc * j + k, pl.ds(r, chunk, stride=SUBLANES), :] for k in range(spc)],
                               axis=1).astype(BF16)

    for d, out in enumerate((yf_ref, yb_ref)):
        for j in range(nlc):
            h = jnp.concatenate(
                [jnp.concatenate([state_rows(d * nbatch + b, j), state_rows(d * nbatch + b + 2 * nbatch, j)], axis=1)
                 for b in range(nbatch)], axis=0)
            y = jnp.dot(h, cc_ref[d, j], preferred_element_type=F32)
            for b in range(nbatch):
                out[b, :, LANES * j:LANES * (j + 1)] = y[b * chunk:(b + 1) * chunk]


def s5_scan(x, mod_seg, bfc, ccc, ar, ai, ctx_len):
    nb, ltot, d = x.shape
    assert 2 * nb * 2 == SUBLANES, "state rows must fill one sublane tile"
    chunk = S5_CHUNK
    assert ltot % chunk == 0 and ctx_len % chunk == 0
    ctx_chunks = ctx_len // chunk
    nchunks = ltot // chunk
    nslab = ar.shape[0]
    fwd_spec = pl.BlockSpec((nb, chunk, d), lambda c: (0, c, 0))
    bwd_spec = pl.BlockSpec((nb, chunk, d), lambda c: (0, (nchunks - 1 - c + ctx_chunks) % nchunks, 0))
    return pl.pallas_call(
        functools.partial(_s5_kernel, chunk=chunk, nbatch=nb),
        grid=(nchunks,),
        in_specs=[fwd_spec, bwd_spec,
                  pl.BlockSpec((1, nb, N_MOD, d), lambda c: (jnp.minimum(c // ctx_chunks, 1), 0, 0, 0)),
                  _full(bfc.shape), _full(ccc.shape), _full(ar.shape), _full(ai.shape)],
        out_specs=[fwd_spec, bwd_spec],
        out_shape=[jax.ShapeDtypeStruct(x.shape, F32)] * 2,
        scratch_shapes=[pltpu.VMEM((nslab, SUBLANES * chunk, LANES), F32),
                        pltpu.VMEM((nslab, SUBLANES, LANES), F32)],
        compiler_params=_cparams("arbitrary"),
        name="s5_scan",
    )(x, x, mod_seg, bfc, ccc, ar, ai)


def s5_prepare(a_re, a_im, log_dt, b_re, b_im, c_re, c_im, nbatch):
    ndir, g, p = a_re.shape
    gc = b_re.shape[-1]
    gpc = LANES // gc
    nlc = g // gpc
    dt = jnp.exp(log_dt.astype(F32))[..., None]
    a_re = a_re.astype(F32)
    a_im = a_im.astype(F32)
    mag = jnp.exp(a_re * dt)
    ab_re = mag * jnp.cos(a_im * dt)
    ab_im = mag * jnp.sin(a_im * dt)
    nr, ni = ab_re - 1.0, ab_im
    den = a_re * a_re + a_im * a_im
    f_re = (nr * a_re + ni * a_im) / den
    f_im = (ni * a_re - nr * a_im) / den
    bf_re = f_re[..., None] * b_re - f_im[..., None] * b_im
    bf_im = f_re[..., None] * b_im + f_im[..., None] * b_re
    bf = jnp.stack([bf_re, bf_im], axis=1).reshape(ndir, 2, nlc, gpc, p, gc)
    eye = jnp.eye(gpc, dtype=F32)
    bfc = jnp.einsum('dqjgpc,gh->djgcqhp', bf, eye).reshape(ndir, nlc, gpc * gc, 2 * gpc * p).astype(BF16)
    cc = jnp.stack([c_re, -c_im], axis=1).astype(F32).reshape(ndir, 2, nlc, gpc, gc, p)
    ccc = jnp.einsum('dqjgcp,gh->djqgphc', cc, eye).reshape(ndir, nlc, 2 * gpc * p, gpc * gc).astype(BF16)
    rows_r, rows_i = [], []
    for part in range(2):
        for d in range(ndir):
            for _ in range(nbatch):
                rows_r.append(ab_re[d].reshape(-1))
                rows_i.append(ab_im[d].reshape(-1) * (-1.0 if part == 0 else 1.0))
    slabs = lambda rows: jnp.stack(rows).reshape(len(rows), -1, LANES).transpose(1, 0, 2)
    return bfc, ccc, slabs(rows_r), slabs(rows_i)


def _post_kernel(*refs, alpha, glu):
    if glu:
        h_ref, yf_ref, yb_ref, mod_ref, dsk_ref, wg_ref, wo_ref, lng_ref, lnb_ref, h1_ref, t_ref = refs
        h = h_ref[...]
        m = h * (1.0 + mod_ref[0, 1:2, :]) + mod_ref[0, 0:1, :]
        z = _gelu(yf_ref[...] + yb_ref[...] + dsk_ref[...] * m)
        z = z * jax.nn.sigmoid(_bdot(z, wg_ref[...]))
    else:
        h_ref, z_ref, mod_ref, wo_ref, lng_ref, lnb_ref, h1_ref, t_ref = refs
        h = h_ref[...]
        z = z_ref[...]
    o = _bdot(z, wo_ref[...])
    h1 = _layer_norm(alpha * h + mod_ref[0, 2:3, :] * o, lng_ref[...], lnb_ref[...])
    h1_ref[...] = h1
    t_ref[...] = h1 * (1.0 + mod_ref[0, 4:5, :]) + mod_ref[0, 3:4, :]


def post_mixer(h, pre, mod_tiles, w_o, ln_g, ln_b, alpha, d_skip=None, w_glu=None):
    n, d = h.shape
    tm = ROW_TILE
    glu = w_glu is not None
    row = pl.BlockSpec((tm, d), lambda i: (i, 0))
    vec = _full((1, d))
    modspec = pl.BlockSpec((1, N_MOD, d), lambda i: (i, 0, 0))
    if glu:
        args = (h, pre[0], pre[1], mod_tiles, d_skip.reshape(1, d), w_glu.astype(BF16), w_o.astype(BF16),
                ln_g.reshape(1, d), ln_b.reshape(1, d))
        in_specs = [row, row, row, modspec, vec, _full((d, d)), _full((d, d)), vec, vec]
    else:
        args = (h, pre[0], mod_tiles, w_o.astype(BF16), ln_g.reshape(1, d), ln_b.reshape(1, d))
        in_specs = [row, pl.BlockSpec((tm, pre[0].shape[1]), lambda i: (i, 0)), modspec,
                    _full(w_o.shape), vec, vec]
    return pl.pallas_call(
        functools.partial(_post_kernel, alpha=alpha, glu=glu),
        grid=(n // tm,),
        in_specs=in_specs,
        out_specs=[row, row],
        out_shape=[jax.ShapeDtypeStruct((n, d), F32)] * 2,
        compiler_params=_cparams("arbitrary"),
        name="post_mixer_glu" if glu else "post_mixer",
    )(*args)


def _res_ln_kernel(h_ref, f_ref, mod_ref, lng_ref, lnb_ref, o_ref, *, alpha):
    o_ref[...] = _layer_norm(alpha * h_ref[...] + mod_ref[0, 5:6, :] * f_ref[...], lng_ref[...], lnb_ref[...])


def residual_ln(h, f, mod_tiles, ln_g, ln_b, alpha):
    n, d = h.shape
    tm = ROW_TILE
    row = pl.BlockSpec((tm, d), lambda i: (i, 0))
    vec = _full((1, d))
    return pl.pallas_call(
        functools.partial(_res_ln_kernel, alpha=alpha),
        grid=(n // tm,),
        in_specs=[row, row, pl.BlockSpec((1, N_MOD, d), lambda i: (i, 0, 0)), vec, vec],
        out_specs=row,
        out_shape=jax.ShapeDtypeStruct((n, d), F32),
        compiler_params=_cparams("arbitrary"),
        name="residual_ln",
    )(h, f, mod_tiles, ln_g.reshape(1, d), ln_b.reshape(1, d))


def _topk_rows(s, k, payload=None):
    nrow = s.shape[0]
    iota = lax.broadcasted_iota(jnp.int32, s.shape, 0).astype(F32)
    vals, idxs, pays = [], [], []
    for _ in range(k):
        m = jnp.max(s, axis=0, keepdims=True)
        i = jnp.min(jnp.where(s == m, iota, float(nrow)), axis=0, keepdims=True)
        hit = iota == i
        vals.append(m)
        idxs.append(i)
        if payload is not None:
            pays.append(jnp.max(jnp.where(hit, payload, -1.0), axis=0, keepdims=True))
        s = jnp.where(hit, -jnp.inf, s)
    return vals, idxs, pays


def _route_kernel(t_ref, wqt_ref, keys_ref, e_ref, g_ref):
    nt = (((1,), (1,)), ((), ()))
    qt = lax.dot_general(wqt_ref[...], t_ref[...], nt, precision=HIGHEST, preferred_element_type=F32)
    half = PEER_DK // 2
    for h in range(PEER_HEADS):
        sv, si = [], []
        for s in range(2):
            row0 = (h * 2 + s) * half
            sc = jnp.dot(keys_ref[h * 2 + s], qt[row0:row0 + half, :], precision=HIGHEST,
                         preferred_element_type=F32)
            v, i, _ = _topk_rows(sc, PEER_TOPK)
            sv.append(v)
            si.append(i)
        v2 = jnp.concatenate(sv[1], axis=0)
        i2 = jnp.concatenate(si[1], axis=0)
        width = [PEER_TOPK // (a + 1) for a in range(PEER_TOPK)]
        npad = -sum(width) % SUBLANES
        tm = v2.shape[1]
        cand = jnp.concatenate([sv[0][a] + v2[:width[a]] for a in range(PEER_TOPK)]
                               + [jnp.full((npad, tm), -jnp.inf, F32)], axis=0)
        cexp = jnp.concatenate([si[0][a] * float(PEER_NKEYS) + i2[:width[a]] for a in range(PEER_TOPK)]
                               + [jnp.full((npad, tm), -1.0, F32)], axis=0)
        cv, _, ce = _topk_rows(cand, PEER_TOPK, payload=cexp)
        cvm = jnp.concatenate(cv, axis=0)
        ex = jnp.exp(cvm - cv[0])
        lo = h * PEER_TOPK
        g_ref[lo:lo + PEER_TOPK, :] = ex / jnp.sum(ex, axis=0, keepdims=True)
        e_ref[lo:lo + PEER_TOPK, :] = jnp.concatenate(ce, axis=0).astype(jnp.int32)


def peer_route(t, w_q, keys):
    n, d = t.shape
    tm = ROW_TILE
    hk = PEER_HEADS * PEER_TOPK
    keys2 = keys.reshape(PEER_HEADS * 2, PEER_NKEYS, PEER_DK // 2)
    out = pl.BlockSpec((hk, tm), lambda i: (0, i))
    return pl.pallas_call(
        _route_kernel,
        grid=(n // tm,),
        in_specs=[pl.BlockSpec((tm, d), lambda i: (i, 0)), _full((w_q.shape[1], d)), _full(keys2.shape)],
        out_specs=[out, out],
        out_shape=[jax.ShapeDtypeStruct((hk, n), jnp.int32), jax.ShapeDtypeStruct((hk, n), F32)],
        compiler_params=_cparams("arbitrary"),
        name="peer_route",
    )(t, w_q.T, keys2)


GATHER_STRIDE = PEER_HEADS * PEER_TOPK + SUBLANES


PEER_GROUP = 4
PEER_GATHER_PARTS = 16
PEER_ANCHOR_LAG = 12


def _split3(x):
    hi = x.astype(BF16)
    r1 = x - hi.astype(F32)
    mid = r1.astype(BF16)
    lo = (r1 - mid.astype(F32)).astype(BF16)
    return hi, mid, lo


def _gather_rows(idx_ref, tab_ref, g_ref, t, k0, k1, after=None):
    rows = tab_ref.shape[1]
    ids = idx_ref.at[t]
    for k in range(k0, k1):
        row = tab_ref[ids[k]]
        g_ref[pl.ds(k, rows, stride=GATHER_STRIDE), :] = row if after is None else row + after


def _zero_after(x):
    bits = pltpu.bitcast(x, jnp.uint32)
    return pltpu.bitcast(lax.shift_right_logical(bits, jnp.full_like(bits, 32)), jnp.int32)


def _chunk_matrix(g_ref, c, nk):
    return pltpu.bitcast(g_ref[GATHER_STRIDE * c:GATHER_STRIDE * c + nk, :], BF16)


def _token_pipeline(idx_ref, idx_next_ref, tab_ref, tm, nk, prepare, compute_chunk, finish, anchor, tiles):
    nchunk = tab_ref.shape[1]
    grp = PEER_GROUP
    per = nk // PEER_GATHER_PARTS
    set_a, set_b = tiles[:grp], tiles[grp:]
    ngroups = tm // grp
    assert ngroups % 2 == 0, "every token block must start on tile set A"

    @pl.when(pl.program_id(0) == 0)
    def _():
        for i in range(grp):
            _gather_rows(idx_ref, tab_ref, set_a[i], i, 0, nk)

    def group(t0, cur, nxt, ids_ref, t_ids):
        dots = [(i, c) for i in range(grp) for c in range(nchunk)]
        parts = [(i, j) for i in range(grp) for j in range(PEER_GATHER_PARTS)]
        acc = [prepare(t0 + i) for i in range(grp)]
        done = []
        for n in range(max(len(dots), len(parts))):
            if n < len(dots):
                i, c = dots[n]
                acc[i] = compute_chunk(t0 + i, c, _chunk_matrix(cur[i], c, nk), acc[i])
                done.append(anchor(acc[i]))
                if c == nchunk - 1:
                    finish(t0 + i, acc[i])
            if n < len(parts):
                i, j = parts[n]
                m = min(n - PEER_ANCHOR_LAG, len(dots) - 1)
                after = _zero_after(done[m]) if m >= 0 and (m < len(dots) - 1 or n == len(parts) - 1) else None
                _gather_rows(ids_ref, tab_ref, nxt[i], t_ids + i, j * per, (j + 1) * per, after)

    def step(g, carry):
        @pl.when(g % 2 == 0)
        def _():
            group(g * grp, set_a, set_b, idx_ref, (g + 1) * grp)

        @pl.when(g % 2 == 1)
        def _():
            group(g * grp, set_b, set_a, idx_ref, (g + 1) * grp)

        return carry

    lax.fori_loop(0, ngroups - 1, step, 0)
    group((ngroups - 1) * grp, set_b, set_a, idx_next_ref, 0)


def _peer_u_kernel(idx_ref, idx_next_ref, tab_ref, th_ref, gate_ref, mask_ref, pool_ref, o_ref, s_ref, *tiles,
                   tm, nk):
    nchunk = tab_ref.shape[1]
    nt = (((1,), (1,)), ((), ()))

    def prepare(t):
        x = th_ref[t]
        hi = x.astype(BF16).astype(F32)
        lhs = jnp.concatenate([hi, x - hi], axis=0).astype(BF16)
        return lhs, jnp.zeros((SUBLANES, 2 * nk), F32)

    def compute_chunk(t, c, w, acc):
        lhs, s = acc
        out = lax.dot_general(lhs, w, nt, preferred_element_type=F32)
        return lhs, s + (out[:SUBLANES] + out[SUBLANES:]) * mask_ref[c]

    def finish(t, acc):
        s_ref[pl.ds(t, 1), :] = jnp.sum(acc[1], axis=0, keepdims=True)

    _token_pipeline(idx_ref, idx_next_ref, tab_ref, tm, nk, prepare, compute_chunk, finish,
                    lambda acc: acc[1][:nchunk, :LANES], tiles)
    act = jnp.dot(s_ref[...], pool_ref[...], precision=HIGHEST, preferred_element_type=F32)
    o_ref[...] = _gelu(act) * gate_ref[...]


def _peer_v_kernel(idx_ref, idx_next_ref, tab_ref, a_ref, elo_ref, ehi_ref, o_ref, x_ref, *tiles, tm, nk):
    nchunk = tab_ref.shape[1]
    for j, p in enumerate(_split3(a_ref[...])):
        x_ref[j] = jnp.dot(p, elo_ref[...], preferred_element_type=F32)
        x_ref[3 + j] = jnp.dot(p, ehi_ref[...], preferred_element_type=F32)

    def prepare(t):
        return (jnp.concatenate([x_ref[r, pl.ds(t, 1), :] for r in range(6)]
                                + [jnp.zeros((SUBLANES - 6, 2 * nk), F32)], axis=0).astype(BF16), None)

    def compute_chunk(t, c, w, state):
        lhs = state[0]
        out = jnp.dot(lhs, w, preferred_element_type=F32)
        o_ref[t, pl.ds(c, 1), :] = out[0:1] + out[1:2] + out[2:3]
        o_ref[t, pl.ds(nchunk + c, 1), :] = out[3:4] + out[4:5] + out[5:6]
        return lhs, out

    _token_pipeline(idx_ref, idx_next_ref, tab_ref, tm, nk, prepare, compute_chunk, lambda t, state: None,
                    lambda state: state[1][:nchunk], tiles)


def _pack_kernel(x_ref, o_ref):
    half = x_ref.shape[1] // 2

    def bf16_bits(x):
        return pltpu.bitcast(x.astype(BF16).astype(F32), jnp.uint32)

    lo = lax.shift_right_logical(bf16_bits(x_ref[:, :half]), jnp.uint32(16))
    hi = bf16_bits(x_ref[:, half:]) & jnp.uint32(0xFFFF0000)
    o_ref[...] = pltpu.bitcast(lo | hi, jnp.int32)


def _pack_table(tab):
    e, d = tab.shape
    half = d // 2
    te = 512
    w = pl.pallas_call(
        _pack_kernel,
        grid=(e // te,),
        in_specs=[pl.BlockSpec((te, d), lambda i: (i, 0))],
        out_specs=pl.BlockSpec((te, half), lambda i: (i, 0)),
        out_shape=jax.ShapeDtypeStruct((e, half), jnp.int32),
        compiler_params=_cparams("arbitrary"),
        name="pack_table",
    )(tab)
    return w.reshape(e, half // LANES, LANES)


def peer_experts(idx, gate, th, u_tab, v_tab):
    n, nk = idx.shape
    tm = PEER_TOK_TILE
    nchunk = th.shape[1] // 2
    up, vp = _pack_table(u_tab), _pack_table(v_tab)
    rho = np.arange(2 * nk)
    mask = np.zeros((nchunk, SUBLANES, 2 * nk), np.float32)
    for c in range(nchunk):
        mask[c, c, rho % 2 == 0] = 1.0
        mask[c, nchunk + c, rho % 2 == 1] = 1.0
    pool = (rho[:, None] // 2 == np.arange(nk)[None, :]).astype(np.float32)
    elo = (np.arange(nk)[:, None] * 2 == rho[None, :]).astype(np.float32)
    ehi = (np.arange(nk)[:, None] * 2 + 1 == rho[None, :]).astype(np.float32)
    smem = pl.BlockSpec((tm, nk), lambda i: (i, 0), memory_space=pltpu.SMEM)
    assert PEER_GROUP <= SUBLANES
    smem_next = pl.BlockSpec((SUBLANES, nk), lambda i: (jnp.minimum((i + 1) * (tm // SUBLANES), n // SUBLANES - 1), 0),
                             memory_space=pltpu.SMEM)
    tab_spec = pl.BlockSpec(up.shape, lambda i: (0, 0, 0), pipeline_mode=pl.Buffered(1))
    row = pl.BlockSpec((tm, nk), lambda i: (i, 0))
    th_spec = pl.BlockSpec((tm,) + th.shape[1:], lambda i: (i, 0, 0))
    gather_tiles = [pltpu.VMEM((GATHER_STRIDE * nchunk, LANES), jnp.int32)] * (2 * PEER_GROUP)
    act = pl.pallas_call(
        functools.partial(_peer_u_kernel, tm=tm, nk=nk),
        grid=(n // tm,),
        in_specs=[smem, smem_next, tab_spec, th_spec, row, _full(mask.shape), _full(pool.shape)],
        out_specs=row,
        out_shape=jax.ShapeDtypeStruct((n, nk), F32),
        scratch_shapes=[pltpu.VMEM((tm, 2 * nk), F32)] + gather_tiles,
        compiler_params=_cparams("arbitrary"),
        name="peer_u",
    )(idx, idx, up, th, gate, jnp.asarray(mask), jnp.asarray(pool))
    return pl.pallas_call(
        functools.partial(_peer_v_kernel, tm=tm, nk=nk),
        grid=(n // tm,),
        in_specs=[smem, smem_next, tab_spec, row, _full(elo.shape), _full(ehi.shape)],
        out_specs=th_spec,
        out_shape=jax.ShapeDtypeStruct(th.shape, F32),
        scratch_shapes=[pltpu.VMEM((6, tm, 2 * nk), F32)] + gather_tiles,
        compiler_params=_cparams("arbitrary"),
        name="peer_v",
    )(idx, idx, vp, act, jnp.asarray(elo, BF16), jnp.asarray(ehi, BF16))


def peer_channel(t, w_q, keys, u_tab, v_tab):
    n, d = t.shape
    e, g = peer_route(t, w_q, keys)
    f = peer_experts(e.T, g.T, t.reshape(n, d // LANES, LANES), u_tab, v_tab)
    return f.reshape(n, d)


def _mla_proj_kernel(h_ref, mod_ref, ck_ref, sk_ref, cq_ref, sq_ref, wdc_ref, wdr_ref, wdrs_ref, kvn_ref,
                     wuk_ref, wuv_ref, wdq_ref, qn_ref, wqn_ref, wqr_ref, wqrs_ref, k_ref, v_ref, q_ref, *, scale):
    m = (h_ref[...] * (1.0 + mod_ref[0, 1:2, :]) + mod_ref[0, 0:1, :]).astype(BF16)
    ckv = _rms_norm(jnp.dot(m, wdc_ref[...], preferred_element_type=F32), kvn_ref[...]).astype(BF16)
    kr = (jnp.dot(m, wdr_ref[...], preferred_element_type=F32) * ck_ref[...]
          + jnp.dot(m, wdrs_ref[...], preferred_element_type=F32) * sk_ref[...]).astype(BF16)
    kn = jnp.dot(ckv, wuk_ref[...], preferred_element_type=F32).astype(BF16)
    vt = lax.dot_general(wuv_ref[...], ckv, (((1,), (1,)), ((), ())),
                         preferred_element_type=F32).astype(BF16)
    cq = _rms_norm(jnp.dot(m, wdq_ref[...], preferred_element_type=F32), qn_ref[...]).astype(BF16)
    qn = (jnp.dot(cq, wqn_ref[...], preferred_element_type=F32) * scale).astype(BF16)
    qr = ((jnp.dot(cq, wqr_ref[...], preferred_element_type=F32) * cq_ref[...]
           + jnp.dot(cq, wqrs_ref[...], preferred_element_type=F32) * sq_ref[...]) * scale).astype(BF16)
    for h in range(MLA_HEADS):
        k_ref[0, h, :, 0:MLA_NOPE] = kn[:, h * MLA_NOPE:(h + 1) * MLA_NOPE]
        k_ref[0, h, :, MLA_NOPE:] = kr
        v_ref[0, h, 0:MLA_V, :] = vt[h * MLA_V:(h + 1) * MLA_V, :]
        v_ref[0, h, MLA_V:, :] = (lax.broadcasted_iota(jnp.int32, (V_PAD_ROWS, vt.shape[1]), 0) == 0).astype(BF16)
        q_ref[0, h, :, 0:MLA_NOPE] = qn[:, h * MLA_NOPE:(h + 1) * MLA_NOPE]
        q_ref[0, h, :, MLA_NOPE:] = qr[:, h * MLA_ROPE:(h + 1) * MLA_ROPE]


def _swap_halves(w, width):
    r = w.reshape(w.shape[0], -1, 2, width // 2)
    return r[:, :, ::-1, :].reshape(w.shape)


def mla_project(x, mod_tiles, ctx_len, nbatch, w_dq, q_norm, w_uq, w_dkv, kv_norm, w_ukv):
    n, d = x.shape
    ltot = n // nbatch
    nlat = ltot - ctx_len
    tm = ROW_TILE
    tpb = ltot // tm
    hd = MLA_NOPE + MLA_ROPE
    kvl = kv_norm.shape[0]
    ql = q_norm.shape[0]
    pos = jnp.arange(nlat, dtype=jnp.int32)
    nf = MLA_ROPE // 4
    inv = ROPE_BASE ** (-jnp.arange(nf, dtype=F32) / nf)
    ang = jnp.concatenate([(pos // GRID_W).astype(F32)[:, None] * inv,
                           (pos % GRID_W).astype(F32)[:, None] * inv], axis=-1)
    cos = jnp.concatenate([jnp.ones((ctx_len, MLA_ROPE // 2), F32), jnp.cos(ang)], axis=0)
    sin = jnp.concatenate([jnp.zeros((ctx_len, MLA_ROPE // 2), F32), jnp.sin(ang)], axis=0)
    ck = jnp.concatenate([cos, cos], axis=1)
    sk = jnp.concatenate([-sin, sin], axis=1)
    cq = jnp.tile(ck, (1, MLA_HEADS))
    sq = jnp.tile(sk, (1, MLA_HEADS))
    w_dkv_c = w_dkv[:, :kvl].astype(BF16)
    w_dkv_r = w_dkv[:, kvl:]
    ukv = w_ukv.reshape(kvl, MLA_HEADS, MLA_NOPE + MLA_V)
    w_uk = ukv[:, :, :MLA_NOPE].reshape(kvl, -1).astype(BF16)
    w_uv = ukv[:, :, MLA_NOPE:].reshape(kvl, -1).T.astype(BF16)
    uq = w_uq.reshape(ql, MLA_HEADS, hd)
    w_qn = uq[:, :, :MLA_NOPE].reshape(ql, -1).astype(BF16)
    w_qr = uq[:, :, MLA_NOPE:].reshape(ql, -1)
    args = (x, mod_tiles, ck, sk, cq, sq, w_dkv_c, w_dkv_r.astype(BF16),
            _swap_halves(w_dkv_r, MLA_ROPE).astype(BF16), kv_norm.reshape(1, kvl), w_uk, w_uv,
            w_dq.astype(BF16), q_norm.reshape(1, ql), w_qn, w_qr.astype(BF16),
            _swap_halves(w_qr, MLA_ROPE).astype(BF16))
    pos_spec = lambda w: pl.BlockSpec((tm, w), lambda i: (i % tpb, 0))
    in_specs = [pl.BlockSpec((tm, d), lambda i: (i, 0)), pl.BlockSpec((1, N_MOD, d), lambda i: (i, 0, 0)),
                pos_spec(MLA_ROPE), pos_spec(MLA_ROPE), pos_spec(MLA_ROPE * MLA_HEADS),
                pos_spec(MLA_ROPE * MLA_HEADS)] + [_full(a.shape) for a in args[6:]]
    head_spec = lambda w: pl.BlockSpec((1, MLA_HEADS, tm, w), lambda i: (i // tpb, 0, i % tpb, 0))
    return pl.pallas_call(
        functools.partial(_mla_proj_kernel, scale=float(hd) ** -0.5 * math.log2(math.e)),
        grid=(n // tm,),
        in_specs=in_specs,
        out_specs=[head_spec(hd),
                   pl.BlockSpec((1, MLA_HEADS, MLA_V + V_PAD_ROWS, tm), lambda i: (i // tpb, 0, 0, i % tpb)),
                   head_spec(hd)],
        out_shape=[jax.ShapeDtypeStruct((nbatch, MLA_HEADS, ltot, hd), BF16),
                   jax.ShapeDtypeStruct((nbatch, MLA_HEADS, MLA_V + V_PAD_ROWS, ltot), BF16),
                   jax.ShapeDtypeStruct((nbatch, MLA_HEADS, ltot, hd), BF16)],
        compiler_params=_cparams("arbitrary"),
        name="mla_project",
    )(*args)


def _attn_kernel(q_ref, k_ref, vt_ref, o_ref, *, tk, nchunks):
    q = q_ref[0, 0]
    tq = q.shape[0]
    nt = (((1,), (1,)), ((), ()))

    def scores(c):
        return lax.dot_general(k_ref[0, 0, c * tk:(c + 1) * tk, :], q, nt, preferred_element_type=F32)

    m_i = jnp.full((1, tq), -jnp.inf, F32)
    acc = jnp.zeros((vt_ref.shape[2], tq), F32)
    s = scores(0)
    for c in range(nchunks):
        s_next = scores(c + 1) if c + 1 < nchunks else None
        m_new = jnp.maximum(m_i, jnp.max(s, axis=0, keepdims=True))
        p = jnp.exp2(s - m_new).astype(BF16)
        acc = jnp.exp2(m_i - m_new) * acc + jnp.dot(vt_ref[0, 0, :, c * tk:(c + 1) * tk], p,
                                                    preferred_element_type=F32)
        m_i, s = m_new, s_next
    o_ref[0] = (acc[:MLA_V] / acc[MLA_V:MLA_V + 1]).T


def mla_attend(q, k, vt, ctx_len):
    nb, nh, ltot, hd = k.shape
    tq = ROW_TILE
    assert ctx_len % tq == 0
    skip = ctx_len // tq
    nlat = ltot - ctx_len
    tk = ATT_KV_CHUNK
    assert ltot % tk == 0
    return pl.pallas_call(
        functools.partial(_attn_kernel, tk=tk, nchunks=ltot // tk),
        grid=(nb, nh, nlat // tq),
        in_specs=[pl.BlockSpec((1, 1, tq, hd), lambda b, h, i: (b, h, i + skip, 0)),
                  pl.BlockSpec((1, 1, ltot, hd), lambda b, h, i: (b, h, 0, 0)),
                  pl.BlockSpec((1, 1, MLA_V + V_PAD_ROWS, ltot), lambda b, h, i: (b, h, 0, 0))],
        out_specs=pl.BlockSpec((1, tq, MLA_V), lambda b, h, i: (b, i, h)),
        out_shape=jax.ShapeDtypeStruct((nb, nlat, nh * MLA_V), F32),
        compiler_params=_cparams("arbitrary", "arbitrary", "arbitrary"),
        name="mla_attend",
    )(q, k, vt)


def kernel(x, c, ctx, c_ctx, ada_w, ada_b, ln_g, ln_b, s5_a_re, s5_a_im, s5_log_dt, s5_b_re, s5_b_im, s5_c_re, s5_c_im, s5_d, s5_w_glu, s5_w_o, mla_w_dq, mla_q_norm, mla_w_uq, mla_w_dkv, mla_kv_norm, mla_w_ukv, mla_w_o, peer_w_q, peer_keys, peer_u, peer_v):
    nb, nlat, d = x.shape
    nctx = ctx.shape[1]
    ltot = nctx + nlat
    depth = ada_w.shape[0]
    alpha = (2 * depth) ** 0.25
    tm = ROW_TILE

    cvec = jnp.zeros((SUBLANES, d), F32).at[:nb].set(c).at[nb].set(c_ctx)
    tiles_per_batch = ltot // tm
    tile_rows_all = np.array([nb if j < nctx // tm else b for b in range(nb) for j in range(tiles_per_batch)])
    tile_rows_lat = np.array([b for b in range(nb) for _ in range(nlat // tm)])
    seg_rows = np.array([[nb] * nb, list(range(nb))])

    h_all = jnp.concatenate([ctx, x], axis=1).reshape(nb * ltot, d)

    mod = ada_mod(cvec, ada_w[0], ada_b[0]).reshape(SUBLANES, N_MOD, d)
    mod_all = mod[tile_rows_all]
    bfc, ccc, ar, ai = s5_prepare(s5_a_re[0], s5_a_im[0], s5_log_dt[0], s5_b_re[0], s5_b_im[0],
                                  s5_c_re[0], s5_c_im[0], nb)
    yf, yb = s5_scan(h_all.reshape(nb, ltot, d), mod[seg_rows], bfc, ccc, ar, ai, nctx)
    h1, t = post_mixer(h_all, (yf.reshape(-1, d), yb.reshape(-1, d)), mod_all, s5_w_o[0], ln_g[0, 0], ln_b[0, 0],
                       alpha, d_skip=s5_d[0], w_glu=s5_w_glu[0])
    f = peer_channel(t, peer_w_q[0], peer_keys[0], peer_u[0], peer_v[0])
    h_all = residual_ln(h1, f, mod_all, ln_g[0, 1], ln_b[0, 1], alpha)

    mod = ada_mod(cvec, ada_w[1], ada_b[1]).reshape(SUBLANES, N_MOD, d)
    k, v, q = mla_project(h_all, mod[tile_rows_all], nctx, nb, mla_w_dq[0], mla_q_norm[0], mla_w_uq[0],
                          mla_w_dkv[0], mla_kv_norm[0], mla_w_ukv[0])
    att = mla_attend(q, k, v, nctx).reshape(nb * nlat, -1)
    h_lat = h_all.reshape(nb, ltot, d)[:, nctx:].reshape(nb * nlat, d)
    mod_lat = mod[tile_rows_lat]
    h1, t = post_mixer(h_lat, (att,), mod_lat, mla_w_o[0], ln_g[1, 0], ln_b[1, 0], alpha)
    f = peer_channel(t, peer_w_q[1], peer_keys[1], peer_u[1], peer_v[1])
    out = residual_ln(h1, f, mod_lat, ln_g[1, 1], ln_b[1, 1], alpha)
    return out.reshape(nb, nlat, d).astype(x.dtype)
```

```python
import functools
import math

import jax
import jax.numpy as jnp
import numpy as np
from jax import lax
from jax.experimental import pallas as pl
from jax.experimental.pallas import tpu as pltpu

F32 = jnp.float32
BF16 = jnp.bfloat16
HIGHEST = lax.Precision.HIGHEST

LANES = 128
SUBLANES = 8
VMEM_LIMIT_BYTES = 56 * 1024 * 1024

N_MOD = 6
GRID_W = 64
S5_GROUP = 16
S5_STATE = 64
MLA_HEADS = 8
MLA_NOPE = 128
MLA_ROPE = 64
MLA_V = 128
ROPE_BASE = 10000.0
PEER_HEADS = 8
PEER_NKEYS = 128
PEER_DK = 128
PEER_TOPK = 16
LN_EPS = 1e-5
RMS_EPS = 1e-6

ROW_TILE = 256
S5_CHUNK = 128
PEER_TOK_TILE = 64
ATT_KV_CHUNK = 2816
V_PAD_ROWS = 16


def _cparams(*sem, flags=None):
    return pltpu.CompilerParams(dimension_semantics=sem, vmem_limit_bytes=VMEM_LIMIT_BYTES, flags=flags)


def _full(shape):
    n = len(shape)
    return pl.BlockSpec(shape, lambda *_: (0,) * n)


def _gelu(x):
    return 0.5 * x * (1.0 + lax.erf(x * (1.0 / math.sqrt(2.0))))


def _layer_norm(x, g, b):
    mu = jnp.mean(x, axis=-1, keepdims=True)
    xc = x - mu
    var = jnp.mean(xc * xc, axis=-1, keepdims=True)
    return xc * lax.rsqrt(var + LN_EPS) * g + b


def _rms_norm(x, g):
    return x * lax.rsqrt(jnp.mean(x * x, axis=-1, keepdims=True) + RMS_EPS) * g


def _bdot(a, b):
    return jnp.dot(a.astype(BF16), b, preferred_element_type=F32)


def _ada_kernel(c_ref, w_ref, b_ref, o_ref):
    c = c_ref[...]
    s = c * jax.nn.sigmoid(c)
    o_ref[...] = jnp.dot(s, w_ref[0], precision=HIGHEST, preferred_element_type=F32) + b_ref[0]


def ada_mod(cvec, ws, bs, layer):
    d = cvec.shape[1]
    depth, _, n = ws.shape
    tn = 1024
    return pl.pallas_call(
        _ada_kernel,
        grid=(n // tn,),
        in_specs=[_full((SUBLANES, d)), pl.BlockSpec((1, d, tn), lambda j: (layer, 0, j)),
                  pl.BlockSpec((1, 1, tn), lambda j: (layer, 0, j))],
        out_specs=pl.BlockSpec((SUBLANES, tn), lambda j: (0, j)),
        out_shape=jax.ShapeDtypeStruct((SUBLANES, n), F32),
        compiler_params=_cparams("arbitrary"),
        name="ada_mod",
    )(cvec, ws, bs.reshape(depth, 1, n))


def _s5_kernel(xf_ref, xb_ref, mod_ref, bf_ref, cc_ref, ar_ref, ai_ref, yf_ref, yb_ref, bu_ref, x_ref,
               *, chunk, nbatch):
    nlc = bf_ref.shape[1]
    spc = bf_ref.shape[3] // (2 * LANES)

    @pl.when(pl.program_id(0) == 0)
    def _():
        x_ref[...] = jnp.zeros_like(x_ref)

    for d, src in enumerate((xf_ref, xb_ref)):
        u = jnp.concatenate([(src[b] * (1.0 + mod_ref[0, b, 1:2, :]) + mod_ref[0, b, 0:1, :]).astype(BF16)
                             for b in range(nbatch)], axis=0)
        for j in range(nlc):
            res = jnp.dot(u[:, LANES * j:LANES * (j + 1)], bf_ref[d, j], preferred_element_type=F32)
            for b in range(nbatch):
                for part in range(2):
                    r = d * nbatch + b + 2 * nbatch * part
                    for k in range(spc):
                        col = (part * spc + k) * LANES
                        bu_ref[spc * j + k, pl.ds(r, chunk, stride=SUBLANES), :] = (
                            res[b * chunk:(b + 1) * chunk, col:col + LANES])

    ar = ar_ref[...]
    ai = ai_ref[...]
    row = lax.broadcasted_iota(jnp.int32, x_ref.shape, 1)
    is_bwd = (row % (2 * nbatch)) >= nbatch

    def step(s, x):
        off_f = pl.multiple_of(s * SUBLANES, SUBLANES)
        off_b = pl.multiple_of((chunk - 1 - s) * SUBLANES, SUBLANES)
        slab_f = bu_ref[:, pl.ds(off_f, SUBLANES), :]
        slab_b = bu_ref[:, pl.ds(off_b, SUBLANES), :]
        xn = ar * x + ai * pltpu.roll(x, SUBLANES // 2, axis=1) + jnp.where(is_bwd, slab_b, slab_f)
        bu_ref[:, pl.ds(off_f, SUBLANES), :] = jnp.where(is_bwd, slab_f, xn)
        bu_ref[:, pl.ds(off_b, SUBLANES), :] = jnp.where(is_bwd, xn, slab_b)
        return xn

    x_ref[...] = lax.fori_loop(0, chunk, step, x_ref[...])

    def state_rows(r, j):
        return jnp.concatenate([bu_ref[spc * j + k, pl.ds(r, chunk, stride=SUBLANES), :] for k in range(spc)],
                               axis=1).astype(BF16)

    for d, out in enumerate((yf_ref, yb_ref)):
        for j in range(nlc):
            h = jnp.concatenate(
                [jnp.concatenate([state_rows(d * nbatch + b, j), state_rows(d * nbatch + b + 2 * nbatch, j)], axis=1)
                 for b in range(nbatch)], axis=0)
            y = jnp.dot(h, cc_ref[d, j], preferred_element_type=F32)
            for b in range(nbatch):
                out[b, :, LANES * j:LANES * (j + 1)] = y[b * chunk:(b + 1) * chunk]


def s5_scan(x, mod_seg, bfc, ccc, ar, ai, ctx_len):
    nb, ltot, d = x.shape
    assert 2 * nb * 2 == SUBLANES, "state rows must fill one sublane tile"
    chunk = S5_CHUNK
    assert ltot % chunk == 0 and ctx_len % chunk == 0
    ctx_chunks = ctx_len // chunk
    nchunks = ltot // chunk
    nslab = ar.shape[0]
    fwd_spec = pl.BlockSpec((nb, chunk, d), lambda c: (0, c, 0))
    bwd_spec = pl.BlockSpec((nb, chunk, d), lambda c: (0, (nchunks - 1 - c + ctx_chunks) % nchunks, 0))
    return pl.pallas_call(
        functools.partial(_s5_kernel, chunk=chunk, nbatch=nb),
        grid=(nchunks,),
        in_specs=[fwd_spec, bwd_spec,
                  pl.BlockSpec((1, nb, N_MOD, d), lambda c: (jnp.minimum(c // ctx_chunks, 1), 0, 0, 0)),
                  _full(bfc.shape), _full(ccc.shape), _full(ar.shape), _full(ai.shape)],
        out_specs=[fwd_spec, bwd_spec],
        out_shape=[jax.ShapeDtypeStruct(x.shape, F32)] * 2,
        scratch_shapes=[pltpu.VMEM((nslab, SUBLANES * chunk, LANES), F32),
                        pltpu.VMEM((nslab, SUBLANES, LANES), F32)],
        compiler_params=_cparams("arbitrary"),
        name="s5_scan",
    )(x, x, mod_seg, bfc, ccc, ar, ai)


def s5_prepare(a_re, a_im, log_dt, b_re, b_im, c_re, c_im, nbatch):
    ndir, g, p = a_re.shape
    gc = b_re.shape[-1]
    gpc = LANES // gc
    nlc = g // gpc
    dt = jnp.exp(log_dt.astype(F32))[..., None]
    a_re = a_re.astype(F32)
    a_im = a_im.astype(F32)
    mag = jnp.exp(a_re * dt)
    ab_re = mag * jnp.cos(a_im * dt)
    ab_im = mag * jnp.sin(a_im * dt)
    nr, ni = ab_re - 1.0, ab_im
    den = a_re * a_re + a_im * a_im
    f_re = (nr * a_re + ni * a_im) / den
    f_im = (ni * a_re - nr * a_im) / den
    bf_re = f_re[..., None] * b_re - f_im[..., None] * b_im
    bf_im = f_re[..., None] * b_im + f_im[..., None] * b_re
    bf = jnp.stack([bf_re, bf_im], axis=1).reshape(ndir, 2, nlc, gpc, p, gc)
    eye = jnp.eye(gpc, dtype=F32)
    bfc = jnp.einsum('dqjgpc,gh->djgcqhp', bf, eye).reshape(ndir, nlc, gpc * gc, 2 * gpc * p).astype(BF16)
    cc = jnp.stack([c_re, -c_im], axis=1).astype(F32).reshape(ndir, 2, nlc, gpc, gc, p)
    ccc = jnp.einsum('dqjgcp,gh->djqgphc', cc, eye).reshape(ndir, nlc, 2 * gpc * p, gpc * gc).astype(BF16)
    rows_r, rows_i = [], []
    for part in range(2):
        for d in range(ndir):
            for _ in range(nbatch):
                rows_r.append(ab_re[d].reshape(-1))
                rows_i.append(ab_im[d].reshape(-1) * (-1.0 if part == 0 else 1.0))
    slabs = lambda rows: jnp.stack(rows).reshape(len(rows), -1, LANES).transpose(1, 0, 2)
    return bfc, ccc, slabs(rows_r), slabs(rows_i)


def _post_kernel(*refs, alpha, glu):
    if glu:
        h_ref, yf_ref, yb_ref, mod_ref, dsk_ref, wg_ref, wo_ref, lng_ref, lnb_ref, h1_ref, t_ref = refs
        h = h_ref[...]
        m = h * (1.0 + mod_ref[0, 1:2, :]) + mod_ref[0, 0:1, :]
        z = _gelu(yf_ref[...] + yb_ref[...] + dsk_ref[...] * m)
        z = z * jax.nn.sigmoid(_bdot(z, wg_ref[...]))
    else:
        h_ref, z_ref, mod_ref, wo_ref, lng_ref, lnb_ref, h1_ref, t_ref = refs
        h = h_ref[...]
        z = z_ref[...]
    o = _bdot(z, wo_ref[...])
    h1 = _layer_norm(alpha * h + mod_ref[0, 2:3, :] * o, lng_ref[...], lnb_ref[...])
    h1_ref[...] = h1
    t_ref[...] = h1 * (1.0 + mod_ref[0, 4:5, :]) + mod_ref[0, 3:4, :]


def post_mixer(h, pre, mod_tiles, w_o, ln_g, ln_b, alpha, d_skip=None, w_glu=None):
    n, d = h.shape
    tm = ROW_TILE
    glu = w_glu is not None
    row = pl.BlockSpec((tm, d), lambda i: (i, 0))
    vec = _full((1, d))
    modspec = pl.BlockSpec((1, N_MOD, d), lambda i: (i, 0, 0))
    if glu:
        args = (h, pre[0], pre[1], mod_tiles, d_skip.reshape(1, d), w_glu.astype(BF16), w_o.astype(BF16),
                ln_g.reshape(1, d), ln_b.reshape(1, d))
        in_specs = [row, row, row, modspec, vec, _full((d, d)), _full((d, d)), vec, vec]
    else:
        args = (h, pre[0], mod_tiles, w_o.astype(BF16), ln_g.reshape(1, d), ln_b.reshape(1, d))
        in_specs = [row, pl.BlockSpec((tm, pre[0].shape[1]), lambda i: (i, 0)), modspec,
                    _full(w_o.shape), vec, vec]
    return pl.pallas_call(
        functools.partial(_post_kernel, alpha=alpha, glu=glu),
        grid=(n // tm,),
        in_specs=in_specs,
        out_specs=[row, row],
        out_shape=[jax.ShapeDtypeStruct((n, d), F32)] * 2,
        compiler_params=_cparams("arbitrary"),
        name="post_mixer_glu" if glu else "post_mixer",
    )(*args)


def _res_ln_kernel(h_ref, f_ref, mod_ref, lng_ref, lnb_ref, o_ref, *, alpha):
    o_ref[...] = _layer_norm(alpha * h_ref[...] + mod_ref[0, 5:6, :] * f_ref[...], lng_ref[...], lnb_ref[...])


def residual_ln(h, f, mod_tiles, ln_g, ln_b, alpha):
    n, d = h.shape
    tm = ROW_TILE
    row = pl.BlockSpec((tm, d), lambda i: (i, 0))
    vec = _full((1, d))
    return pl.pallas_call(
        functools.partial(_res_ln_kernel, alpha=alpha),
        grid=(n // tm,),
        in_specs=[row, row, pl.BlockSpec((1, N_MOD, d), lambda i: (i, 0, 0)), vec, vec],
        out_specs=row,
        out_shape=jax.ShapeDtypeStruct((n, d), F32),
        compiler_params=_cparams("arbitrary"),
        name="residual_ln",
    )(h, f, mod_tiles, ln_g.reshape(1, d), ln_b.reshape(1, d))


def _topk_rows(s, k, payload=None):
    nrow = s.shape[0]
    iota = lax.broadcasted_iota(jnp.int32, s.shape, 0).astype(F32)
    vals, idxs, pays = [], [], []
    for _ in range(k):
        m = jnp.max(s, axis=0, keepdims=True)
        i = jnp.min(jnp.where(s == m, iota, float(nrow)), axis=0, keepdims=True)
        hit = iota == i
        vals.append(m)
        idxs.append(i)
        if payload is not None:
            pays.append(jnp.max(jnp.where(hit, payload, -1.0), axis=0, keepdims=True))
        s = jnp.where(hit, -jnp.inf, s)
    return vals, idxs, pays


def _route_kernel(t_ref, wqt_ref, keys_ref, e_ref, g_ref):
    nt = (((1,), (1,)), ((), ()))
    qt = lax.dot_general(wqt_ref[...], t_ref[...], nt, precision=HIGHEST, preferred_element_type=F32)
    half = PEER_DK // 2
    for h in range(PEER_HEADS):
        sv, si = [], []
        for s in range(2):
            row0 = (h * 2 + s) * half
            sc = jnp.dot(keys_ref[h * 2 + s], qt[row0:row0 + half, :], precision=HIGHEST,
                         preferred_element_type=F32)
            v, i, _ = _topk_rows(sc, PEER_TOPK)
            sv.append(v)
            si.append(i)
        v2 = jnp.concatenate(sv[1], axis=0)
        i2 = jnp.concatenate(si[1], axis=0)
        width = [PEER_TOPK // (a + 1) for a in range(PEER_TOPK)]
        npad = -sum(width) % SUBLANES
        tm = v2.shape[1]
        cand = jnp.concatenate([sv[0][a] + v2[:width[a]] for a in range(PEER_TOPK)]
                               + [jnp.full((npad, tm), -jnp.inf, F32)], axis=0)
        cexp = jnp.concatenate([si[0][a] * float(PEER_NKEYS) + i2[:width[a]] for a in range(PEER_TOPK)]
                               + [jnp.full((npad, tm), -1.0, F32)], axis=0)
        cv, _, ce = _topk_rows(cand, PEER_TOPK, payload=cexp)
        cvm = jnp.concatenate(cv, axis=0)
        ex = jnp.exp(cvm - cv[0])
        lo = h * PEER_TOPK
        g_ref[lo:lo + PEER_TOPK, :] = ex / jnp.sum(ex, axis=0, keepdims=True)
        e_ref[lo:lo + PEER_TOPK, :] = jnp.concatenate(ce, axis=0).astype(jnp.int32)


def peer_route(t, w_q, keys):
    n, d = t.shape
    tm = ROW_TILE
    hk = PEER_HEADS * PEER_TOPK
    keys2 = keys.reshape(PEER_HEADS * 2, PEER_NKEYS, PEER_DK // 2)
    out = pl.BlockSpec((hk, tm), lambda i: (0, i))
    return pl.pallas_call(
        _route_kernel,
        grid=(n // tm,),
        in_specs=[pl.BlockSpec((tm, d), lambda i: (i, 0)), _full((w_q.shape[1], d)), _full(keys2.shape)],
        out_specs=[out, out],
        out_shape=[jax.ShapeDtypeStruct((hk, n), jnp.int32), jax.ShapeDtypeStruct((hk, n), F32)],
        compiler_params=_cparams("arbitrary"),
        name="peer_route",
    )(t, w_q.T, keys2)


GATHER_STRIDE = PEER_HEADS * PEER_TOPK + SUBLANES


PEER_GROUP = 4
PEER_GATHER_PARTS = 16
PEER_ANCHOR_LAG = 12


def _split3(x):
    hi = x.astype(BF16)
    r1 = x - hi.astype(F32)
    mid = r1.astype(BF16)
    lo = (r1 - mid.astype(F32)).astype(BF16)
    return hi, mid, lo


def _gather_rows(idx_ref, tab_ref, g_ref, t, k0, k1, after=None):
    rows = tab_ref.shape[1]
    ids = idx_ref.at[t]
    for k in range(k0, k1):
        row = tab_ref[ids[k]]
        g_ref[pl.ds(k, rows, stride=GATHER_STRIDE), :] = row if after is None else row + after


def _zero_after(x):
    bits = pltpu.bitcast(x, jnp.uint32)
    return pltpu.bitcast(lax.shift_right_logical(bits, jnp.full_like(bits, 32)), jnp.int32)


def _chunk_matrix(g_ref, c, nk):
    return pltpu.bitcast(g_ref[GATHER_STRIDE * c:GATHER_STRIDE * c + nk, :], BF16)


def _token_pipeline(idx_ref, idx_next_ref, tab_ref, tm, nk, prepare, compute_chunk, finish, anchor, tiles):
    nchunk = tab_ref.shape[1]
    grp = PEER_GROUP
    per = nk // PEER_GATHER_PARTS
    set_a, set_b = tiles[:grp], tiles[grp:]
    ngroups = tm // grp
    assert ngroups % 2 == 0, "every token block must start on tile set A"

    @pl.when(pl.program_id(0) == 0)
    def _():
        for i in range(grp):
            _gather_rows(idx_ref, tab_ref, set_a[i], i, 0, nk)

    def group(t0, cur, nxt, ids_ref, t_ids):
        dots = [(i, c) for i in range(grp) for c in range(nchunk)]
        parts = [(i, j) for i in range(grp) for j in range(PEER_GATHER_PARTS)]
        acc = [prepare(t0 + i) for i in range(grp)]
        done = []
        for n in range(max(len(dots), len(parts))):
            if n < len(dots):
                i, c = dots[n]
                acc[i] = compute_chunk(t0 + i, c, _chunk_matrix(cur[i], c, nk), acc[i])
                done.append(anchor(acc[i]))
                if c == nchunk - 1:
                    finish(t0 + i, acc[i])
            if n < len(parts):
                i, j = parts[n]
                m = min(n - PEER_ANCHOR_LAG, len(dots) - 1)
                after = _zero_after(done[m]) if m >= 0 and (m < len(dots) - 1 or n == len(parts) - 1) else None
                _gather_rows(ids_ref, tab_ref, nxt[i], t_ids + i, j * per, (j + 1) * per, after)

    def step(g, carry):
        @pl.when(g % 2 == 0)
        def _():
            group(g * grp, set_a, set_b, idx_ref, (g + 1) * grp)

        @pl.when(g % 2 == 1)
        def _():
            group(g * grp, set_b, set_a, idx_ref, (g + 1) * grp)

        return carry

    lax.fori_loop(0, ngroups - 1, step, 0)
    group((ngroups - 1) * grp, set_b, set_a, idx_next_ref, 0)


def _peer_u_kernel(idx_ref, idx_next_ref, tab_ref, th_ref, gate_ref, mask_ref, pool_ref, o_ref, s_ref, *tiles,
                   tm, nk):
    nchunk = tab_ref.shape[1]
    nt = (((1,), (1,)), ((), ()))

    def prepare(t):
        x = th_ref[t]
        hi = x.astype(BF16).astype(F32)
        lhs = jnp.concatenate([hi, x - hi], axis=0).astype(BF16)
        return lhs, jnp.zeros((SUBLANES, 2 * nk), F32)

    def compute_chunk(t, c, w, acc):
        lhs, s = acc
        out = lax.dot_general(lhs, w, nt, preferred_element_type=F32)
        return lhs, s + (out[:SUBLANES] + out[SUBLANES:]) * mask_ref[c]

    def finish(t, acc):
        s_ref[pl.ds(t, 1), :] = jnp.sum(acc[1], axis=0, keepdims=True)

    _token_pipeline(idx_ref, idx_next_ref, tab_ref, tm, nk, prepare, compute_chunk, finish,
                    lambda acc: acc[1][:nchunk, :LANES], tiles)
    act = jnp.dot(s_ref[...], pool_ref[...], precision=HIGHEST, preferred_element_type=F32)
    o_ref[...] = _gelu(act) * gate_ref[...]


def _peer_v_kernel(idx_ref, idx_next_ref, tab_ref, a_ref, elo_ref, ehi_ref, o_ref, x_ref, *tiles, tm, nk):
    nchunk = tab_ref.shape[1]
    for j, p in enumerate(_split3(a_ref[...])):
        x_ref[j] = jnp.dot(p, elo_ref[...], preferred_element_type=F32)
        x_ref[3 + j] = jnp.dot(p, ehi_ref[...], preferred_element_type=F32)

    def prepare(t):
        return (jnp.concatenate([x_ref[r, pl.ds(t, 1), :] for r in range(6)]
                                + [jnp.zeros((SUBLANES - 6, 2 * nk), F32)], axis=0).astype(BF16), None)

    def compute_chunk(t, c, w, state):
        lhs = state[0]
        out = jnp.dot(lhs, w, preferred_element_type=F32)
        o_ref[t, pl.ds(c, 1), :] = out[0:1] + out[1:2] + out[2:3]
        o_ref[t, pl.ds(nchunk + c, 1), :] = out[3:4] + out[4:5] + out[5:6]
        return lhs, out

    _token_pipeline(idx_ref, idx_next_ref, tab_ref, tm, nk, prepare, compute_chunk, lambda t, state: None,
                    lambda state: state[1][:nchunk], tiles)


def _pack_kernel(x_ref, o_ref):
    half = x_ref.shape[2] // 2

    def bf16_bits(x):
        return pltpu.bitcast(x.astype(BF16).astype(F32), jnp.uint32)

    lo = lax.shift_right_logical(bf16_bits(x_ref[0, :, :half]), jnp.uint32(16))
    hi = bf16_bits(x_ref[0, :, half:]) & jnp.uint32(0xFFFF0000)
    o_ref[...] = pltpu.bitcast(lo | hi, jnp.int32)


def _pack_table(tabs, layer):
    _, e, d = tabs.shape
    half = d // 2
    te = 512
    w = pl.pallas_call(
        _pack_kernel,
        grid=(e // te,),
        in_specs=[pl.BlockSpec((1, te, d), lambda i: (layer, i, 0))],
        out_specs=pl.BlockSpec((te, half), lambda i: (i, 0)),
        out_shape=jax.ShapeDtypeStruct((e, half), jnp.int32),
        compiler_params=_cparams("arbitrary"),
        name="pack_table",
    )(tabs)
    return w.reshape(e, half // LANES, LANES)


def peer_experts(idx, gate, th, u_tabs, v_tabs, layer):
    n, nk = idx.shape
    tm = PEER_TOK_TILE
    nchunk = th.shape[1] // 2
    up, vp = _pack_table(u_tabs, layer), _pack_table(v_tabs, layer)
    rho = np.arange(2 * nk)
    mask = np.zeros((nchunk, SUBLANES, 2 * nk), np.float32)
    for c in range(nchunk):
        mask[c, c, rho % 2 == 0] = 1.0
        mask[c, nchunk + c, rho % 2 == 1] = 1.0
    pool = (rho[:, None] // 2 == np.arange(nk)[None, :]).astype(np.float32)
    elo = (np.arange(nk)[:, None] * 2 == rho[None, :]).astype(np.float32)
    ehi = (np.arange(nk)[:, None] * 2 + 1 == rho[None, :]).astype(np.float32)
    smem = pl.BlockSpec((tm, nk), lambda i: (i, 0), memory_space=pltpu.SMEM)
    assert PEER_GROUP <= SUBLANES
    smem_next = pl.BlockSpec((SUBLANES, nk), lambda i: (jnp.minimum((i + 1) * (tm // SUBLANES), n // SUBLANES - 1), 0),
                             memory_space=pltpu.SMEM)
    tab_spec = pl.BlockSpec(up.shape, lambda i: (0, 0, 0), pipeline_mode=pl.Buffered(1))
    row = pl.BlockSpec((tm, nk), lambda i: (i, 0))
    th_spec = pl.BlockSpec((tm,) + th.shape[1:], lambda i: (i, 0, 0))
    gather_tiles = [pltpu.VMEM((GATHER_STRIDE * nchunk, LANES), jnp.int32)] * (2 * PEER_GROUP)
    act = pl.pallas_call(
        functools.partial(_peer_u_kernel, tm=tm, nk=nk),
        grid=(n // tm,),
        in_specs=[smem, smem_next, tab_spec, th_spec, row, _full(mask.shape), _full(pool.shape)],
        out_specs=row,
        out_shape=jax.ShapeDtypeStruct((n, nk), F32),
        scratch_shapes=[pltpu.VMEM((tm, 2 * nk), F32)] + gather_tiles,
        compiler_params=_cparams("arbitrary"),
        name="peer_u",
    )(idx, idx, up, th, gate, jnp.asarray(mask), jnp.asarray(pool))
    return pl.pallas_call(
        functools.partial(_peer_v_kernel, tm=tm, nk=nk),
        grid=(n // tm,),
        in_specs=[smem, smem_next, tab_spec, row, _full(elo.shape), _full(ehi.shape)],
        out_specs=th_spec,
        out_shape=jax.ShapeDtypeStruct(th.shape, F32),
        scratch_shapes=[pltpu.VMEM((6, tm, 2 * nk), F32)] + gather_tiles,
        compiler_params=_cparams("arbitrary"),
        name="peer_v",
    )(idx, idx, vp, act, jnp.asarray(elo, BF16), jnp.asarray(ehi, BF16))


def peer_channel(t, w_q, keys, u_tabs, v_tabs, layer):
    n, d = t.shape
    e, g = peer_route(t, w_q, keys)
    f = peer_experts(e.T, g.T, t.reshape(n, d // LANES, LANES), u_tabs, v_tabs, layer)
    return f.reshape(n, d)


def _mla_proj_kernel(h_ref, mod_ref, ck_ref, sk_ref, cq_ref, sq_ref, wdc_ref, wdr_ref, wdrs_ref, kvn_ref,
                     wuk_ref, wuv_ref, wdq_ref, qn_ref, wqn_ref, wqr_ref, wqrs_ref, k_ref, v_ref, q_ref, *, scale):
    m = (h_ref[...] * (1.0 + mod_ref[0, 1:2, :]) + mod_ref[0, 0:1, :]).astype(BF16)
    ckv = _rms_norm(jnp.dot(m, wdc_ref[...], preferred_element_type=F32), kvn_ref[...]).astype(BF16)
    kr = (jnp.dot(m, wdr_ref[...], preferred_element_type=F32) * ck_ref[...]
          + jnp.dot(m, wdrs_ref[...], preferred_element_type=F32) * sk_ref[...]).astype(BF16)
    kn = jnp.dot(ckv, wuk_ref[...], preferred_element_type=F32).astype(BF16)
    vt = lax.dot_general(wuv_ref[...], ckv, (((1,), (1,)), ((), ())),
                         preferred_element_type=F32).astype(BF16)
    cq = _rms_norm(jnp.dot(m, wdq_ref[...], preferred_element_type=F32), qn_ref[...]).astype(BF16)
    qn = (jnp.dot(cq, wqn_ref[...], preferred_element_type=F32) * scale).astype(BF16)
    qr = ((jnp.dot(cq, wqr_ref[...], preferred_element_type=F32) * cq_ref[...]
           + jnp.dot(cq, wqrs_ref[...], preferred_element_type=F32) * sq_ref[...]) * scale).astype(BF16)
    for h in range(MLA_HEADS):
        k_ref[0, h, :, 0:MLA_NOPE] = kn[:, h * MLA_NOPE:(h + 1) * MLA_NOPE]
        k_ref[0, h, :, MLA_NOPE:] = kr
        v_ref[0, h, 0:MLA_V, :] = vt[h * MLA_V:(h + 1) * MLA_V, :]
        v_ref[0, h, MLA_V:, :] = (lax.broadcasted_iota(jnp.int32, (V_PAD_ROWS, vt.shape[1]), 0) == 0).astype(BF16)
        q_ref[0, h, :, 0:MLA_NOPE] = qn[:, h * MLA_NOPE:(h + 1) * MLA_NOPE]
        q_ref[0, h, :, MLA_NOPE:] = qr[:, h * MLA_ROPE:(h + 1) * MLA_ROPE]


def _swap_halves(w, width):
    r = w.reshape(w.shape[0], -1, 2, width // 2)
    return r[:, :, ::-1, :].reshape(w.shape)


def mla_project(x, mod_tiles, ctx_len, nbatch, w_dq, q_norm, w_uq, w_dkv, kv_norm, w_ukv):
    n, d = x.shape
    ltot = n // nbatch
    nlat = ltot - ctx_len
    tm = ROW_TILE
    tpb = ltot // tm
    hd = MLA_NOPE + MLA_ROPE
    kvl = kv_norm.shape[0]
    ql = q_norm.shape[0]
    pos = jnp.arange(nlat, dtype=jnp.int32)
    nf = MLA_ROPE // 4
    inv = ROPE_BASE ** (-jnp.arange(nf, dtype=F32) / nf)
    ang = jnp.concatenate([(pos // GRID_W).astype(F32)[:, None] * inv,
                           (pos % GRID_W).astype(F32)[:, None] * inv], axis=-1)
    cos = jnp.concatenate([jnp.ones((ctx_len, MLA_ROPE // 2), F32), jnp.cos(ang)], axis=0)
    sin = jnp.concatenate([jnp.zeros((ctx_len, MLA_ROPE // 2), F32), jnp.sin(ang)], axis=0)
    ck = jnp.concatenate([cos, cos], axis=1)
    sk = jnp.concatenate([-sin, sin], axis=1)
    cq = jnp.tile(ck, (1, MLA_HEADS))
    sq = jnp.tile(sk, (1, MLA_HEADS))
    w_dkv_c = w_dkv[:, :kvl].astype(BF16)
    w_dkv_r = w_dkv[:, kvl:]
    ukv = w_ukv.reshape(kvl, MLA_HEADS, MLA_NOPE + MLA_V)
    w_uk = ukv[:, :, :MLA_NOPE].reshape(kvl, -1).astype(BF16)
    w_uv = ukv[:, :, MLA_NOPE:].reshape(kvl, -1).T.astype(BF16)
    uq = w_uq.reshape(ql, MLA_HEADS, hd)
    w_qn = uq[:, :, :MLA_NOPE].reshape(ql, -1).astype(BF16)
    w_qr = uq[:, :, MLA_NOPE:].reshape(ql, -1)
    args = (x, mod_tiles, ck, sk, cq, sq, w_dkv_c, w_dkv_r.astype(BF16),
            _swap_halves(w_dkv_r, MLA_ROPE).astype(BF16), kv_norm.reshape(1, kvl), w_uk, w_uv,
            w_dq.astype(BF16), q_norm.reshape(1, ql), w_qn, w_qr.astype(BF16),
            _swap_halves(w_qr, MLA_ROPE).astype(BF16))
    pos_spec = lambda w: pl.BlockSpec((tm, w), lambda i: (i % tpb, 0))
    in_specs = [pl.BlockSpec((tm, d), lambda i: (i, 0)), pl.BlockSpec((1, N_MOD, d), lambda i: (i, 0, 0)),
                pos_spec(MLA_ROPE), pos_spec(MLA_ROPE), pos_spec(MLA_ROPE * MLA_HEADS),
                pos_spec(MLA_ROPE * MLA_HEADS)] + [_full(a.shape) for a in args[6:]]
    head_spec = lambda w: pl.BlockSpec((1, MLA_HEADS, tm, w), lambda i: (i // tpb, 0, i % tpb, 0))
    return pl.pallas_call(
        functools.partial(_mla_proj_kernel, scale=float(hd) ** -0.5 * math.log2(math.e)),
        grid=(n // tm,),
        in_specs=in_specs,
        out_specs=[head_spec(hd),
                   pl.BlockSpec((1, MLA_HEADS, MLA_V + V_PAD_ROWS, tm), lambda i: (i // tpb, 0, 0, i % tpb)),
                   head_spec(hd)],
        out_shape=[jax.ShapeDtypeStruct((nbatch, MLA_HEADS, ltot, hd), BF16),
                   jax.ShapeDtypeStruct((nbatch, MLA_HEADS, MLA_V + V_PAD_ROWS, ltot), BF16),
                   jax.ShapeDtypeStruct((nbatch, MLA_HEADS, ltot, hd), BF16)],
        compiler_params=_cparams("arbitrary"),
        name="mla_project",
    )(*args)


def _attn_kernel(q_ref, k_ref, vt_ref, o_ref, *, tk, nchunks):
    q = q_ref[0, 0]
    tq = q.shape[0]
    nt = (((1,), (1,)), ((), ()))

    def scores(c):
        return lax.dot_general(k_ref[0, 0, c * tk:(c + 1) * tk, :], q, nt, preferred_element_type=F32)

    m_i = jnp.full((1, tq), -jnp.inf, F32)
    acc = jnp.zeros((vt_ref.shape[2], tq), F32)
    s = scores(0)
    for c in range(nchunks):
        s_next = scores(c + 1) if c + 1 < nchunks else None
        m_new = jnp.maximum(m_i, jnp.max(s, axis=0, keepdims=True))
        p = jnp.exp2(s - m_new).astype(BF16)
        acc = jnp.exp2(m_i - m_new) * acc + jnp.dot(vt_ref[0, 0, :, c * tk:(c + 1) * tk], p,
                                                    preferred_element_type=F32)
        m_i, s = m_new, s_next
    o_ref[0] = (acc[:MLA_V] / acc[MLA_V:MLA_V + 1]).T


def mla_attend(q, k, vt, ctx_len):
    nb, nh, ltot, hd = k.shape
    tq = ROW_TILE
    assert ctx_len % tq == 0
    skip = ctx_len // tq
    nlat = ltot - ctx_len
    tk = ATT_KV_CHUNK
    assert ltot % tk == 0
    return pl.pallas_call(
        functools.partial(_attn_kernel, tk=tk, nchunks=ltot // tk),
        grid=(nb, nh, nlat // tq),
        in_specs=[pl.BlockSpec((1, 1, tq, hd), lambda b, h, i: (b, h, i + skip, 0)),
                  pl.BlockSpec((1, 1, ltot, hd), lambda b, h, i: (b, h, 0, 0)),
                  pl.BlockSpec((1, 1, MLA_V + V_PAD_ROWS, ltot), lambda b, h, i: (b, h, 0, 0))],
        out_specs=pl.BlockSpec((1, tq, MLA_V), lambda b, h, i: (b, i, h)),
        out_shape=jax.ShapeDtypeStruct((nb, nlat, nh * MLA_V), F32),
        compiler_params=_cparams("arbitrary", "arbitrary", "arbitrary"),
        name="mla_attend",
    )(q, k, vt)


def kernel(x, c, ctx, c_ctx, ada_w, ada_b, ln_g, ln_b, s5_a_re, s5_a_im, s5_log_dt, s5_b_re, s5_b_im, s5_c_re, s5_c_im, s5_d, s5_w_glu, s5_w_o, mla_w_dq, mla_q_norm, mla_w_uq, mla_w_dkv, mla_kv_norm, mla_w_ukv, mla_w_o, peer_w_q, peer_keys, peer_u, peer_v):
    nb, nlat, d = x.shape
    nctx = ctx.shape[1]
    ltot = nctx + nlat
    depth = ada_w.shape[0]
    alpha = (2 * depth) ** 0.25
    tm = ROW_TILE

    cvec = jnp.zeros((SUBLANES, d), F32).at[:nb].set(c).at[nb].set(c_ctx)
    tiles_per_batch = ltot // tm
    tile_rows_all = np.array([nb if j < nctx // tm else b for b in range(nb) for j in range(tiles_per_batch)])
    tile_rows_lat = np.array([b for b in range(nb) for _ in range(nlat // tm)])
    seg_rows = np.array([[nb] * nb, list(range(nb))])

    h_all = jnp.concatenate([ctx, x], axis=1).reshape(nb * ltot, d)

    mod = ada_mod(cvec, ada_w, ada_b, 0).reshape(SUBLANES, N_MOD, d)
    mod_all = mod[tile_rows_all]
    bfc, ccc, ar, ai = s5_prepare(s5_a_re[0], s5_a_im[0], s5_log_dt[0], s5_b_re[0], s5_b_im[0],
                                  s5_c_re[0], s5_c_im[0], nb)
    yf, yb = s5_scan(h_all.reshape(nb, ltot, d), mod[seg_rows], bfc, ccc, ar, ai, nctx)
    h1, t = post_mixer(h_all, (yf.reshape(-1, d), yb.reshape(-1, d)), mod_all, s5_w_o[0], ln_g[0, 0], ln_b[0, 0],
                       alpha, d_skip=s5_d[0], w_glu=s5_w_glu[0])
    f = peer_channel(t, peer_w_q[0], peer_keys[0], peer_u, peer_v, 0)
    h_all = residual_ln(h1, f, mod_all, ln_g[0, 1], ln_b[0, 1], alpha)

    mod = ada_mod(cvec, ada_w, ada_b, 1).reshape(SUBLANES, N_MOD, d)
    k, v, q = mla_project(h_all, mod[tile_rows_all], nctx, nb, mla_w_dq[0], mla_q_norm[0], mla_w_uq[0],
                          mla_w_dkv[0], mla_kv_norm[0], mla_w_ukv[0])
    att = mla_attend(q, k, v, nctx).reshape(nb * nlat, -1)
    h_lat = h_all.reshape(nb, ltot, d)[:, nctx:].reshape(nb * nlat, d)
    mod_lat = mod[tile_rows_lat]
    h1, t = post_mixer(h_lat, (att,), mod_lat, mla_w_o[0], ln_g[1, 0], ln_b[1, 0], alpha)
    f = peer_channel(t, peer_w_q[1], peer_keys[1], peer_u, peer_v, 1)
    out = residual_ln(h1, f, mod_lat, ln_g[1, 1], ln_b[1, 1], alpha)
    return out.reshape(nb, nlat, d).astype(x.dtype)
```

```python
import functools
import math

import jax
import jax.numpy as jnp
import numpy as np
from jax import lax
from jax.experimental import pallas as pl
from jax.experimental.pallas import tpu as pltpu

F32 = jnp.float32
BF16 = jnp.bfloat16
HIGHEST = lax.Precision.HIGHEST

LANES = 128
SUBLANES = 8
VMEM_LIMIT_BYTES = 56 * 1024 * 1024

N_MOD = 6
GRID_W = 64
S5_GROUP = 16
S5_STATE = 64
MLA_HEADS = 8
MLA_NOPE = 128
MLA_ROPE = 64
MLA_V = 128
ROPE_BASE = 10000.0
PEER_HEADS = 8
PEER_NKEYS = 128
PEER_DK = 128
PEER_TOPK = 16
LN_EPS = 1e-5
RMS_EPS = 1e-6

ROW_TILE = 256
S5_CHUNK = 128
S5_SLAB_GROUPS = 2
S5_STEP_UNROLL = 8
PEER_TOK_TILE = 64
ATT_KV_CHUNK = 2816
V_PAD_ROWS = 16


def _cparams(*sem):
    return pltpu.CompilerParams(dimension_semantics=sem, vmem_limit_bytes=VMEM_LIMIT_BYTES)


def _full(shape):
    n = len(shape)
    return pl.BlockSpec(shape, lambda *_: (0,) * n)


def _gelu(x):
    return 0.5 * x * (1.0 + lax.erf(x * (1.0 / math.sqrt(2.0))))


def _layer_norm(x, g, b):
    mu = jnp.mean(x, axis=-1, keepdims=True)
    xc = x - mu
    var = jnp.mean(xc * xc, axis=-1, keepdims=True)
    return xc * lax.rsqrt(var + LN_EPS) * g + b


def _rms_norm(x, g):
    return x * lax.rsqrt(jnp.mean(x * x, axis=-1, keepdims=True) + RMS_EPS) * g


def _bdot(a, b):
    return jnp.dot(a.astype(BF16), b, preferred_element_type=F32)


def _ada_kernel(c_ref, w_ref, b_ref, o_ref):
    c = c_ref[...]
    s = c * jax.nn.sigmoid(c)
    o_ref[...] = jnp.dot(s, w_ref[0], precision=HIGHEST, preferred_element_type=F32) + b_ref[0]


def ada_mod(cvec, ws, bs, layer):
    d = cvec.shape[1]
    depth, _, n = ws.shape
    tn = 1024
    return pl.pallas_call(
        _ada_kernel,
        grid=(n // tn,),
        in_specs=[_full((SUBLANES, d)), pl.BlockSpec((1, d, tn), lambda j: (layer, 0, j)),
                  pl.BlockSpec((1, 1, tn), lambda j: (layer, 0, j))],
        out_specs=pl.BlockSpec((SUBLANES, tn), lambda j: (0, j)),
        out_shape=jax.ShapeDtypeStruct((SUBLANES, n), F32),
        compiler_params=_cparams("arbitrary"),
        name="ada_mod",
    )(cvec, ws, bs.reshape(depth, 1, n))


def _s5_kernel(xf_ref, xb_ref, mod_ref, bf_ref, cc_ref, ar_ref, ai_ref, yf_ref, yb_ref, bu_ref, x_ref,
               *, chunk, nbatch):
    nlc = bf_ref.shape[1]
    spc = bf_ref.shape[3] // (2 * LANES)

    @pl.when(pl.program_id(0) == 0)
    def _():
        x_ref[...] = jnp.zeros_like(x_ref)

    for d, src in enumerate((xf_ref, xb_ref)):
        u = jnp.concatenate([(src[b] * (1.0 + mod_ref[0, b, 1:2, :]) + mod_ref[0, b, 0:1, :]).astype(BF16)
                             for b in range(nbatch)], axis=0)
        for j in range(nlc):
            res = jnp.dot(u[:, LANES * j:LANES * (j + 1)], bf_ref[d, j], preferred_element_type=F32)
            for b in range(nbatch):
                for part in range(2):
                    r = d * nbatch + b + 2 * nbatch * part
                    for k in range(spc):
                        col = (part * spc + k) * LANES
                        bu_ref[spc * j + k, pl.ds(r, chunk, stride=SUBLANES), :] = (
                            res[b * chunk:(b + 1) * chunk, col:col + LANES])

    nslab = x_ref.shape[0]
    gs = nslab // S5_SLAB_GROUPS
    row = lax.broadcasted_iota(jnp.int32, (gs,) + x_ref.shape[1:], 1)
    is_bwd = (row % (2 * nbatch)) >= nbatch
    for g0 in range(0, nslab, gs):
        ar = ar_ref[g0:g0 + gs]
        ai = ai_ref[g0:g0 + gs]

        def step(s, x, g0=g0, ar=ar, ai=ai):
            off_f = pl.multiple_of(s * SUBLANES, SUBLANES)
            off_b = pl.multiple_of((chunk - 1 - s) * SUBLANES, SUBLANES)
            slab_f = bu_ref[g0:g0 + gs, pl.ds(off_f, SUBLANES), :]
            slab_b = bu_ref[g0:g0 + gs, pl.ds(off_b, SUBLANES), :]
            xn = ar * x + ai * pltpu.roll(x, SUBLANES // 2, axis=1) + jnp.where(is_bwd, slab_b, slab_f)
            bu_ref[g0:g0 + gs, pl.ds(off_f, SUBLANES), :] = jnp.where(is_bwd, slab_f, xn)
            bu_ref[g0:g0 + gs, pl.ds(off_b, SUBLANES), :] = jnp.where(is_bwd, xn, slab_b)
            return xn

        def steps(i, x, step=step):
            for j in range(S5_STEP_UNROLL):
                x = step(i * S5_STEP_UNROLL + j, x)
            return x

        x_ref[g0:g0 + gs] = lax.fori_loop(0, chunk // S5_STEP_UNROLL, steps, x_ref[g0:g0 + gs])

    def state_rows(r, j):
        return jnp.concatenate([bu_ref[spc * j + k, pl.ds(r, chunk, stride=SUBLANES), :] for k in range(spc)],
                               axis=1).astype(BF16)

    for d, out in enumerate((yf_ref, yb_ref)):
        for j in range(nlc):
            h = jnp.concatenate(
                [jnp.concatenate([state_rows(d * nbatch + b, j), state_rows(d * nbatch + b + 2 * nbatch, j)], axis=1)
                 for b in range(nbatch)], axis=0)
            y = jnp.dot(h, cc_ref[d, j], preferred_element_type=F32)
            for b in range(nbatch):
                out[b, :, LANES * j:LANES * (j + 1)] = y[b * chunk:(b + 1) * chunk]


def s5_scan(x, mod_seg, bfc, ccc, ar, ai, ctx_len):
    nb, ltot, d = x.shape
    assert 2 * nb * 2 == SUBLANES, "state rows must fill one sublane tile"
    chunk = S5_CHUNK
    assert ltot % chunk == 0 and ctx_len % chunk == 0
    ctx_chunks = ctx_len // chunk
    nchunks = ltot // chunk
    nslab = ar.shape[0]
    fwd_spec = pl.BlockSpec((nb, chunk, d), lambda c: (0, c, 0))
    bwd_spec = pl.BlockSpec((nb, chunk, d), lambda c: (0, (nchunks - 1 - c + ctx_chunks) % nchunks, 0))
    return pl.pallas_call(
        functools.partial(_s5_kernel, chunk=chunk, nbatch=nb),
        grid=(nchunks,),
        in_specs=[fwd_spec, bwd_spec,
                  pl.BlockSpec((1, nb, N_MOD, d), lambda c: (jnp.minimum(c // ctx_chunks, 1), 0, 0, 0)),
                  _full(bfc.shape), _full(ccc.shape), _full(ar.shape), _full(ai.shape)],
        out_specs=[fwd_spec, bwd_spec],
        out_shape=[jax.ShapeDtypeStruct(x.shape, F32)] * 2,
        scratch_shapes=[pltpu.VMEM((nslab, SUBLANES * chunk, LANES), F32),
                        pltpu.VMEM((nslab, SUBLANES, LANES), F32)],
        compiler_params=_cparams("arbitrary"),
        name="s5_scan",
    )(x, x, mod_seg, bfc, ccc, ar, ai)


def s5_prepare(a_re, a_im, log_dt, b_re, b_im, c_re, c_im, nbatch):
    ndir, g, p = a_re.shape
    gc = b_re.shape[-1]
    gpc = LANES // gc
    nlc = g // gpc
    dt = jnp.exp(log_dt.astype(F32))[..., None]
    a_re = a_re.astype(F32)
    a_im = a_im.astype(F32)
    mag = jnp.exp(a_re * dt)
    ab_re = mag * jnp.cos(a_im * dt)
    ab_im = mag * jnp.sin(a_im * dt)
    nr, ni = ab_re - 1.0, ab_im
    den = a_re * a_re + a_im * a_im
    f_re = (nr * a_re + ni * a_im) / den
    f_im = (ni * a_re - nr * a_im) / den
    bf_re = f_re[..., None] * b_re - f_im[..., None] * b_im
    bf_im = f_re[..., None] * b_im + f_im[..., None] * b_re
    bf = jnp.stack([bf_re, bf_im], axis=1).reshape(ndir, 2, nlc, gpc, p, gc)
    eye = jnp.eye(gpc, dtype=F32)
    bfc = jnp.einsum('dqjgpc,gh->djgcqhp', bf, eye).reshape(ndir, nlc, gpc * gc, 2 * gpc * p).astype(BF16)
    cc = jnp.stack([c_re, -c_im], axis=1).astype(F32).reshape(ndir, 2, nlc, gpc, gc, p)
    ccc = jnp.einsum('dqjgcp,gh->djqgphc', cc, eye).reshape(ndir, nlc, 2 * gpc * p, gpc * gc).astype(BF16)
    rows_r, rows_i = [], []
    for part in range(2):
        for d in range(ndir):
            for _ in range(nbatch):
                rows_r.append(ab_re[d].reshape(-1))
                rows_i.append(ab_im[d].reshape(-1) * (-1.0 if part == 0 else 1.0))
    slabs = lambda rows: jnp.stack(rows).reshape(len(rows), -1, LANES).transpose(1, 0, 2)
    return bfc, ccc, slabs(rows_r), slabs(rows_i)


def _post_kernel(*refs, alpha, glu):
    if glu:
        h_ref, yf_ref, yb_ref, mod_ref, dsk_ref, wg_ref, wo_ref, lng_ref, lnb_ref, h1_ref, t_ref = refs
        h = h_ref[...]
        m = h * (1.0 + mod_ref[0, 1:2, :]) + mod_ref[0, 0:1, :]
        z = _gelu(yf_ref[...] + yb_ref[...] + dsk_ref[...] * m)
        z = z * jax.nn.sigmoid(_bdot(z, wg_ref[...]))
    else:
        h_ref, z_ref, mod_ref, wo_ref, lng_ref, lnb_ref, h1_ref, t_ref = refs
        h = h_ref[...]
        z = z_ref[...]
    o = _bdot(z, wo_ref[...])
    h1 = _layer_norm(alpha * h + mod_ref[0, 2:3, :] * o, lng_ref[...], lnb_ref[...])
    h1_ref[...] = h1
    t_ref[...] = h1 * (1.0 + mod_ref[0, 4:5, :]) + mod_ref[0, 3:4, :]


def post_mixer(h, pre, mod_tiles, w_o, ln_g, ln_b, alpha, d_skip=None, w_glu=None):
    n, d = h.shape
    tm = ROW_TILE
    glu = w_glu is not None
    row = pl.BlockSpec((tm, d), lambda i: (i, 0))
    vec = _full((1, d))
    modspec = pl.BlockSpec((1, N_MOD, d), lambda i: (i, 0, 0))
    if glu:
        args = (h, pre[0], pre[1], mod_tiles, d_skip.reshape(1, d), w_glu.astype(BF16), w_o.astype(BF16),
                ln_g.reshape(1, d), ln_b.reshape(1, d))
        in_specs = [row, row, row, modspec, vec, _full((d, d)), _full((d, d)), vec, vec]
    else:
        args = (h, pre[0], mod_tiles, w_o.astype(BF16), ln_g.reshape(1, d), ln_b.reshape(1, d))
        in_specs = [row, pl.BlockSpec((tm, pre[0].shape[1]), lambda i: (i, 0)), modspec,
                    _full(w_o.shape), vec, vec]
    return pl.pallas_call(
        functools.partial(_post_kernel, alpha=alpha, glu=glu),
        grid=(n // tm,),
        in_specs=in_specs,
        out_specs=[row, row],
        out_shape=[jax.ShapeDtypeStruct((n, d), F32)] * 2,
        compiler_params=_cparams("arbitrary"),
        name="post_mixer_glu" if glu else "post_mixer",
    )(*args)


def _res_ln_kernel(h_ref, f_ref, mod_ref, lng_ref, lnb_ref, o_ref, *, alpha):
    o_ref[...] = _layer_norm(alpha * h_ref[...] + mod_ref[0, 5:6, :] * f_ref[...], lng_ref[...], lnb_ref[...])


def residual_ln(h, f, mod_tiles, ln_g, ln_b, alpha):
    n, d = h.shape
    tm = ROW_TILE
    row = pl.BlockSpec((tm, d), lambda i: (i, 0))
    vec = _full((1, d))
    return pl.pallas_call(
        functools.partial(_res_ln_kernel, alpha=alpha),
        grid=(n // tm,),
        in_specs=[row, row, pl.BlockSpec((1, N_MOD, d), lambda i: (i, 0, 0)), vec, vec],
        out_specs=row,
        out_shape=jax.ShapeDtypeStruct((n, d), F32),
        compiler_params=_cparams("arbitrary"),
        name="residual_ln",
    )(h, f, mod_tiles, ln_g.reshape(1, d), ln_b.reshape(1, d))


def _topk_rows(s, k, payload=None):
    nrow = s.shape[0]
    iota = lax.broadcasted_iota(jnp.int32, s.shape, 0).astype(F32)
    vals, idxs, pays = [], [], []
    for _ in range(k):
        m = jnp.max(s, axis=0, keepdims=True)
        i = jnp.min(jnp.where(s == m, iota, float(nrow)), axis=0, keepdims=True)
        hit = iota == i
        vals.append(m)
        idxs.append(i)
        if payload is not None:
            pays.append(jnp.max(jnp.where(hit, payload, -1.0), axis=0, keepdims=True))
        s = jnp.where(hit, -jnp.inf, s)
    return vals, idxs, pays


def _route_kernel(t_ref, wqt_ref, keys_ref, e_ref, g_ref):
    nt = (((1,), (1,)), ((), ()))
    qt = lax.dot_general(wqt_ref[...], t_ref[...], nt, precision=HIGHEST, preferred_element_type=F32)
    half = PEER_DK // 2
    gates, ids = [], []
    for h in range(PEER_HEADS):
        sv, si = [], []
        for s in range(2):
            row0 = (h * 2 + s) * half
            sc = jnp.dot(keys_ref[h * 2 + s], qt[row0:row0 + half, :], precision=HIGHEST,
                         preferred_element_type=F32)
            v, i, _ = _topk_rows(sc, PEER_TOPK)
            sv.append(v)
            si.append(i)
        v2 = jnp.concatenate(sv[1], axis=0)
        i2 = jnp.concatenate(si[1], axis=0)
        width = [PEER_TOPK // (a + 1) for a in range(PEER_TOPK)]
        npad = -sum(width) % SUBLANES
        tm = v2.shape[1]
        cand = jnp.concatenate([sv[0][a] + v2[:width[a]] for a in range(PEER_TOPK)]
                               + [jnp.full((npad, tm), -jnp.inf, F32)], axis=0)
        cexp = jnp.concatenate([si[0][a] * float(PEER_NKEYS) + i2[:width[a]] for a in range(PEER_TOPK)]
                               + [jnp.full((npad, tm), -1.0, F32)], axis=0)
        cv, _, ce = _topk_rows(cand, PEER_TOPK, payload=cexp)
        cvm = jnp.concatenate(cv, axis=0)
        ex = jnp.exp(cvm - cv[0])
        gates.append(ex / jnp.sum(ex, axis=0, keepdims=True))
        ids.extend(ce)
    g_ref[...] = jnp.concatenate(gates, axis=0).T
    e_ref[...] = jnp.concatenate(ids, axis=0).T.astype(jnp.int32)


def peer_route(t, w_q, keys):
    n, d = t.shape
    tm = ROW_TILE
    hk = PEER_HEADS * PEER_TOPK
    keys2 = keys.reshape(PEER_HEADS * 2, PEER_NKEYS, PEER_DK // 2)
    out = pl.BlockSpec((tm, hk), lambda i: (i, 0))
    return pl.pallas_call(
        _route_kernel,
        grid=(n // tm,),
        in_specs=[pl.BlockSpec((tm, d), lambda i: (i, 0)), _full((w_q.shape[1], d)), _full(keys2.shape)],
        out_specs=[out, out],
        out_shape=[jax.ShapeDtypeStruct((n, hk), jnp.int32), jax.ShapeDtypeStruct((n, hk), F32)],
        compiler_params=_cparams("arbitrary"),
        name="peer_route",
    )(t, w_q.T, keys2)


GATHER_STRIDE = PEER_HEADS * PEER_TOPK + SUBLANES


PEER_GROUP = 4
PEER_GATHER_PARTS = 16
PEER_ANCHOR_LAG = 12


def _split3(x):
    hi = x.astype(BF16)
    r1 = x - hi.astype(F32)
    mid = r1.astype(BF16)
    lo = (r1 - mid.astype(F32)).astype(BF16)
    return hi, mid, lo


def _gather_rows(idx_ref, tab_ref, g_ref, t, k0, k1, after=None):
    rows = tab_ref.shape[1]
    ids = idx_ref.at[t]
    for k in range(k0, k1):
        row = tab_ref[ids[k]]
        g_ref[pl.ds(k, rows, stride=GATHER_STRIDE), :] = row if after is None else row + after


def _zero_after(x):
    bits = pltpu.bitcast(x, jnp.uint32)
    return pltpu.bitcast(lax.shift_right_logical(bits, jnp.full_like(bits, 32)), jnp.int32)


def _chunk_matrix(g_ref, c, nk):
    return pltpu.bitcast(g_ref[GATHER_STRIDE * c:GATHER_STRIDE * c + nk, :], BF16)


def _token_pipeline(idx_ref, idx_next_ref, tab_ref, tm, nk, prepare, compute_chunk, finish, anchor, tiles):
    nchunk = tab_ref.shape[1]
    grp = PEER_GROUP
    per = nk // PEER_GATHER_PARTS
    set_a, set_b = tiles[:grp], tiles[grp:]
    ngroups = tm // grp
    assert ngroups % 2 == 0, "every token block must start on tile set A"

    @pl.when(pl.program_id(0) == 0)
    def _():
        for i in range(grp):
            _gather_rows(idx_ref, tab_ref, set_a[i], i, 0, nk)

    def group(t0, cur, nxt, ids_ref, t_ids):
        dots = [(i, c) for i in range(grp) for c in range(nchunk)]
        parts = [(i, j) for i in range(grp) for j in range(PEER_GATHER_PARTS)]
        acc = [prepare(t0 + i) for i in range(grp)]
        done = []
        for n in range(max(len(dots), len(parts))):
            if n < len(dots):
                i, c = dots[n]
                acc[i] = compute_chunk(t0 + i, c, _chunk_matrix(cur[i], c, nk), acc[i])
                done.append(anchor(acc[i]))
                if c == nchunk - 1:
                    finish(t0 + i, acc[i])
            if n < len(parts):
                i, j = parts[n]
                m = min(n - PEER_ANCHOR_LAG, len(dots) - 1)
                after = _zero_after(done[m]) if m >= 0 and (m < len(dots) - 1 or n == len(parts) - 1) else None
                _gather_rows(ids_ref, tab_ref, nxt[i], t_ids + i, j * per, (j + 1) * per, after)

    def step(g, carry):
        @pl.when(g % 2 == 0)
        def _():
            group(g * grp, set_a, set_b, idx_ref, (g + 1) * grp)

        @pl.when(g % 2 == 1)
        def _():
            group(g * grp, set_b, set_a, idx_ref, (g + 1) * grp)

        return carry

    lax.fori_loop(0, ngroups - 1, step, 0)
    group((ngroups - 1) * grp, set_b, set_a, idx_next_ref, 0)


def _peer_u_kernel(idx_ref, idx_next_ref, tab_ref, th_ref, gate_ref, mask_ref, pool_ref, o_ref, s_ref, *tiles,
                   tm, nk):
    nchunk = tab_ref.shape[1]
    nt = (((1,), (1,)), ((), ()))

    def prepare(t):
        x = th_ref[t]
        hi = x.astype(BF16).astype(F32)
        lhs = jnp.concatenate([hi, x - hi], axis=0).astype(BF16)
        return lhs, jnp.zeros((SUBLANES, 2 * nk), F32)

    def compute_chunk(t, c, w, acc):
        lhs, s = acc
        out = lax.dot_general(lhs, w, nt, preferred_element_type=F32)
        return lhs, s + (out[:SUBLANES] + out[SUBLANES:]) * mask_ref[c]

    def finish(t, acc):
        s_ref[pl.ds(t, 1), :] = jnp.sum(acc[1], axis=0, keepdims=True)

    _token_pipeline(idx_ref, idx_next_ref, tab_ref, tm, nk, prepare, compute_chunk, finish,
                    lambda acc: acc[1][:nchunk, :LANES], tiles)
    act = jnp.dot(s_ref[...], pool_ref[...], precision=HIGHEST, preferred_element_type=F32)
    o_ref[...] = _gelu(act) * gate_ref[...]


def _peer_v_kernel(idx_ref, idx_next_ref, tab_ref, a_ref, elo_ref, ehi_ref, o_ref, x_ref, *tiles, tm, nk):
    nchunk = tab_ref.shape[1]
    for j, p in enumerate(_split3(a_ref[...])):
        x_ref[j] = jnp.dot(p, elo_ref[...], preferred_element_type=F32)
        x_ref[3 + j] = jnp.dot(p, ehi_ref[...], preferred_element_type=F32)

    def prepare(t):
        return (jnp.concatenate([x_ref[r, pl.ds(t, 1), :] for r in range(6)]
                                + [jnp.zeros((SUBLANES - 6, 2 * nk), F32)], axis=0).astype(BF16), None)

    def compute_chunk(t, c, w, state):
        lhs = state[0]
        out = jnp.dot(lhs, w, preferred_element_type=F32)
        o_ref[t, pl.ds(c, 1), :] = out[0:1] + out[1:2] + out[2:3]
        o_ref[t, pl.ds(nchunk + c, 1), :] = out[3:4] + out[4:5] + out[5:6]
        return lhs, out

    _token_pipeline(idx_ref, idx_next_ref, tab_ref, tm, nk, prepare, compute_chunk, lambda t, state: None,
                    lambda state: state[1][:nchunk], tiles)


def _pack_kernel(x_ref, o_ref):
    half = x_ref.shape[2] // 2

    def bf16_bits(x):
        return pltpu.bitcast(x.astype(BF16).astype(F32), jnp.uint32)

    lo = lax.shift_right_logical(bf16_bits(x_ref[0, :, :half]), jnp.uint32(16))
    hi = bf16_bits(x_ref[0, :, half:]) & jnp.uint32(0xFFFF0000)
    o_ref[...] = pltpu.bitcast(lo | hi, jnp.int32)


def _pack_table(tabs, layer):
    _, e, d = tabs.shape
    half = d // 2
    te = 512
    w = pl.pallas_call(
        _pack_kernel,
        grid=(e // te,),
        in_specs=[pl.BlockSpec((1, te, d), lambda i: (layer, i, 0))],
        out_specs=pl.BlockSpec((te, half), lambda i: (i, 0)),
        out_shape=jax.ShapeDtypeStruct((e, half), jnp.int32),
        compiler_params=_cparams("arbitrary"),
        name="pack_table",
    )(tabs)
    return w.reshape(e, half // LANES, LANES)


def peer_experts(idx, gate, th, u_tabs, v_tabs, layer):
    n, nk = idx.shape
    tm = PEER_TOK_TILE
    nchunk = th.shape[1] // 2
    up, vp = _pack_table(u_tabs, layer), _pack_table(v_tabs, layer)
    rho = np.arange(2 * nk)
    mask = np.zeros((nchunk, SUBLANES, 2 * nk), np.float32)
    for c in range(nchunk):
        mask[c, c, rho % 2 == 0] = 1.0
        mask[c, nchunk + c, rho % 2 == 1] = 1.0
    pool = (rho[:, None] // 2 == np.arange(nk)[None, :]).astype(np.float32)
    elo = (np.arange(nk)[:, None] * 2 == rho[None, :]).astype(np.float32)
    ehi = (np.arange(nk)[:, None] * 2 + 1 == rho[None, :]).astype(np.float32)
    smem = pl.BlockSpec((tm, nk), lambda i: (i, 0), memory_space=pltpu.SMEM)
    assert PEER_GROUP <= SUBLANES
    smem_next = pl.BlockSpec((SUBLANES, nk), lambda i: (jnp.minimum((i + 1) * (tm // SUBLANES), n // SUBLANES - 1), 0),
                             memory_space=pltpu.SMEM)
    tab_spec = pl.BlockSpec(up.shape, lambda i: (0, 0, 0), pipeline_mode=pl.Buffered(1))
    row = pl.BlockSpec((tm, nk), lambda i: (i, 0))
    th_spec = pl.BlockSpec((tm,) + th.shape[1:], lambda i: (i, 0, 0))
    gather_tiles = [pltpu.VMEM((GATHER_STRIDE * nchunk, LANES), jnp.int32)] * (2 * PEER_GROUP)
    act = pl.pallas_call(
        functools.partial(_peer_u_kernel, tm=tm, nk=nk),
        grid=(n // tm,),
        in_specs=[smem, smem_next, tab_spec, th_spec, row, _full(mask.shape), _full(pool.shape)],
        out_specs=row,
        out_shape=jax.ShapeDtypeStruct((n, nk), F32),
        scratch_shapes=[pltpu.VMEM((tm, 2 * nk), F32)] + gather_tiles,
        compiler_params=_cparams("arbitrary"),
        name="peer_u",
    )(idx, idx, up, th, gate, jnp.asarray(mask), jnp.asarray(pool))
    return pl.pallas_call(
        functools.partial(_peer_v_kernel, tm=tm, nk=nk),
        grid=(n // tm,),
        in_specs=[smem, smem_next, tab_spec, row, _full(elo.shape), _full(ehi.shape)],
        out_specs=th_spec,
        out_shape=jax.ShapeDtypeStruct(th.shape, F32),
        scratch_shapes=[pltpu.VMEM((6, tm, 2 * nk), F32)] + gather_tiles,
        compiler_params=_cparams("arbitrary"),
        name="peer_v",
    )(idx, idx, vp, act, jnp.asarray(elo, BF16), jnp.asarray(ehi, BF16))


def peer_channel(t, w_q, keys, u_tabs, v_tabs, layer):
    n, d = t.shape
    e, g = peer_route(t, w_q, keys)
    f = peer_experts(e, g, t.reshape(n, d // LANES, LANES), u_tabs, v_tabs, layer)
    return f.reshape(n, d)


def _mla_proj_kernel(h_ref, mod_ref, ck_ref, sk_ref, cq_ref, sq_ref, wdc_ref, wdr_ref, wdrs_ref, kvn_ref,
                     wuk_ref, wuv_ref, wdq_ref, qn_ref, wqn_ref, wqr_ref, wqrs_ref, k_ref, v_ref, q_ref, *, scale):
    m = (h_ref[...] * (1.0 + mod_ref[0, 1:2, :]) + mod_ref[0, 0:1, :]).astype(BF16)
    ckv = _rms_norm(jnp.dot(m, wdc_ref[...], preferred_element_type=F32), kvn_ref[...]).astype(BF16)
    kr = (jnp.dot(m, wdr_ref[...], preferred_element_type=F32) * ck_ref[...]
          + jnp.dot(m, wdrs_ref[...], preferred_element_type=F32) * sk_ref[...]).astype(BF16)
    kn = jnp.dot(ckv, wuk_ref[...], preferred_element_type=F32).astype(BF16)
    vt = lax.dot_general(wuv_ref[...], ckv, (((1,), (1,)), ((), ())),
                         preferred_element_type=F32).astype(BF16)
    cq = _rms_norm(jnp.dot(m, wdq_ref[...], preferred_element_type=F32), qn_ref[...]).astype(BF16)
    qn = (jnp.dot(cq, wqn_ref[...], preferred_element_type=F32) * scale).astype(BF16)
    qr = ((jnp.dot(cq, wqr_ref[...], preferred_element_type=F32) * cq_ref[...]
           + jnp.dot(cq, wqrs_ref[...], preferred_element_type=F32) * sq_ref[...]) * scale).astype(BF16)
    for h in range(MLA_HEADS):
        k_ref[0, h, :, 0:MLA_NOPE] = kn[:, h * MLA_NOPE:(h + 1) * MLA_NOPE]
        k_ref[0, h, :, MLA_NOPE:] = kr
        v_ref[0, h, 0:MLA_V, :] = vt[h * MLA_V:(h + 1) * MLA_V, :]
        v_ref[0, h, MLA_V:, :] = (lax.broadcasted_iota(jnp.int32, (V_PAD_ROWS, vt.shape[1]), 0) == 0).astype(BF16)
        q_ref[0, h, :, 0:MLA_NOPE] = qn[:, h * MLA_NOPE:(h + 1) * MLA_NOPE]
        q_ref[0, h, :, MLA_NOPE:] = qr[:, h * MLA_ROPE:(h + 1) * MLA_ROPE]


def _swap_halves(w, width):
    r = w.reshape(w.shape[0], -1, 2, width // 2)
    return r[:, :, ::-1, :].reshape(w.shape)


def mla_project(x, mod_tiles, ctx_len, nbatch, w_dq, q_norm, w_uq, w_dkv, kv_norm, w_ukv):
    n, d = x.shape
    ltot = n // nbatch
    nlat = ltot - ctx_len
    tm = ROW_TILE
    tpb = ltot // tm
    hd = MLA_NOPE + MLA_ROPE
    kvl = kv_norm.shape[0]
    ql = q_norm.shape[0]
    pos = jnp.arange(nlat, dtype=jnp.int32)
    nf = MLA_ROPE // 4
    inv = ROPE_BASE ** (-jnp.arange(nf, dtype=F32) / nf)
    ang = jnp.concatenate([(pos // GRID_W).astype(F32)[:, None] * inv,
                           (pos % GRID_W).astype(F32)[:, None] * inv], axis=-1)
    cos = jnp.concatenate([jnp.ones((ctx_len, MLA_ROPE // 2), F32), jnp.cos(ang)], axis=0)
    sin = jnp.concatenate([jnp.zeros((ctx_len, MLA_ROPE // 2), F32), jnp.sin(ang)], axis=0)
    ck = jnp.concatenate([cos, cos], axis=1)
    sk = jnp.concatenate([-sin, sin], axis=1)
    cq = jnp.tile(ck, (1, MLA_HEADS))
    sq = jnp.tile(sk, (1, MLA_HEADS))
    w_dkv_c = w_dkv[:, :kvl].astype(BF16)
    w_dkv_r = w_dkv[:, kvl:]
    ukv = w_ukv.reshape(kvl, MLA_HEADS, MLA_NOPE + MLA_V)
    w_uk = ukv[:, :, :MLA_NOPE].reshape(kvl, -1).astype(BF16)
    w_uv = ukv[:, :, MLA_NOPE:].reshape(kvl, -1).T.astype(BF16)
    uq = w_uq.reshape(ql, MLA_HEADS, hd)
    w_qn = uq[:, :, :MLA_NOPE].reshape(ql, -1).astype(BF16)
    w_qr = uq[:, :, MLA_NOPE:].reshape(ql, -1)
    args = (x, mod_tiles, ck, sk, cq, sq, w_dkv_c, w_dkv_r.astype(BF16),
            _swap_halves(w_dkv_r, MLA_ROPE).astype(BF16), kv_norm.reshape(1, kvl), w_uk, w_uv,
            w_dq.astype(BF16), q_norm.reshape(1, ql), w_qn, w_qr.astype(BF16),
            _swap_halves(w_qr, MLA_ROPE).astype(BF16))
    pos_spec = lambda w: pl.BlockSpec((tm, w), lambda i: (i % tpb, 0))
    in_specs = [pl.BlockSpec((tm, d), lambda i: (i, 0)), pl.BlockSpec((1, N_MOD, d), lambda i: (i, 0, 0)),
                pos_spec(MLA_ROPE), pos_spec(MLA_ROPE), pos_spec(MLA_ROPE * MLA_HEADS),
                pos_spec(MLA_ROPE * MLA_HEADS)] + [_full(a.shape) for a in args[6:]]
    head_spec = lambda w: pl.BlockSpec((1, MLA_HEADS, tm, w), lambda i: (i // tpb, 0, i % tpb, 0))
    return pl.pallas_call(
        functools.partial(_mla_proj_kernel, scale=float(hd) ** -0.5 * math.log2(math.e)),
        grid=(n // tm,),
        in_specs=in_specs,
        out_specs=[head_spec(hd),
                   pl.BlockSpec((1, MLA_HEADS, MLA_V + V_PAD_ROWS, tm), lambda i: (i // tpb, 0, 0, i % tpb)),
                   head_spec(hd)],
        out_shape=[jax.ShapeDtypeStruct((nbatch, MLA_HEADS, ltot, hd), BF16),
                   jax.ShapeDtypeStruct((nbatch, MLA_HEADS, MLA_V + V_PAD_ROWS, ltot), BF16),
                   jax.ShapeDtypeStruct((nbatch, MLA_HEADS, ltot, hd), BF16)],
        compiler_params=_cparams("arbitrary"),
        name="mla_project",
    )(*args)


def _attn_kernel(q_ref, k_ref, vt_ref, o_ref, *, tk, nchunks):
    q = q_ref[0, 0]
    tq = q.shape[0]
    nt = (((1,), (1,)), ((), ()))

    def scores(c):
        return lax.dot_general(k_ref[0, 0, c * tk:(c + 1) * tk, :], q, nt, preferred_element_type=F32)

    m_i = jnp.full((1, tq), -jnp.inf, F32)
    acc = jnp.zeros((vt_ref.shape[2], tq), F32)
    s = scores(0)
    for c in range(nchunks):
        s_next = scores(c + 1) if c + 1 < nchunks else None
        m_new = jnp.maximum(m_i, jnp.max(s, axis=0, keepdims=True))
        p = jnp.exp2(s - m_new).astype(BF16)
        acc = jnp.exp2(m_i - m_new) * acc + jnp.dot(vt_ref[0, 0, :, c * tk:(c + 1) * tk], p,
                                                    preferred_element_type=F32)
        m_i, s = m_new, s_next
    o_ref[0] = (acc[:MLA_V] / acc[MLA_V:MLA_V + 1]).T


def mla_attend(q, k, vt, ctx_len):
    nb, nh, ltot, hd = k.shape
    tq = ROW_TILE
    assert ctx_len % tq == 0
    skip = ctx_len // tq
    nlat = ltot - ctx_len
    tk = ATT_KV_CHUNK
    assert ltot % tk == 0
    return pl.pallas_call(
        functools.partial(_attn_kernel, tk=tk, nchunks=ltot // tk),
        grid=(nb, nh, nlat // tq),
        in_specs=[pl.BlockSpec((1, 1, tq, hd), lambda b, h, i: (b, h, i + skip, 0)),
                  pl.BlockSpec((1, 1, ltot, hd), lambda b, h, i: (b, h, 0, 0)),
                  pl.BlockSpec((1, 1, MLA_V + V_PAD_ROWS, ltot), lambda b, h, i: (b, h, 0, 0))],
        out_specs=pl.BlockSpec((1, tq, MLA_V), lambda b, h, i: (b, i, h)),
        out_shape=jax.ShapeDtypeStruct((nb, nlat, nh * MLA_V), F32),
        compiler_params=_cparams("arbitrary", "arbitrary", "arbitrary"),
        name="mla_attend",
    )(q, k, vt)


def kernel(x, c, ctx, c_ctx, ada_w, ada_b, ln_g, ln_b, s5_a_re, s5_a_im, s5_log_dt, s5_b_re, s5_b_im, s5_c_re, s5_c_im, s5_d, s5_w_glu, s5_w_o, mla_w_dq, mla_q_norm, mla_w_uq, mla_w_dkv, mla_kv_norm, mla_w_ukv, mla_w_o, peer_w_q, peer_keys, peer_u, peer_v):
    nb, nlat, d = x.shape
    nctx = ctx.shape[1]
    ltot = nctx + nlat
    depth = ada_w.shape[0]
    alpha = (2 * depth) ** 0.25
    tm = ROW_TILE

    cvec = jnp.zeros((SUBLANES, d), F32).at[:nb].set(c).at[nb].set(c_ctx)
    tiles_per_batch = ltot // tm
    tile_rows_all = np.array([nb if j < nctx // tm else b for b in range(nb) for j in range(tiles_per_batch)])
    tile_rows_lat = np.array([b for b in range(nb) for _ in range(nlat // tm)])
    seg_rows = np.array([[nb] * nb, list(range(nb))])

    h_all = jnp.concatenate([ctx, x], axis=1).reshape(nb * ltot, d)

    mod = ada_mod(cvec, ada_w, ada_b, 0).reshape(SUBLANES, N_MOD, d)
    mod_all = mod[tile_rows_all]
    bfc, ccc, ar, ai = s5_prepare(s5_a_re[0], s5_a_im[0], s5_log_dt[0], s5_b_re[0], s5_b_im[0],
                                  s5_c_re[0], s5_c_im[0], nb)
    yf, yb = s5_scan(h_all.reshape(nb, ltot, d), mod[seg_rows], bfc, ccc, ar, ai, nctx)
    h1, t = post_mixer(h_all, (yf.reshape(-1, d), yb.reshape(-1, d)), mod_all, s5_w_o[0], ln_g[0, 0], ln_b[0, 0],
                       alpha, d_skip=s5_d[0], w_glu=s5_w_glu[0])
    f = peer_channel(t, peer_w_q[0], peer_keys[0], peer_u, peer_v, 0)
    h_all = residual_ln(h1, f, mod_all, ln_g[0, 1], ln_b[0, 1], alpha)

    mod = ada_mod(cvec, ada_w, ada_b, 1).reshape(SUBLANES, N_MOD, d)
    k, v, q = mla_project(h_all, mod[tile_rows_all], nctx, nb, mla_w_dq[0], mla_q_norm[0], mla_w_uq[0],
                          mla_w_dkv[0], mla_kv_norm[0], mla_w_ukv[0])
    att = mla_attend(q, k, v, nctx).reshape(nb * nlat, -1)
    h_lat = h_all.reshape(nb, ltot, d)[:, nctx:].reshape(nb * nlat, d)
    mod_lat = mod[tile_rows_lat]
    h1, t = post_mixer(h_lat, (att,), mod_lat, mla_w_o[0], ln_g[1, 0], ln_b[1, 0], alpha)
    f = peer_channel(t, peer_w_q[1], peer_keys[1], peer_u, peer_v, 1)
    out = residual_ln(h1, f, mod_lat, ln_g[1, 1], ln_b[1, 1], alpha)
    return out.reshape(nb, nlat, d).astype(x.dtype)
```

```python
import functools
import math

import jax
import jax.numpy as jnp
import numpy as np
from jax import lax
from jax.experimental import pallas as pl
from jax.experimental.pallas import tpu as pltpu

F32 = jnp.float32
BF16 = jnp.bfloat16
HIGHEST = lax.Precision.HIGHEST

LANES = 128
SUBLANES = 8
VMEM_LIMIT_BYTES = 56 * 1024 * 1024

N_MOD = 6
GRID_W = 64
S5_GROUP = 16
S5_STATE = 64
MLA_HEADS = 8
MLA_NOPE = 128
MLA_ROPE = 64
MLA_V = 128
ROPE_BASE = 10000.0
PEER_HEADS = 8
PEER_NKEYS = 128
PEER_DK = 128
PEER_TOPK = 16
LN_EPS = 1e-5
RMS_EPS = 1e-6

ROW_TILE = 256
S5_CHUNK = 128
S5_SLAB_GROUPS = 2
S5_STEP_UNROLL = 8
PEER_TOK_TILE = 64
ATT_KV_CHUNK = 2816
V_PAD_ROWS = 16


def _cparams(*sem):
    return pltpu.CompilerParams(dimension_semantics=sem, vmem_limit_bytes=VMEM_LIMIT_BYTES)


def _full(shape):
    n = len(shape)
    return pl.BlockSpec(shape, lambda *_: (0,) * n)


def _gelu(x):
    return 0.5 * x * (1.0 + lax.erf(x * (1.0 / math.sqrt(2.0))))


def _layer_norm(x, g, b):
    mu = jnp.mean(x, axis=-1, keepdims=True)
    xc = x - mu
    var = jnp.mean(xc * xc, axis=-1, keepdims=True)
    return xc * lax.rsqrt(var + LN_EPS) * g + b


def _rms_norm(x, g):
    return x * lax.rsqrt(jnp.mean(x * x, axis=-1, keepdims=True) + RMS_EPS) * g


def _bdot(a, b):
    return jnp.dot(a.astype(BF16), b, preferred_element_type=F32)


def _ada_kernel(c_ref, w_ref, b_ref, o_ref):
    c = c_ref[...]
    s = c * jax.nn.sigmoid(c)
    o_ref[...] = jnp.dot(s, w_ref[0], precision=HIGHEST, preferred_element_type=F32) + b_ref[0]


def ada_mod(cvec, ws, bs, layer):
    d = cvec.shape[1]
    depth, _, n = ws.shape
    tn = 1024
    return pl.pallas_call(
        _ada_kernel,
        grid=(n // tn,),
        in_specs=[_full((SUBLANES, d)), pl.BlockSpec((1, d, tn), lambda j: (layer, 0, j)),
                  pl.BlockSpec((1, 1, tn), lambda j: (layer, 0, j))],
        out_specs=pl.BlockSpec((SUBLANES, tn), lambda j: (0, j)),
        out_shape=jax.ShapeDtypeStruct((SUBLANES, n), F32),
        compiler_params=_cparams("arbitrary"),
        name="ada_mod",
    )(cvec, ws, bs.reshape(depth, 1, n))


def _s5_kernel(xf_ref, xb_ref, mod_ref, bf_ref, cc_ref, ar_ref, ai_ref, yf_ref, yb_ref, bu_ref, x_ref,
               *, chunk, nbatch):
    nlc = bf_ref.shape[1]
    spc = bf_ref.shape[3] // (2 * LANES)

    @pl.when(pl.program_id(0) == 0)
    def _():
        x_ref[...] = jnp.zeros_like(x_ref)

    for d, src in enumerate((xf_ref, xb_ref)):
        u = jnp.concatenate([(src[b] * (1.0 + mod_ref[0, b, 1:2, :]) + mod_ref[0, b, 0:1, :]).astype(BF16)
                             for b in range(nbatch)], axis=0)
        for j in range(nlc):
            res = jnp.dot(u[:, LANES * j:LANES * (j + 1)], bf_ref[d, j], preferred_element_type=F32)
            for b in range(nbatch):
                for part in range(2):
                    r = d * nbatch + b + 2 * nbatch * part
                    for k in range(spc):
                        col = (part * spc + k) * LANES
                        bu_ref[spc * j + k, pl.ds(r, chunk, stride=SUBLANES), :] = (
                            res[b * chunk:(b + 1) * chunk, col:col + LANES])

    nslab = x_ref.shape[0]
    gs = nslab // S5_SLAB_GROUPS
    row = lax.broadcasted_iota(jnp.int32, (gs,) + x_ref.shape[1:], 1)
    is_bwd = (row % (2 * nbatch)) >= nbatch
    for g0 in range(0, nslab, gs):
        ar = ar_ref[g0:g0 + gs]
        ai = ai_ref[g0:g0 + gs]

        def step(s, x, g0=g0, ar=ar, ai=ai):
            off_f = pl.multiple_of(s * SUBLANES, SUBLANES)
            off_b = pl.multiple_of((chunk - 1 - s) * SUBLANES, SUBLANES)
            slab_f = bu_ref[g0:g0 + gs, pl.ds(off_f, SUBLANES), :]
            slab_b = bu_ref[g0:g0 + gs, pl.ds(off_b, SUBLANES), :]
            xn = ar * x + ai * pltpu.roll(x, SUBLANES // 2, axis=1) + jnp.where(is_bwd, slab_b, slab_f)
            bu_ref[g0:g0 + gs, pl.ds(off_f, SUBLANES), :] = jnp.where(is_bwd, slab_f, xn)
            bu_ref[g0:g0 + gs, pl.ds(off_b, SUBLANES), :] = jnp.where(is_bwd, xn, slab_b)
            return xn

        def steps(i, x, step=step):
            for j in range(S5_STEP_UNROLL):
                x = step(i * S5_STEP_UNROLL + j, x)
            return x

        x_ref[g0:g0 + gs] = lax.fori_loop(0, chunk // S5_STEP_UNROLL, steps, x_ref[g0:g0 + gs])

    def state_rows(r, j):
        return jnp.concatenate([bu_ref[spc * j + k, pl.ds(r, chunk, stride=SUBLANES), :] for k in range(spc)],
                               axis=1).astype(BF16)

    for d, out in enumerate((yf_ref, yb_ref)):
        for j in range(nlc):
            h = jnp.concatenate(
                [jnp.concatenate([state_rows(d * nbatch + b, j), state_rows(d * nbatch + b + 2 * nbatch, j)], axis=1)
                 for b in range(nbatch)], axis=0)
            y = jnp.dot(h, cc_ref[d, j], preferred_element_type=F32)
            for b in range(nbatch):
                out[b, :, LANES * j:LANES * (j + 1)] = y[b * chunk:(b + 1) * chunk]


def s5_scan(x, mod_seg, bfc, ccc, ar, ai, ctx_len):
    nb, ltot, d = x.shape
    assert 2 * nb * 2 == SUBLANES, "state rows must fill one sublane tile"
    chunk = S5_CHUNK
    assert ltot % chunk == 0 and ctx_len % chunk == 0
    ctx_chunks = ctx_len // chunk
    nchunks = ltot // chunk
    nslab = ar.shape[0]
    fwd_spec = pl.BlockSpec((nb, chunk, d), lambda c: (0, c, 0))
    bwd_spec = pl.BlockSpec((nb, chunk, d), lambda c: (0, (nchunks - 1 - c + ctx_chunks) % nchunks, 0))
    return pl.pallas_call(
        functools.partial(_s5_kernel, chunk=chunk, nbatch=nb),
        grid=(nchunks,),
        in_specs=[fwd_spec, bwd_spec,
                  pl.BlockSpec((1, nb, N_MOD, d), lambda c: (jnp.minimum(c // ctx_chunks, 1), 0, 0, 0)),
                  _full(bfc.shape), _full(ccc.shape), _full(ar.shape), _full(ai.shape)],
        out_specs=[fwd_spec, bwd_spec],
        out_shape=[jax.ShapeDtypeStruct(x.shape, F32)] * 2,
        scratch_shapes=[pltpu.VMEM((nslab, SUBLANES * chunk, LANES), F32),
                        pltpu.VMEM((nslab, SUBLANES, LANES), F32)],
        compiler_params=_cparams("arbitrary"),
        name="s5_scan",
    )(x, x, mod_seg, bfc, ccc, ar, ai)


def s5_prepare(a_re, a_im, log_dt, b_re, b_im, c_re, c_im, nbatch):
    ndir, g, p = a_re.shape
    gc = b_re.shape[-1]
    gpc = LANES // gc
    nlc = g // gpc
    dt = jnp.exp(log_dt.astype(F32))[..., None]
    a_re = a_re.astype(F32)
    a_im = a_im.astype(F32)
    mag = jnp.exp(a_re * dt)
    ab_re = mag * jnp.cos(a_im * dt)
    ab_im = mag * jnp.sin(a_im * dt)
    nr, ni = ab_re - 1.0, ab_im
    den = a_re * a_re + a_im * a_im
    f_re = (nr * a_re + ni * a_im) / den
    f_im = (ni * a_re - nr * a_im) / den
    bf_re = f_re[..., None] * b_re - f_im[..., None] * b_im
    bf_im = f_re[..., None] * b_im + f_im[..., None] * b_re
    bf = jnp.stack([bf_re, bf_im], axis=1).reshape(ndir, 2, nlc, gpc, p, gc)
    eye = jnp.eye(gpc, dtype=F32)
    bfc = jnp.einsum('dqjgpc,gh->djgcqhp', bf, eye).reshape(ndir, nlc, gpc * gc, 2 * gpc * p).astype(BF16)
    cc = jnp.stack([c_re, -c_im], axis=1).astype(F32).reshape(ndir, 2, nlc, gpc, gc, p)
    ccc = jnp.einsum('dqjgcp,gh->djqgphc', cc, eye).reshape(ndir, nlc, 2 * gpc * p, gpc * gc).astype(BF16)
    rows_r, rows_i = [], []
    for part in range(2):
        for d in range(ndir):
            for _ in range(nbatch):
                rows_r.append(ab_re[d].reshape(-1))
                rows_i.append(ab_im[d].reshape(-1) * (-1.0 if part == 0 else 1.0))
    slabs = lambda rows: jnp.stack(rows).reshape(len(rows), -1, LANES).transpose(1, 0, 2)
    return bfc, ccc, slabs(rows_r), slabs(rows_i)


def _post_kernel(*refs, alpha, glu):
    if glu:
        h_ref, yf_ref, yb_ref, mod_ref, dsk_ref, wg_ref, wo_ref, lng_ref, lnb_ref, h1_ref, t_ref = refs
        h = h_ref[...]
        m = h * (1.0 + mod_ref[0, 1:2, :]) + mod_ref[0, 0:1, :]
        z = _gelu(yf_ref[...] + yb_ref[...] + dsk_ref[...] * m)
        z = z * jax.nn.sigmoid(_bdot(z, wg_ref[...]))
    else:
        h_ref, z_ref, mod_ref, wo_ref, lng_ref, lnb_ref, h1_ref, t_ref = refs
        h = h_ref[...]
        z = z_ref[...]
    o = _bdot(z, wo_ref[...])
    h1 = _layer_norm(alpha * h + mod_ref[0, 2:3, :] * o, lng_ref[...], lnb_ref[...])
    h1_ref[...] = h1
    t_ref[...] = h1 * (1.0 + mod_ref[0, 4:5, :]) + mod_ref[0, 3:4, :]


def post_mixer(h, pre, mod_tiles, w_o, ln_g, ln_b, alpha, d_skip=None, w_glu=None):
    n, d = h.shape
    tm = ROW_TILE
    glu = w_glu is not None
    row = pl.BlockSpec((tm, d), lambda i: (i, 0))
    vec = _full((1, d))
    modspec = pl.BlockSpec((1, N_MOD, d), lambda i: (i, 0, 0))
    if glu:
        args = (h, pre[0], pre[1], mod_tiles, d_skip.reshape(1, d), w_glu.astype(BF16), w_o.astype(BF16),
                ln_g.reshape(1, d), ln_b.reshape(1, d))
        in_specs = [row, row, row, modspec, vec, _full((d, d)), _full((d, d)), vec, vec]
    else:
        args = (h, pre[0], mod_tiles, w_o.astype(BF16), ln_g.reshape(1, d), ln_b.reshape(1, d))
        in_specs = [row, pl.BlockSpec((tm, pre[0].shape[1]), lambda i: (i, 0)), modspec,
                    _full(w_o.shape), vec, vec]
    return pl.pallas_call(
        functools.partial(_post_kernel, alpha=alpha, glu=glu),
        grid=(n // tm,),
        in_specs=in_specs,
        out_specs=[row, row],
        out_shape=[jax.ShapeDtypeStruct((n, d), F32)] * 2,
        compiler_params=_cparams("arbitrary"),
        name="post_mixer_glu" if glu else "post_mixer",
    )(*args)


def _res_ln_kernel(h_ref, f_ref, mod_ref, lng_ref, lnb_ref, o_ref, *, alpha):
    o_ref[...] = _layer_norm(alpha * h_ref[...] + mod_ref[0, 5:6, :] * f_ref[...], lng_ref[...], lnb_ref[...])


def residual_ln(h, f, mod_tiles, ln_g, ln_b, alpha):
    n, d = h.shape
    tm = ROW_TILE
    row = pl.BlockSpec((tm, d), lambda i: (i, 0))
    vec = _full((1, d))
    return pl.pallas_call(
        functools.partial(_res_ln_kernel, alpha=alpha),
        grid=(n // tm,),
        in_specs=[row, row, pl.BlockSpec((1, N_MOD, d), lambda i: (i, 0, 0)), vec, vec],
        out_specs=row,
        out_shape=jax.ShapeDtypeStruct((n, d), F32),
        compiler_params=_cparams("arbitrary"),
        name="residual_ln",
    )(h, f, mod_tiles, ln_g.reshape(1, d), ln_b.reshape(1, d))


def _topk_rows(s, k, payload=None):
    nrow = s.shape[0]
    iota = lax.broadcasted_iota(jnp.int32, s.shape, 0).astype(F32)
    vals, idxs, pays = [], [], []
    for _ in range(k):
        m = jnp.max(s, axis=0, keepdims=True)
        i = jnp.min(jnp.where(s == m, iota, float(nrow)), axis=0, keepdims=True)
        hit = iota == i
        vals.append(m)
        idxs.append(i)
        if payload is not None:
            pays.append(jnp.max(jnp.where(hit, payload, -1.0), axis=0, keepdims=True))
        s = jnp.where(hit, -jnp.inf, s)
    return vals, idxs, pays


def _split2(x):
    hi = x.astype(BF16)
    return hi, (x - hi.astype(F32)).astype(BF16)


def _dot3(a_hi, a_lo, b_hi, b_lo, dims):
    dg = functools.partial(lax.dot_general, dimension_numbers=dims, preferred_element_type=F32)
    return dg(a_hi, b_hi) + dg(a_hi, b_lo) + dg(a_lo, b_hi)


def _route_kernel(t_ref, wqt_hi_ref, wqt_lo_ref, keys_hi_ref, keys_lo_ref, e_ref, g_ref):
    nt = (((1,), (1,)), ((), ()))
    nn = (((1,), (0,)), ((), ()))
    qt = _dot3(wqt_hi_ref[...], wqt_lo_ref[...], *_split2(t_ref[...]), nt)
    half = PEER_DK // 2
    gates, ids = [], []
    for h in range(PEER_HEADS):
        sv, si = [], []
        for s in range(2):
            row0 = (h * 2 + s) * half
            sc = _dot3(keys_hi_ref[h * 2 + s], keys_lo_ref[h * 2 + s], *_split2(qt[row0:row0 + half, :]), nn)
            v, i, _ = _topk_rows(sc, PEER_TOPK)
            sv.append(v)
            si.append(i)
        v2 = jnp.concatenate(sv[1], axis=0)
        i2 = jnp.concatenate(si[1], axis=0)
        width = [PEER_TOPK // (a + 1) for a in range(PEER_TOPK)]
        npad = -sum(width) % SUBLANES
        tm = v2.shape[1]
        cand = jnp.concatenate([sv[0][a] + v2[:width[a]] for a in range(PEER_TOPK)]
                               + [jnp.full((npad, tm), -jnp.inf, F32)], axis=0)
        cexp = jnp.concatenate([si[0][a] * float(PEER_NKEYS) + i2[:width[a]] for a in range(PEER_TOPK)]
                               + [jnp.full((npad, tm), -1.0, F32)], axis=0)
        cv, _, ce = _topk_rows(cand, PEER_TOPK, payload=cexp)
        cvm = jnp.concatenate(cv, axis=0)
        ex = jnp.exp(cvm - cv[0])
        gates.append(ex / jnp.sum(ex, axis=0, keepdims=True))
        ids.extend(ce)
    g_ref[...] = jnp.concatenate(gates, axis=0).T
    e_ref[...] = jnp.concatenate(ids, axis=0).T.astype(jnp.int32)


def peer_route(t, w_q, keys):
    n, d = t.shape
    tm = ROW_TILE
    hk = PEER_HEADS * PEER_TOPK
    keys2 = keys.reshape(PEER_HEADS * 2, PEER_NKEYS, PEER_DK // 2)
    wqt = w_q.T
    out = pl.BlockSpec((tm, hk), lambda i: (i, 0))
    return pl.pallas_call(
        _route_kernel,
        grid=(n // tm,),
        in_specs=[pl.BlockSpec((tm, d), lambda i: (i, 0)), _full(wqt.shape), _full(wqt.shape),
                  _full(keys2.shape), _full(keys2.shape)],
        out_specs=[out, out],
        out_shape=[jax.ShapeDtypeStruct((n, hk), jnp.int32), jax.ShapeDtypeStruct((n, hk), F32)],
        compiler_params=_cparams("arbitrary"),
        name="peer_route",
    )(t, *_split2(wqt), *_split2(keys2))


GATHER_STRIDE = PEER_HEADS * PEER_TOPK + SUBLANES


PEER_GROUP = 4
PEER_GATHER_PARTS = 16
PEER_ANCHOR_LAG = 12


def _split3(x):
    hi = x.astype(BF16)
    r1 = x - hi.astype(F32)
    mid = r1.astype(BF16)
    lo = (r1 - mid.astype(F32)).astype(BF16)
    return hi, mid, lo


def _gather_rows(idx_ref, tab_ref, g_ref, t, k0, k1, after=None):
    npair = tab_ref.shape[1] // 2
    ids = idx_ref.at[t]
    for k in range(k0, k1):
        slab = tab_ref[ids[k]]
        for c in range(npair):
            piece = slab[2 * c:2 * c + 2, :]
            g_ref[pl.ds(2 * (c * GATHER_STRIDE + k), 2), :] = piece if after is None else piece + after


def _zero_after(x):
    bits = pltpu.bitcast(x[:2, :LANES], jnp.uint32)
    zero = pltpu.bitcast(lax.shift_right_logical(bits, jnp.full_like(bits, 32)), jnp.int32)
    return zero.astype(F32).astype(BF16)


def _chunk_matrix(g_ref, c, nk):
    return g_ref[2 * GATHER_STRIDE * c:2 * GATHER_STRIDE * c + 2 * nk, :]


def _token_pipeline(idx_ref, idx_next_ref, tab_ref, tm, nk, prepare, compute_chunk, finish, anchor, tiles):
    nchunk = tab_ref.shape[1] // 2
    grp = PEER_GROUP
    per = nk // PEER_GATHER_PARTS
    set_a, set_b = tiles[:grp], tiles[grp:]
    ngroups = tm // grp
    assert ngroups % 2 == 0, "every token block must start on tile set A"

    @pl.when(pl.program_id(0) == 0)
    def _():
        for i in range(grp):
            _gather_rows(idx_ref, tab_ref, set_a[i], i, 0, nk)

    def group(t0, cur, nxt, ids_ref, t_ids):
        dots = [(i, c) for i in range(grp) for c in range(nchunk)]
        parts = [(i, j) for i in range(grp) for j in range(PEER_GATHER_PARTS)]
        acc = [prepare(t0 + i) for i in range(grp)]
        done = []
        for n in range(max(len(dots), len(parts))):
            if n < len(dots):
                i, c = dots[n]
                acc[i] = compute_chunk(t0 + i, c, _chunk_matrix(cur[i], c, nk), acc[i])
                done.append(anchor(acc[i]))
                if c == nchunk - 1:
                    finish(t0 + i, acc[i])
            if n < len(parts):
                i, j = parts[n]
                m = min(n - PEER_ANCHOR_LAG, len(dots) - 1)
                after = _zero_after(done[m]) if m >= 0 and (m < len(dots) - 1 or n == len(parts) - 1) else None
                _gather_rows(ids_ref, tab_ref, nxt[i], t_ids + i, j * per, (j + 1) * per, after)

    def step(g, carry):
        @pl.when(g % 2 == 0)
        def _():
            group(g * grp, set_a, set_b, idx_ref, (g + 1) * grp)

        @pl.when(g % 2 == 1)
        def _():
            group(g * grp, set_b, set_a, idx_ref, (g + 1) * grp)

        return carry

    lax.fori_loop(0, ngroups - 1, step, 0)
    group((ngroups - 1) * grp, set_b, set_a, idx_next_ref, 0)


def _peer_u_kernel(idx_ref, idx_next_ref, tab_ref, th_ref, gate_ref, mask_ref, pool_ref, o_ref, s_ref, *tiles,
                   tm, nk):
    nchunk = tab_ref.shape[1] // 2
    nt = (((1,), (1,)), ((), ()))

    def prepare(t):
        x = th_ref[t]
        hi = x.astype(BF16).astype(F32)
        lhs = jnp.concatenate([hi, x - hi], axis=0).astype(BF16)
        return lhs, jnp.zeros((SUBLANES, 2 * nk), F32)

    def compute_chunk(t, c, w, acc):
        lhs, s = acc
        out = lax.dot_general(lhs, w, nt, preferred_element_type=F32)
        return lhs, s + (out[:SUBLANES] + out[SUBLANES:]) * mask_ref[c]

    def finish(t, acc):
        s_ref[pl.ds(t, 1), :] = jnp.sum(acc[1], axis=0, keepdims=True)

    _token_pipeline(idx_ref, idx_next_ref, tab_ref, tm, nk, prepare, compute_chunk, finish,
                    lambda acc: acc[1][:nchunk, :LANES], tiles)
    act = jnp.dot(s_ref[...], pool_ref[...], precision=HIGHEST, preferred_element_type=F32)
    o_ref[...] = _gelu(act) * gate_ref[...]


def _peer_v_kernel(idx_ref, idx_next_ref, tab_ref, a_ref, elo_ref, ehi_ref, o_ref, x_ref, *tiles, tm, nk):
    nchunk = tab_ref.shape[1] // 2
    for j, p in enumerate(_split3(a_ref[...])):
        x_ref[j] = jnp.dot(p, elo_ref[...], preferred_element_type=F32)
        x_ref[3 + j] = jnp.dot(p, ehi_ref[...], preferred_element_type=F32)

    def prepare(t):
        return (jnp.concatenate([x_ref[r, pl.ds(t, 1), :] for r in range(6)]
                                + [jnp.zeros((SUBLANES - 6, 2 * nk), F32)], axis=0).astype(BF16), None)

    def compute_chunk(t, c, w, state):
        lhs = state[0]
        out = jnp.dot(lhs, w, preferred_element_type=F32)
        o_ref[t, pl.ds(c, 1), :] = out[0:1] + out[1:2] + out[2:3]
        o_ref[t, pl.ds(nchunk + c, 1), :] = out[3:4] + out[4:5] + out[5:6]
        return lhs, out

    _token_pipeline(idx_ref, idx_next_ref, tab_ref, tm, nk, prepare, compute_chunk, lambda t, state: None,
                    lambda state: state[1][:nchunk], tiles)


def _pack_table(tabs, layer):
    _, e, d = tabs.shape
    nchunk = d // (2 * LANES)
    t = tabs[layer].astype(BF16).reshape(e, 2, nchunk, LANES)
    return t.transpose(0, 2, 1, 3).reshape(e, 2 * nchunk, LANES)


def peer_experts(idx, gate, th, u_tabs, v_tabs, layer):
    n, nk = idx.shape
    tm = PEER_TOK_TILE
    nchunk = th.shape[1] // 2
    up, vp = _pack_table(u_tabs, layer), _pack_table(v_tabs, layer)
    rho = np.arange(2 * nk)
    mask = np.zeros((nchunk, SUBLANES, 2 * nk), np.float32)
    for c in range(nchunk):
        mask[c, c, rho % 2 == 0] = 1.0
        mask[c, nchunk + c, rho % 2 == 1] = 1.0
    pool = (rho[:, None] // 2 == np.arange(nk)[None, :]).astype(np.float32)
    elo = (np.arange(nk)[:, None] * 2 == rho[None, :]).astype(np.float32)
    ehi = (np.arange(nk)[:, None] * 2 + 1 == rho[None, :]).astype(np.float32)
    smem = pl.BlockSpec((tm, nk), lambda i: (i, 0), memory_space=pltpu.SMEM)
    assert PEER_GROUP <= SUBLANES
    smem_next = pl.BlockSpec((SUBLANES, nk), lambda i: (jnp.minimum((i + 1) * (tm // SUBLANES), n // SUBLANES - 1), 0),
                             memory_space=pltpu.SMEM)
    tab_spec = pl.BlockSpec(up.shape, lambda i: (0, 0, 0), pipeline_mode=pl.Buffered(1))
    row = pl.BlockSpec((tm, nk), lambda i: (i, 0))
    th_spec = pl.BlockSpec((tm,) + th.shape[1:], lambda i: (i, 0, 0))
    gather_tiles = [pltpu.VMEM((2 * GATHER_STRIDE * nchunk, LANES), BF16)] * (2 * PEER_GROUP)
    act = pl.pallas_call(
        functools.partial(_peer_u_kernel, tm=tm, nk=nk),
        grid=(n // tm,),
        in_specs=[smem, smem_next, tab_spec, th_spec, row, _full(mask.shape), _full(pool.shape)],
        out_specs=row,
        out_shape=jax.ShapeDtypeStruct((n, nk), F32),
        scratch_shapes=[pltpu.VMEM((tm, 2 * nk), F32)] + gather_tiles,
        compiler_params=_cparams("arbitrary"),
        name="peer_u",
    )(idx, idx, up, th, gate, jnp.asarray(mask), jnp.asarray(pool))
    return pl.pallas_call(
        functools.partial(_peer_v_kernel, tm=tm, nk=nk),
        grid=(n // tm,),
        in_specs=[smem, smem_next, tab_spec, row, _full(elo.shape), _full(ehi.shape)],
        out_specs=th_spec,
        out_shape=jax.ShapeDtypeStruct(th.shape, F32),
        scratch_shapes=[pltpu.VMEM((6, tm, 2 * nk), F32)] + gather_tiles,
        compiler_params=_cparams("arbitrary"),
        name="peer_v",
    )(idx, idx, vp, act, jnp.asarray(elo, BF16), jnp.asarray(ehi, BF16))


def peer_channel(t, w_q, keys, u_tabs, v_tabs, layer):
    n, d = t.shape
    e, g = peer_route(t, w_q, keys)
    f = peer_experts(e, g, t.reshape(n, d // LANES, LANES), u_tabs, v_tabs, layer)
    return f.reshape(n, d)


def _mla_proj_kernel(h_ref, mod_ref, ck_ref, sk_ref, cq_ref, sq_ref, wdc_ref, wdr_ref, wdrs_ref, kvn_ref,
                     wuk_ref, wuv_ref, wdq_ref, qn_ref, wqn_ref, wqr_ref, wqrs_ref, k_ref, v_ref, q_ref, *, scale):
    m = (h_ref[...] * (1.0 + mod_ref[0, 1:2, :]) + mod_ref[0, 0:1, :]).astype(BF16)
    ckv = _rms_norm(jnp.dot(m, wdc_ref[...], preferred_element_type=F32), kvn_ref[...]).astype(BF16)
    kr = (jnp.dot(m, wdr_ref[...], preferred_element_type=F32) * ck_ref[...]
          + jnp.dot(m, wdrs_ref[...], preferred_element_type=F32) * sk_ref[...]).astype(BF16)
    kn = jnp.dot(ckv, wuk_ref[...], preferred_element_type=F32).astype(BF16)
    vt = lax.dot_general(wuv_ref[...], ckv, (((1,), (1,)), ((), ())),
                         preferred_element_type=F32).astype(BF16)
    cq = _rms_norm(jnp.dot(m, wdq_ref[...], preferred_element_type=F32), qn_ref[...]).astype(BF16)
    qn = (jnp.dot(cq, wqn_ref[...], preferred_element_type=F32) * scale).astype(BF16)
    qr = ((jnp.dot(cq, wqr_ref[...], preferred_element_type=F32) * cq_ref[...]
           + jnp.dot(cq, wqrs_ref[...], preferred_element_type=F32) * sq_ref[...]) * scale).astype(BF16)
    for h in range(MLA_HEADS):
        k_ref[0, h, :, 0:MLA_NOPE] = kn[:, h * MLA_NOPE:(h + 1) * MLA_NOPE]
        k_ref[0, h, :, MLA_NOPE:] = kr
        v_ref[0, h, 0:MLA_V, :] = vt[h * MLA_V:(h + 1) * MLA_V, :]
        v_ref[0, h, MLA_V:, :] = (lax.broadcasted_iota(jnp.int32, (V_PAD_ROWS, vt.shape[1]), 0) == 0).astype(BF16)
        q_ref[0, h, :, 0:MLA_NOPE] = qn[:, h * MLA_NOPE:(h + 1) * MLA_NOPE]
        q_ref[0, h, :, MLA_NOPE:] = qr[:, h * MLA_ROPE:(h + 1) * MLA_ROPE]


def _swap_halves(w, width):
    r = w.reshape(w.shape[0], -1, 2, width // 2)
    return r[:, :, ::-1, :].reshape(w.shape)


def mla_project(x, mod_tiles, ctx_len, nbatch, w_dq, q_norm, w_uq, w_dkv, kv_norm, w_ukv):
    n, d = x.shape
    ltot = n // nbatch
    nlat = ltot - ctx_len
    tm = ROW_TILE
    tpb = ltot // tm
    hd = MLA_NOPE + MLA_ROPE
    kvl = kv_norm.shape[0]
    ql = q_norm.shape[0]
    pos = jnp.arange(nlat, dtype=jnp.int32)
    nf = MLA_ROPE // 4
    inv = ROPE_BASE ** (-jnp.arange(nf, dtype=F32) / nf)
    ang = jnp.concatenate([(pos // GRID_W).astype(F32)[:, None] * inv,
                           (pos % GRID_W).astype(F32)[:, None] * inv], axis=-1)
    cos = jnp.concatenate([jnp.ones((ctx_len, MLA_ROPE // 2), F32), jnp.cos(ang)], axis=0)
    sin = jnp.concatenate([jnp.zeros((ctx_len, MLA_ROPE // 2), F32), jnp.sin(ang)], axis=0)
    ck = jnp.concatenate([cos, cos], axis=1)
    sk = jnp.concatenate([-sin, sin], axis=1)
    cq = jnp.tile(ck, (1, MLA_HEADS))
    sq = jnp.tile(sk, (1, MLA_HEADS))
    w_dkv_c = w_dkv[:, :kvl].astype(BF16)
    w_dkv_r = w_dkv[:, kvl:]
    ukv = w_ukv.reshape(kvl, MLA_HEADS, MLA_NOPE + MLA_V)
    w_uk = ukv[:, :, :MLA_NOPE].reshape(kvl, -1).astype(BF16)
    w_uv = ukv[:, :, MLA_NOPE:].reshape(kvl, -1).T.astype(BF16)
    uq = w_uq.reshape(ql, MLA_HEADS, hd)
    w_qn = uq[:, :, :MLA_NOPE].reshape(ql, -1).astype(BF16)
    w_qr = uq[:, :, MLA_NOPE:].reshape(ql, -1)
    args = (x, mod_tiles, ck, sk, cq, sq, w_dkv_c, w_dkv_r.astype(BF16),
            _swap_halves(w_dkv_r, MLA_ROPE).astype(BF16), kv_norm.reshape(1, kvl), w_uk, w_uv,
            w_dq.astype(BF16), q_norm.reshape(1, ql), w_qn, w_qr.astype(BF16),
            _swap_halves(w_qr, MLA_ROPE).astype(BF16))
    pos_spec = lambda w: pl.BlockSpec((tm, w), lambda i: (i % tpb, 0))
    in_specs = [pl.BlockSpec((tm, d), lambda i: (i, 0)), pl.BlockSpec((1, N_MOD, d), lambda i: (i, 0, 0)),
                pos_spec(MLA_ROPE), pos_spec(MLA_ROPE), pos_spec(MLA_ROPE * MLA_HEADS),
                pos_spec(MLA_ROPE * MLA_HEADS)] + [_full(a.shape) for a in args[6:]]
    head_spec = lambda w: pl.BlockSpec((1, MLA_HEADS, tm, w), lambda i: (i // tpb, 0, i % tpb, 0))
    return pl.pallas_call(
        functools.partial(_mla_proj_kernel, scale=float(hd) ** -0.5 * math.log2(math.e)),
        grid=(n // tm,),
        in_specs=in_specs,
        out_specs=[head_spec(hd),
                   pl.BlockSpec((1, MLA_HEADS, MLA_V + V_PAD_ROWS, tm), lambda i: (i // tpb, 0, 0, i % tpb)),
                   head_spec(hd)],
        out_shape=[jax.ShapeDtypeStruct((nbatch, MLA_HEADS, ltot, hd), BF16),
                   jax.ShapeDtypeStruct((nbatch, MLA_HEADS, MLA_V + V_PAD_ROWS, ltot), BF16),
                   jax.ShapeDtypeStruct((nbatch, MLA_HEADS, ltot, hd), BF16)],
        compiler_params=_cparams("arbitrary"),
        name="mla_project",
    )(*args)


def _attn_kernel(q_ref, k_ref, vt_ref, o_ref, *, tk, nchunks):
    q = q_ref[0, 0]
    tq = q.shape[0]
    nt = (((1,), (1,)), ((), ()))

    def scores(c):
        return lax.dot_general(k_ref[0, 0, c * tk:(c + 1) * tk, :], q, nt, preferred_element_type=F32)

    m_i = jnp.full((1, tq), -jnp.inf, F32)
    acc = jnp.zeros((vt_ref.shape[2], tq), F32)
    s = scores(0)
    for c in range(nchunks):
        s_next = scores(c + 1) if c + 1 < nchunks else None
        m_new = jnp.maximum(m_i, jnp.max(s, axis=0, keepdims=True))
        p = jnp.exp2(s - m_new).astype(BF16)
        acc = jnp.exp2(m_i - m_new) * acc + jnp.dot(vt_ref[0, 0, :, c * tk:(c + 1) * tk], p,
                                                    preferred_element_type=F32)
        m_i, s = m_new, s_next
    o_ref[0] = (acc[:MLA_V] / acc[MLA_V:MLA_V + 1]).T


def mla_attend(q, k, vt, ctx_len):
    nb, nh, ltot, hd = k.shape
    tq = ROW_TILE
    assert ctx_len % tq == 0
    skip = ctx_len // tq
    nlat = ltot - ctx_len
    tk = ATT_KV_CHUNK
    assert ltot % tk == 0
    return pl.pallas_call(
        functools.partial(_attn_kernel, tk=tk, nchunks=ltot // tk),
        grid=(nb, nh, nlat // tq),
        in_specs=[pl.BlockSpec((1, 1, tq, hd), lambda b, h, i: (b, h, i + skip, 0)),
                  pl.BlockSpec((1, 1, ltot, hd), lambda b, h, i: (b, h, 0, 0)),
                  pl.BlockSpec((1, 1, MLA_V + V_PAD_ROWS, ltot), lambda b, h, i: (b, h, 0, 0))],
        out_specs=pl.BlockSpec((1, tq, MLA_V), lambda b, h, i: (b, i, h)),
        out_shape=jax.ShapeDtypeStruct((nb, nlat, nh * MLA_V), F32),
        compiler_params=_cparams("arbitrary", "arbitrary", "arbitrary"),
        name="mla_attend",
    )(q, k, vt)


def kernel(x, c, ctx, c_ctx, ada_w, ada_b, ln_g, ln_b, s5_a_re, s5_a_im, s5_log_dt, s5_b_re, s5_b_im, s5_c_re, s5_c_im, s5_d, s5_w_glu, s5_w_o, mla_w_dq, mla_q_norm, mla_w_uq, mla_w_dkv, mla_kv_norm, mla_w_ukv, mla_w_o, peer_w_q, peer_keys, peer_u, peer_v):
    nb, nlat, d = x.shape
    nctx = ctx.shape[1]
    ltot = nctx + nlat
    depth = ada_w.shape[0]
    alpha = (2 * depth) ** 0.25
    tm = ROW_TILE

    cvec = jnp.zeros((SUBLANES, d), F32).at[:nb].set(c).at[nb].set(c_ctx)
    tiles_per_batch = ltot // tm
    tile_rows_all = np.array([nb if j < nctx // tm else b for b in range(nb) for j in range(tiles_per_batch)])
    tile_rows_lat = np.array([b for b in range(nb) for _ in range(nlat // tm)])
    seg_rows = np.array([[nb] * nb, list(range(nb))])

    h_all = jnp.concatenate([ctx, x], axis=1).reshape(nb * ltot, d)

    mod = ada_mod(cvec, ada_w, ada_b, 0).reshape(SUBLANES, N_MOD, d)
    mod_all = mod[tile_rows_all]
    bfc, ccc, ar, ai = s5_prepare(s5_a_re[0], s5_a_im[0], s5_log_dt[0], s5_b_re[0], s5_b_im[0],
                                  s5_c_re[0], s5_c_im[0], nb)
    yf, yb = s5_scan(h_all.reshape(nb, ltot, d), mod[seg_rows], bfc, ccc, ar, ai, nctx)
    h1, t = post_mixer(h_all, (yf.reshape(-1, d), yb.reshape(-1, d)), mod_all, s5_w_o[0], ln_g[0, 0], ln_b[0, 0],
                       alpha, d_skip=s5_d[0], w_glu=s5_w_glu[0])
    f = peer_channel(t, peer_w_q[0], peer_keys[0], peer_u, peer_v, 0)
    h_all = residual_ln(h1, f, mod_all, ln_g[0, 1], ln_b[0, 1], alpha)

    mod = ada_mod(cvec, ada_w, ada_b, 1).reshape(SUBLANES, N_MOD, d)
    k, v, q = mla_project(h_all, mod[tile_rows_all], nctx, nb, mla_w_dq[0], mla_q_norm[0], mla_w_uq[0],
                          mla_w_dkv[0], mla_kv_norm[0], mla_w_ukv[0])
    att = mla_attend(q, k, v, nctx).reshape(nb * nlat, -1)
    h_lat = h_all.reshape(nb, ltot, d)[:, nctx:].reshape(nb * nlat, d)
    mod_lat = mod[tile_rows_lat]
    h1, t = post_mixer(h_lat, (att,), mod_lat, mla_w_o[0], ln_g[1, 0], ln_b[1, 0], alpha)
    f = peer_channel(t, peer_w_q[1], peer_keys[1], peer_u, peer_v, 1)
    out = residual_ln(h1, f, mod_lat, ln_g[1, 1], ln_b[1, 1], alpha)
    return out.reshape(nb, nlat, d).astype(x.dtype)
```

```python
import functools
import math

import jax
import jax.numpy as jnp
import numpy as np
from jax import lax
from jax.experimental import pallas as pl
from jax.experimental.pallas import tpu as pltpu

F32 = jnp.float32
BF16 = jnp.bfloat16
HIGHEST = lax.Precision.HIGHEST

LANES = 128
SUBLANES = 8
VMEM_LIMIT_BYTES = 56 * 1024 * 1024

N_MOD = 6
GRID_W = 64
S5_GROUP = 16
S5_STATE = 64
MLA_HEADS = 8
MLA_NOPE = 128
MLA_ROPE = 64
MLA_V = 128
ROPE_BASE = 10000.0
PEER_HEADS = 8
PEER_NKEYS = 128
PEER_DK = 128
PEER_TOPK = 16
LN_EPS = 1e-5
RMS_EPS = 1e-6

ROW_TILE = 256
S5_CHUNK = 128
S5_SLAB_GROUPS = 2
S5_STEP_UNROLL = 8
PEER_TOK_TILE = 64
ATT_KV_CHUNK = 2816
V_PAD_ROWS = 16


def _cparams(*sem):
    return pltpu.CompilerParams(dimension_semantics=sem, vmem_limit_bytes=VMEM_LIMIT_BYTES)


def _full(shape):
    n = len(shape)
    return pl.BlockSpec(shape, lambda *_: (0,) * n)


def _gelu(x):
    return 0.5 * x * (1.0 + lax.erf(x * (1.0 / math.sqrt(2.0))))


def _layer_norm(x, g, b):
    mu = jnp.mean(x, axis=-1, keepdims=True)
    xc = x - mu
    var = jnp.mean(xc * xc, axis=-1, keepdims=True)
    return xc * lax.rsqrt(var + LN_EPS) * g + b


def _rms_norm(x, g):
    return x * lax.rsqrt(jnp.mean(x * x, axis=-1, keepdims=True) + RMS_EPS) * g


def _bdot(a, b):
    return jnp.dot(a.astype(BF16), b, preferred_element_type=F32)


def _ada_kernel(c_ref, w_ref, b_ref, o_ref):
    c = c_ref[...]
    s = c * jax.nn.sigmoid(c)
    o_ref[...] = jnp.dot(s, w_ref[0], precision=HIGHEST, preferred_element_type=F32) + b_ref[0]


def ada_mod(cvec, ws, bs, layer):
    d = cvec.shape[1]
    depth, _, n = ws.shape
    tn = 1024
    return pl.pallas_call(
        _ada_kernel,
        grid=(n // tn,),
        in_specs=[_full((SUBLANES, d)), pl.BlockSpec((1, d, tn), lambda j: (layer, 0, j)),
                  pl.BlockSpec((1, 1, tn), lambda j: (layer, 0, j))],
        out_specs=pl.BlockSpec((SUBLANES, tn), lambda j: (0, j)),
        out_shape=jax.ShapeDtypeStruct((SUBLANES, n), F32),
        compiler_params=_cparams("arbitrary"),
        name="ada_mod",
    )(cvec, ws, bs.reshape(depth, 1, n))


def _s5_kernel(xf_ref, xb_ref, mod_ref, bf_ref, cc_ref, ar_ref, ai_ref, yf_ref, yb_ref, bu_ref, x_ref,
               *, chunk, nbatch):
    nlc = bf_ref.shape[1]
    spc = bf_ref.shape[3] // (2 * LANES)

    @pl.when(pl.program_id(0) == 0)
    def _():
        x_ref[...] = jnp.zeros_like(x_ref)

    for d, src in enumerate((xf_ref, xb_ref)):
        u = jnp.concatenate([(src[b] * (1.0 + mod_ref[0, b, 1:2, :]) + mod_ref[0, b, 0:1, :]).astype(BF16)
                             for b in range(nbatch)], axis=0)
        for j in range(nlc):
            res = jnp.dot(u[:, LANES * j:LANES * (j + 1)], bf_ref[d, j], preferred_element_type=F32)
            for b in range(nbatch):
                for part in range(2):
                    r = d * nbatch + b + 2 * nbatch * part
                    for k in range(spc):
                        col = (part * spc + k) * LANES
                        bu_ref[spc * j + k, pl.ds(r, chunk, stride=SUBLANES), :] = (
                            res[b * chunk:(b + 1) * chunk, col:col + LANES])

    nslab = x_ref.shape[0]
    gs = nslab // S5_SLAB_GROUPS
    row = lax.broadcasted_iota(jnp.int32, (gs,) + x_ref.shape[1:], 1)
    is_bwd = (row % (2 * nbatch)) >= nbatch
    for g0 in range(0, nslab, gs):
        ar = ar_ref[g0:g0 + gs]
        ai = ai_ref[g0:g0 + gs]

        def step(s, x, g0=g0, ar=ar, ai=ai):
            off_f = pl.multiple_of(s * SUBLANES, SUBLANES)
            off_b = pl.multiple_of((chunk - 1 - s) * SUBLANES, SUBLANES)
            slab_f = bu_ref[g0:g0 + gs, pl.ds(off_f, SUBLANES), :]
            slab_b = bu_ref[g0:g0 + gs, pl.ds(off_b, SUBLANES), :]
            xn = ar * x + ai * pltpu.roll(x, SUBLANES // 2, axis=1) + jnp.where(is_bwd, slab_b, slab_f)
            bu_ref[g0:g0 + gs, pl.ds(off_f, SUBLANES), :] = jnp.where(is_bwd, slab_f, xn)
            bu_ref[g0:g0 + gs, pl.ds(off_b, SUBLANES), :] = jnp.where(is_bwd, xn, slab_b)
            return xn

        def steps(i, x, step=step):
            for j in range(S5_STEP_UNROLL):
                x = step(i * S5_STEP_UNROLL + j, x)
            return x

        x_ref[g0:g0 + gs] = lax.fori_loop(0, chunk // S5_STEP_UNROLL, steps, x_ref[g0:g0 + gs])

    def state_rows(r, j):
        return jnp.concatenate([bu_ref[spc * j + k, pl.ds(r, chunk, stride=SUBLANES), :] for k in range(spc)],
                               axis=1).astype(BF16)

    for d, out in enumerate((yf_ref, yb_ref)):
        for j in range(nlc):
            h = jnp.concatenate(
                [jnp.concatenate([state_rows(d * nbatch + b, j), state_rows(d * nbatch + b + 2 * nbatch, j)], axis=1)
                 for b in range(nbatch)], axis=0)
            y = jnp.dot(h, cc_ref[d, j], preferred_element_type=F32)
            for b in range(nbatch):
                out[b, :, LANES * j:LANES * (j + 1)] = y[b * chunk:(b + 1) * chunk]


def s5_scan(x, mod_seg, bfc, ccc, ar, ai, ctx_len):
    nb, ltot, d = x.shape
    assert 2 * nb * 2 == SUBLANES, "state rows must fill one sublane tile"
    chunk = S5_CHUNK
    assert ltot % chunk == 0 and ctx_len % chunk == 0
    ctx_chunks = ctx_len // chunk
    nchunks = ltot // chunk
    nslab = ar.shape[0]
    fwd_spec = pl.BlockSpec((nb, chunk, d), lambda c: (0, c, 0))
    bwd_spec = pl.BlockSpec((nb, chunk, d), lambda c: (0, (nchunks - 1 - c + ctx_chunks) % nchunks, 0))
    return pl.pallas_call(
        functools.partial(_s5_kernel, chunk=chunk, nbatch=nb),
        grid=(nchunks,),
        in_specs=[fwd_spec, bwd_spec,
                  pl.BlockSpec((1, nb, N_MOD, d), lambda c: (jnp.minimum(c // ctx_chunks, 1), 0, 0, 0)),
                  _full(bfc.shape), _full(ccc.shape), _full(ar.shape), _full(ai.shape)],
        out_specs=[fwd_spec, bwd_spec],
        out_shape=[jax.ShapeDtypeStruct(x.shape, F32)] * 2,
        scratch_shapes=[pltpu.VMEM((nslab, SUBLANES * chunk, LANES), F32),
                        pltpu.VMEM((nslab, SUBLANES, LANES), F32)],
        compiler_params=_cparams("arbitrary"),
        name="s5_scan",
    )(x, x, mod_seg, bfc, ccc, ar, ai)


def s5_prepare(a_re, a_im, log_dt, b_re, b_im, c_re, c_im, nbatch):
    ndir, g, p = a_re.shape
    gc = b_re.shape[-1]
    gpc = LANES // gc
    nlc = g // gpc
    dt = jnp.exp(log_dt.astype(F32))[..., None]
    a_re = a_re.astype(F32)
    a_im = a_im.astype(F32)
    mag = jnp.exp(a_re * dt)
    ab_re = mag * jnp.cos(a_im * dt)
    ab_im = mag * jnp.sin(a_im * dt)
    nr, ni = ab_re - 1.0, ab_im
    den = a_re * a_re + a_im * a_im
    f_re = (nr * a_re + ni * a_im) / den
    f_im = (ni * a_re - nr * a_im) / den
    bf_re = f_re[..., None] * b_re - f_im[..., None] * b_im
    bf_im = f_re[..., None] * b_im + f_im[..., None] * b_re
    bf = jnp.stack([bf_re, bf_im], axis=1).reshape(ndir, 2, nlc, gpc, p, gc)
    eye = jnp.eye(gpc, dtype=F32)
    bfc = jnp.einsum('dqjgpc,gh->djgcqhp', bf, eye).reshape(ndir, nlc, gpc * gc, 2 * gpc * p).astype(BF16)
    cc = jnp.stack([c_re, -c_im], axis=1).astype(F32).reshape(ndir, 2, nlc, gpc, gc, p)
    ccc = jnp.einsum('dqjgcp,gh->djqgphc', cc, eye).reshape(ndir, nlc, 2 * gpc * p, gpc * gc).astype(BF16)
    rows_r, rows_i = [], []
    for part in range(2):
        for d in range(ndir):
            for _ in range(nbatch):
                rows_r.append(ab_re[d].reshape(-1))
                rows_i.append(ab_im[d].reshape(-1) * (-1.0 if part == 0 else 1.0))
    slabs = lambda rows: jnp.stack(rows).reshape(len(rows), -1, LANES).transpose(1, 0, 2)
    return bfc, ccc, slabs(rows_r), slabs(rows_i)


def _post_kernel(*refs, alpha, glu):
    if glu:
        h_ref, yf_ref, yb_ref, mod_ref, dsk_ref, wg_ref, wo_ref, lng_ref, lnb_ref, h1_ref, t_ref = refs
        h = h_ref[...]
        m = h * (1.0 + mod_ref[0, 1:2, :]) + mod_ref[0, 0:1, :]
        z = _gelu(yf_ref[...] + yb_ref[...] + dsk_ref[...] * m)
        z = z * jax.nn.sigmoid(_bdot(z, wg_ref[...]))
    else:
        h_ref, z_ref, mod_ref, wo_ref, lng_ref, lnb_ref, h1_ref, t_ref = refs
        h = h_ref[...]
        z = z_ref[...]
    o = _bdot(z, wo_ref[...])
    h1 = _layer_norm(alpha * h + mod_ref[0, 2:3, :] * o, lng_ref[...], lnb_ref[...])
    h1_ref[...] = h1
    t_ref[...] = h1 * (1.0 + mod_ref[0, 4:5, :]) + mod_ref[0, 3:4, :]


def post_mixer(h, pre, mod_tiles, w_o, ln_g, ln_b, alpha, d_skip=None, w_glu=None):
    n, d = h.shape
    tm = ROW_TILE
    glu = w_glu is not None
    row = pl.BlockSpec((tm, d), lambda i: (i, 0))
    vec = _full((1, d))
    modspec = pl.BlockSpec((1, N_MOD, d), lambda i: (i, 0, 0))
    if glu:
        args = (h, pre[0], pre[1], mod_tiles, d_skip.reshape(1, d), w_glu.astype(BF16), w_o.astype(BF16),
                ln_g.reshape(1, d), ln_b.reshape(1, d))
        in_specs = [row, row, row, modspec, vec, _full((d, d)), _full((d, d)), vec, vec]
    else:
        args = (h, pre[0], mod_tiles, w_o.astype(BF16), ln_g.reshape(1, d), ln_b.reshape(1, d))
        in_specs = [row, pl.BlockSpec((tm, pre[0].shape[1]), lambda i: (i, 0)), modspec,
                    _full(w_o.shape), vec, vec]
    return pl.pallas_call(
        functools.partial(_post_kernel, alpha=alpha, glu=glu),
        grid=(n // tm,),
        in_specs=in_specs,
        out_specs=[row, row],
        out_shape=[jax.ShapeDtypeStruct((n, d), F32)] * 2,
        compiler_params=_cparams("arbitrary"),
        name="post_mixer_glu" if glu else "post_mixer",
    )(*args)


def _res_ln_kernel(h_ref, f_ref, mod_ref, lng_ref, lnb_ref, o_ref, *, alpha):
    o_ref[...] = _layer_norm(alpha * h_ref[...] + mod_ref[0, 5:6, :] * f_ref[...], lng_ref[...], lnb_ref[...])


def residual_ln(h, f, mod_tiles, ln_g, ln_b, alpha):
    n, d = h.shape
    tm = ROW_TILE
    row = pl.BlockSpec((tm, d), lambda i: (i, 0))
    vec = _full((1, d))
    return pl.pallas_call(
        functools.partial(_res_ln_kernel, alpha=alpha),
        grid=(n // tm,),
        in_specs=[row, row, pl.BlockSpec((1, N_MOD, d), lambda i: (i, 0, 0)), vec, vec],
        out_specs=row,
        out_shape=jax.ShapeDtypeStruct((n, d), F32),
        compiler_params=_cparams("arbitrary"),
        name="residual_ln",
    )(h, f, mod_tiles, ln_g.reshape(1, d), ln_b.reshape(1, d))


def _topk_rows(s, k, payload=None):
    nrow = s.shape[0]
    iota = lax.broadcasted_iota(jnp.int32, s.shape, 0).astype(F32)
    vals, idxs, pays = [], [], []
    for _ in range(k):
        m = jnp.max(s, axis=0, keepdims=True)
        i = jnp.min(jnp.where(s == m, iota, float(nrow)), axis=0, keepdims=True)
        hit = iota == i
        vals.append(m)
        idxs.append(i)
        if payload is not None:
            pays.append(jnp.max(jnp.where(hit, payload, -1.0), axis=0, keepdims=True))
        s = jnp.where(hit, -jnp.inf, s)
    return vals, idxs, pays


def _split2(x):
    hi = x.astype(BF16)
    return hi, (x - hi.astype(F32)).astype(BF16)


def _dot3(a_hi, a_lo, b_hi, b_lo, dims):
    dg = functools.partial(lax.dot_general, dimension_numbers=dims, preferred_element_type=F32)
    return dg(a_hi, b_hi) + dg(a_hi, b_lo) + dg(a_lo, b_hi)


def _route_kernel(t_ref, wqt_hi_ref, wqt_lo_ref, keys_hi_ref, keys_lo_ref, e_ref, g_ref):
    nt = (((1,), (1,)), ((), ()))
    nn = (((1,), (0,)), ((), ()))
    qt = _dot3(wqt_hi_ref[...], wqt_lo_ref[...], *_split2(t_ref[...]), nt)
    half = PEER_DK // 2
    gates, ids = [], []
    for h in range(PEER_HEADS):
        sv, si = [], []
        for s in range(2):
            row0 = (h * 2 + s) * half
            sc = _dot3(keys_hi_ref[h * 2 + s], keys_lo_ref[h * 2 + s], *_split2(qt[row0:row0 + half, :]), nn)
            v, i, _ = _topk_rows(sc, PEER_TOPK)
            sv.append(v)
            si.append(i)
        v2 = jnp.concatenate(sv[1], axis=0)
        i2 = jnp.concatenate(si[1], axis=0)
        width = [PEER_TOPK // (a + 1) for a in range(PEER_TOPK)]
        npad = -sum(width) % SUBLANES
        tm = v2.shape[1]
        cand = jnp.concatenate([sv[0][a] + v2[:width[a]] for a in range(PEER_TOPK)]
                               + [jnp.full((npad, tm), -jnp.inf, F32)], axis=0)
        cexp = jnp.concatenate([si[0][a] * float(PEER_NKEYS) + i2[:width[a]] for a in range(PEER_TOPK)]
                               + [jnp.full((npad, tm), -1.0, F32)], axis=0)
        cv, _, ce = _topk_rows(cand, PEER_TOPK, payload=cexp)
        cvm = jnp.concatenate(cv, axis=0)
        ex = jnp.exp(cvm - cv[0])
        gates.append(ex / jnp.sum(ex, axis=0, keepdims=True))
        ids.extend(ce)
    g_ref[...] = jnp.concatenate(gates, axis=0).T
    e_ref[...] = jnp.concatenate(ids, axis=0).T.astype(jnp.int32)


def peer_route(t, w_q, keys):
    n, d = t.shape
    tm = ROW_TILE
    hk = PEER_HEADS * PEER_TOPK
    keys2 = keys.reshape(PEER_HEADS * 2, PEER_NKEYS, PEER_DK // 2)
    wqt = w_q.T
    out = pl.BlockSpec((tm, hk), lambda i: (i, 0))
    return pl.pallas_call(
        _route_kernel,
        grid=(n // tm,),
        in_specs=[pl.BlockSpec((tm, d), lambda i: (i, 0)), _full(wqt.shape), _full(wqt.shape),
                  _full(keys2.shape), _full(keys2.shape)],
        out_specs=[out, out],
        out_shape=[jax.ShapeDtypeStruct((n, hk), jnp.int32), jax.ShapeDtypeStruct((n, hk), F32)],
        compiler_params=_cparams("arbitrary"),
        name="peer_route",
    )(t, *_split2(wqt), *_split2(keys2))


GATHER_STRIDE = PEER_HEADS * PEER_TOPK + SUBLANES


PEER_PIECE_ROWS = 4
PEER_GROUP = 4
PEER_GATHER_PARTS = 8
PEER_ANCHOR_LAG = 4


def _split3(x):
    hi = x.astype(BF16)
    r1 = x - hi.astype(F32)
    mid = r1.astype(BF16)
    lo = (r1 - mid.astype(F32)).astype(BF16)
    return hi, mid, lo


def _gather_rows(idx_ref, tab_ref, g_ref, t, k0, k1, after=None):
    rp = PEER_PIECE_ROWS
    ids = idx_ref.at[t]
    for k in range(k0, k1):
        slab = tab_ref[ids[k]]
        for d in range(tab_ref.shape[1] // rp):
            piece = slab[rp * d:rp * (d + 1), :]
            g_ref[pl.ds(rp * (d * GATHER_STRIDE + k), rp), :] = piece if after is None else piece + after


def _zero_after(x):
    bits = pltpu.bitcast(x[:PEER_PIECE_ROWS, :LANES], jnp.uint32)
    zero = pltpu.bitcast(lax.shift_right_logical(bits, jnp.full_like(bits, 32)), jnp.int32)
    return zero.astype(F32).astype(BF16)


def _chunk_matrix(g_ref, d, nk):
    rp = PEER_PIECE_ROWS
    return g_ref[rp * GATHER_STRIDE * d:rp * GATHER_STRIDE * d + rp * nk, :]


def _token_pipeline(idx_ref, idx_next_ref, tab_ref, tm, nk, prepare, compute_chunk, finish, anchor, tiles):
    nchunk = tab_ref.shape[1] // PEER_PIECE_ROWS
    grp = PEER_GROUP
    per = nk // PEER_GATHER_PARTS
    set_a, set_b = tiles[:grp], tiles[grp:]
    ngroups = tm // grp
    assert ngroups % 2 == 0, "every token block must start on tile set A"

    @pl.when(pl.program_id(0) == 0)
    def _():
        for i in range(grp):
            _gather_rows(idx_ref, tab_ref, set_a[i], i, 0, nk)

    def group(t0, cur, nxt, ids_ref, t_ids):
        dots = [(i, c) for i in range(grp) for c in range(nchunk)]
        parts = [(i, j) for i in range(grp) for j in range(PEER_GATHER_PARTS)]
        acc = [prepare(t0 + i) for i in range(grp)]
        done = []
        for n in range(max(len(dots), len(parts))):
            if n < len(dots):
                i, c = dots[n]
                acc[i] = compute_chunk(t0 + i, c, _chunk_matrix(cur[i], c, nk), acc[i])
                done.append(anchor(acc[i]))
                if c == nchunk - 1:
                    finish(t0 + i, acc[i])
            if n < len(parts):
                i, j = parts[n]
                m = min(n - PEER_ANCHOR_LAG, len(dots) - 1)
                after = _zero_after(done[m]) if m >= 0 and (m < len(dots) - 1 or n == len(parts) - 1) else None
                _gather_rows(ids_ref, tab_ref, nxt[i], t_ids + i, j * per, (j + 1) * per, after)

    def step(g, carry):
        @pl.when(g % 2 == 0)
        def _():
            group(g * grp, set_a, set_b, idx_ref, (g + 1) * grp)

        @pl.when(g % 2 == 1)
        def _():
            group(g * grp, set_b, set_a, idx_ref, (g + 1) * grp)

        return carry

    lax.fori_loop(0, ngroups - 1, step, 0)
    group((ngroups - 1) * grp, set_b, set_a, idx_next_ref, 0)


def _peer_u_kernel(idx_ref, idx_next_ref, tab_ref, th_ref, gate_ref, mask_ref, pool_ref, o_ref, s_ref, *tiles,
                   tm, nk):
    nchunk = tab_ref.shape[1] // 2
    nt = (((1,), (1,)), ((), ()))

    def prepare(t):
        x = th_ref[t]
        hi = x.astype(BF16).astype(F32)
        lhs = jnp.concatenate([hi, x - hi], axis=0).astype(BF16)
        return lhs, jnp.zeros((SUBLANES, PEER_PIECE_ROWS * nk), F32)

    def compute_chunk(t, c, w, acc):
        lhs, s = acc
        out = lax.dot_general(lhs, w, nt, preferred_element_type=F32)
        return lhs, s + (out[:SUBLANES] + out[SUBLANES:]) * mask_ref[c]

    def finish(t, acc):
        s_ref[pl.ds(t, 1), :] = jnp.sum(acc[1], axis=0, keepdims=True)

    _token_pipeline(idx_ref, idx_next_ref, tab_ref, tm, nk, prepare, compute_chunk, finish,
                    lambda acc: acc[1][:nchunk, :LANES], tiles)
    act = jnp.dot(s_ref[...], pool_ref[...], precision=HIGHEST, preferred_element_type=F32)
    o_ref[...] = _gelu(act) * gate_ref[...]


def _peer_v_kernel(idx_ref, idx_next_ref, tab_ref, a_ref, e_ref, o_ref, x_ref, *tiles, tm, nk):
    nchunk = tab_ref.shape[1] // 2
    rp = PEER_PIECE_ROWS
    nrow = 3 * rp
    for j, p in enumerate(_split3(a_ref[...])):
        for q in range(rp):
            x_ref[3 * q + j] = jnp.dot(p, e_ref[q], preferred_element_type=F32)

    def prepare(t):
        pad = -nrow % SUBLANES
        rows = [x_ref[r, pl.ds(t, 1), :] for r in range(nrow)] + [jnp.zeros((pad, rp * nk), F32)] * (pad > 0)
        return jnp.concatenate(rows, axis=0).astype(BF16), None

    def compute_chunk(t, d, w, state):
        lhs = state[0]
        out = jnp.dot(lhs, w, preferred_element_type=F32)
        for q in range(rp):
            feat_row = (q % 2) * nchunk + d * (rp // 2) + q // 2
            o_ref[t, pl.ds(feat_row, 1), :] = out[3 * q:3 * q + 1] + out[3 * q + 1:3 * q + 2] + out[3 * q + 2:3 * q + 3]
        return lhs, out

    _token_pipeline(idx_ref, idx_next_ref, tab_ref, tm, nk, prepare, compute_chunk, lambda t, state: None,
                    lambda state: state[1][:nchunk], tiles)


def _pack_table(tabs, layer):
    _, e, d = tabs.shape
    nchunk = d // (2 * LANES)
    t = tabs[layer].astype(BF16).reshape(e, 2, nchunk, LANES)
    return t.transpose(0, 2, 1, 3).reshape(e, 2 * nchunk, LANES)


def peer_experts(idx, gate, th, u_tabs, v_tabs, layer):
    n, nk = idx.shape
    tm = PEER_TOK_TILE
    nchunk = th.shape[1] // 2
    up, vp = _pack_table(u_tabs, layer), _pack_table(v_tabs, layer)
    rp = PEER_PIECE_ROWS
    ndot = 2 * nchunk // rp
    rho = np.arange(rp * nk)
    mask = np.zeros((ndot, SUBLANES, rp * nk), np.float32)
    for d in range(ndot):
        for q in range(rp):
            mask[d, (q % 2) * nchunk + d * (rp // 2) + q // 2, rho % rp == q] = 1.0
    pool = (rho[:, None] // rp == np.arange(nk)[None, :]).astype(np.float32)
    expand = np.stack([(np.arange(nk)[:, None] * rp + q == rho[None, :]) for q in range(rp)]).astype(np.float32)
    smem = pl.BlockSpec((tm, nk), lambda i: (i, 0), memory_space=pltpu.SMEM)
    assert PEER_GROUP <= SUBLANES
    smem_next = pl.BlockSpec((SUBLANES, nk), lambda i: (jnp.minimum((i + 1) * (tm // SUBLANES), n // SUBLANES - 1), 0),
                             memory_space=pltpu.SMEM)
    tab_spec = pl.BlockSpec(up.shape, lambda i: (0, 0, 0), pipeline_mode=pl.Buffered(1))
    row = pl.BlockSpec((tm, nk), lambda i: (i, 0))
    th_spec = pl.BlockSpec((tm,) + th.shape[1:], lambda i: (i, 0, 0))
    gather_tiles = [pltpu.VMEM((rp * GATHER_STRIDE * ndot, LANES), BF16)] * (2 * PEER_GROUP)
    act = pl.pallas_call(
        functools.partial(_peer_u_kernel, tm=tm, nk=nk),
        grid=(n // tm,),
        in_specs=[smem, smem_next, tab_spec, th_spec, row, _full(mask.shape), _full(pool.shape)],
        out_specs=row,
        out_shape=jax.ShapeDtypeStruct((n, nk), F32),
        scratch_shapes=[pltpu.VMEM((tm, rp * nk), F32)] + gather_tiles,
        compiler_params=_cparams("arbitrary"),
        name="peer_u",
    )(idx, idx, up, th, gate, jnp.asarray(mask), jnp.asarray(pool))
    return pl.pallas_call(
        functools.partial(_peer_v_kernel, tm=tm, nk=nk),
        grid=(n // tm,),
        in_specs=[smem, smem_next, tab_spec, row, _full(expand.shape)],
        out_specs=th_spec,
        out_shape=jax.ShapeDtypeStruct(th.shape, F32),
        scratch_shapes=[pltpu.VMEM((3 * rp, tm, rp * nk), F32)] + gather_tiles,
        compiler_params=_cparams("arbitrary"),
        name="peer_v",
    )(idx, idx, vp, act, jnp.asarray(expand, BF16))


def peer_channel(t, w_q, keys, u_tabs, v_tabs, layer):
    n, d = t.shape
    e, g = peer_route(t, w_q, keys)
    f = peer_experts(e, g, t.reshape(n, d // LANES, LANES), u_tabs, v_tabs, layer)
    return f.reshape(n, d)


def _mla_proj_kernel(h_ref, mod_ref, ck_ref, sk_ref, cq_ref, sq_ref, wdc_ref, wdr_ref, wdrs_ref, kvn_ref,
                     wuk_ref, wuv_ref, wdq_ref, qn_ref, wqn_ref, wqr_ref, wqrs_ref, k_ref, v_ref, q_ref, *, scale):
    m = (h_ref[...] * (1.0 + mod_ref[0, 1:2, :]) + mod_ref[0, 0:1, :]).astype(BF16)
    ckv = _rms_norm(jnp.dot(m, wdc_ref[...], preferred_element_type=F32), kvn_ref[...]).astype(BF16)
    kr = (jnp.dot(m, wdr_ref[...], preferred_element_type=F32) * ck_ref[...]
          + jnp.dot(m, wdrs_ref[...], preferred_element_type=F32) * sk_ref[...]).astype(BF16)
    kn = jnp.dot(ckv, wuk_ref[...], preferred_element_type=F32).astype(BF16)
    vt = lax.dot_general(wuv_ref[...], ckv, (((1,), (1,)), ((), ())),
                         preferred_element_type=F32).astype(BF16)
    cq = _rms_norm(jnp.dot(m, wdq_ref[...], preferred_element_type=F32), qn_ref[...]).astype(BF16)
    qn = (jnp.dot(cq, wqn_ref[...], preferred_element_type=F32) * scale).astype(BF16)
    qr = ((jnp.dot(cq, wqr_ref[...], preferred_element_type=F32) * cq_ref[...]
           + jnp.dot(cq, wqrs_ref[...], preferred_element_type=F32) * sq_ref[...]) * scale).astype(BF16)
    for h in range(MLA_HEADS):
        k_ref[0, h, :, 0:MLA_NOPE] = kn[:, h * MLA_NOPE:(h + 1) * MLA_NOPE]
        k_ref[0, h, :, MLA_NOPE:] = kr
        v_ref[0, h, 0:MLA_V, :] = vt[h * MLA_V:(h + 1) * MLA_V, :]
        v_ref[0, h, MLA_V:, :] = (lax.broadcasted_iota(jnp.int32, (V_PAD_ROWS, vt.shape[1]), 0) == 0).astype(BF16)
        q_ref[0, h, :, 0:MLA_NOPE] = qn[:, h * MLA_NOPE:(h + 1) * MLA_NOPE]
        q_ref[0, h, :, MLA_NOPE:] = qr[:, h * MLA_ROPE:(h + 1) * MLA_ROPE]


def _swap_halves(w, width):
    r = w.reshape(w.shape[0], -1, 2, width // 2)
    return r[:, :, ::-1, :].reshape(w.shape)


def mla_project(x, mod_tiles, ctx_len, nbatch, w_dq, q_norm, w_uq, w_dkv, kv_norm, w_ukv):
    n, d = x.shape
    ltot = n // nbatch
    nlat = ltot - ctx_len
    tm = ROW_TILE
    tpb = ltot // tm
    hd = MLA_NOPE + MLA_ROPE
    kvl = kv_norm.shape[0]
    ql = q_norm.shape[0]
    pos = jnp.arange(nlat, dtype=jnp.int32)
    nf = MLA_ROPE // 4
    inv = ROPE_BASE ** (-jnp.arange(nf, dtype=F32) / nf)
    ang = jnp.concatenate([(pos // GRID_W).astype(F32)[:, None] * inv,
                           (pos % GRID_W).astype(F32)[:, None] * inv], axis=-1)
    cos = jnp.concatenate([jnp.ones((ctx_len, MLA_ROPE // 2), F32), jnp.cos(ang)], axis=0)
    sin = jnp.concatenate([jnp.zeros((ctx_len, MLA_ROPE // 2), F32), jnp.sin(ang)], axis=0)
    ck = jnp.concatenate([cos, cos], axis=1)
    sk = jnp.concatenate([-sin, sin], axis=1)
    cq = jnp.tile(ck, (1, MLA_HEADS))
    sq = jnp.tile(sk, (1, MLA_HEADS))
    w_dkv_c = w_dkv[:, :kvl].astype(BF16)
    w_dkv_r = w_dkv[:, kvl:]
    ukv = w_ukv.reshape(kvl, MLA_HEADS, MLA_NOPE + MLA_V)
    w_uk = ukv[:, :, :MLA_NOPE].reshape(kvl, -1).astype(BF16)
    w_uv = ukv[:, :, MLA_NOPE:].reshape(kvl, -1).T.astype(BF16)
    uq = w_uq.reshape(ql, MLA_HEADS, hd)
    w_qn = uq[:, :, :MLA_NOPE].reshape(ql, -1).astype(BF16)
    w_qr = uq[:, :, MLA_NOPE:].reshape(ql, -1)
    args = (x, mod_tiles, ck, sk, cq, sq, w_dkv_c, w_dkv_r.astype(BF16),
            _swap_halves(w_dkv_r, MLA_ROPE).astype(BF16), kv_norm.reshape(1, kvl), w_uk, w_uv,
            w_dq.astype(BF16), q_norm.reshape(1, ql), w_qn, w_qr.astype(BF16),
            _swap_halves(w_qr, MLA_ROPE).astype(BF16))
    pos_spec = lambda w: pl.BlockSpec((tm, w), lambda i: (i % tpb, 0))
    in_specs = [pl.BlockSpec((tm, d), lambda i: (i, 0)), pl.BlockSpec((1, N_MOD, d), lambda i: (i, 0, 0)),
                pos_spec(MLA_ROPE), pos_spec(MLA_ROPE), pos_spec(MLA_ROPE * MLA_HEADS),
                pos_spec(MLA_ROPE * MLA_HEADS)] + [_full(a.shape) for a in args[6:]]
    head_spec = lambda w: pl.BlockSpec((1, MLA_HEADS, tm, w), lambda i: (i // tpb, 0, i % tpb, 0))
    return pl.pallas_call(
        functools.partial(_mla_proj_kernel, scale=float(hd) ** -0.5 * math.log2(math.e)),
        grid=(n // tm,),
        in_specs=in_specs,
        out_specs=[head_spec(hd),
                   pl.BlockSpec((1, MLA_HEADS, MLA_V + V_PAD_ROWS, tm), lambda i: (i // tpb, 0, 0, i % tpb)),
                   head_spec(hd)],
        out_shape=[jax.ShapeDtypeStruct((nbatch, MLA_HEADS, ltot, hd), BF16),
                   jax.ShapeDtypeStruct((nbatch, MLA_HEADS, MLA_V + V_PAD_ROWS, ltot), BF16),
                   jax.ShapeDtypeStruct((nbatch, MLA_HEADS, ltot, hd), BF16)],
        compiler_params=_cparams("arbitrary"),
        name="mla_project",
    )(*args)


def _attn_kernel(q_ref, k_ref, vt_ref, o_ref, *, tk, nchunks):
    q = q_ref[0, 0]
    tq = q.shape[0]
    nt = (((1,), (1,)), ((), ()))

    def scores(c):
        return lax.dot_general(k_ref[0, 0, c * tk:(c + 1) * tk, :], q, nt, preferred_element_type=F32)

    m_i = jnp.full((1, tq), -jnp.inf, F32)
    acc = jnp.zeros((vt_ref.shape[2], tq), F32)
    s = scores(0)
    for c in range(nchunks):
        s_next = scores(c + 1) if c + 1 < nchunks else None
        m_new = jnp.maximum(m_i, jnp.max(s, axis=0, keepdims=True))
        p = jnp.exp2(s - m_new).astype(BF16)
        acc = jnp.exp2(m_i - m_new) * acc + jnp.dot(vt_ref[0, 0, :, c * tk:(c + 1) * tk], p,
                                                    preferred_element_type=F32)
        m_i, s = m_new, s_next
    o_ref[0] = (acc[:MLA_V] / acc[MLA_V:MLA_V + 1]).T


def mla_attend(q, k, vt, ctx_len):
    nb, nh, ltot, hd = k.shape
    tq = ROW_TILE
    assert ctx_len % tq == 0
    skip = ctx_len // tq
    nlat = ltot - ctx_len
    tk = ATT_KV_CHUNK
    assert ltot % tk == 0
    return pl.pallas_call(
        functools.partial(_attn_kernel, tk=tk, nchunks=ltot // tk),
        grid=(nb, nh, nlat // tq),
        in_specs=[pl.BlockSpec((1, 1, tq, hd), lambda b, h, i: (b, h, i + skip, 0)),
                  pl.BlockSpec((1, 1, ltot, hd), lambda b, h, i: (b, h, 0, 0)),
                  pl.BlockSpec((1, 1, MLA_V + V_PAD_ROWS, ltot), lambda b, h, i: (b, h, 0, 0))],
        out_specs=pl.BlockSpec((1, tq, MLA_V), lambda b, h, i: (b, i, h)),
        out_shape=jax.ShapeDtypeStruct((nb, nlat, nh * MLA_V), F32),
        compiler_params=_cparams("arbitrary", "arbitrary", "arbitrary"),
        name="mla_attend",
    )(q, k, vt)


def kernel(x, c, ctx, c_ctx, ada_w, ada_b, ln_g, ln_b, s5_a_re, s5_a_im, s5_log_dt, s5_b_re, s5_b_im, s5_c_re, s5_c_im, s5_d, s5_w_glu, s5_w_o, mla_w_dq, mla_q_norm, mla_w_uq, mla_w_dkv, mla_kv_norm, mla_w_ukv, mla_w_o, peer_w_q, peer_keys, peer_u, peer_v):
    nb, nlat, d = x.shape
    nctx = ctx.shape[1]
    ltot = nctx + nlat
    depth = ada_w.shape[0]
    alpha = (2 * depth) ** 0.25
    tm = ROW_TILE

    cvec = jnp.zeros((SUBLANES, d), F32).at[:nb].set(c).at[nb].set(c_ctx)
    tiles_per_batch = ltot // tm
    tile_rows_all = np.array([nb if j < nctx // tm else b for b in range(nb) for j in range(tiles_per_batch)])
    tile_rows_lat = np.array([b for b in range(nb) for _ in range(nlat // tm)])
    seg_rows = np.array([[nb] * nb, list(range(nb))])

    h_all = jnp.concatenate([ctx, x], axis=1).reshape(nb * ltot, d)

    mod = ada_mod(cvec, ada_w, ada_b, 0).reshape(SUBLANES, N_MOD, d)
    mod_all = mod[tile_rows_all]
    bfc, ccc, ar, ai = s5_prepare(s5_a_re[0], s5_a_im[0], s5_log_dt[0], s5_b_re[0], s5_b_im[0],
                                  s5_c_re[0], s5_c_im[0], nb)
    yf, yb = s5_scan(h_all.reshape(nb, ltot, d), mod[seg_rows], bfc, ccc, ar, ai, nctx)
    h1, t = post_mixer(h_all, (yf.reshape(-1, d), yb.reshape(-1, d)), mod_all, s5_w_o[0], ln_g[0, 0], ln_b[0, 0],
                       alpha, d_skip=s5_d[0], w_glu=s5_w_glu[0])
    f = peer_channel(t, peer_w_q[0], peer_keys[0], peer_u, peer_v, 0)
    h_all = residual_ln(h1, f, mod_all, ln_g[0, 1], ln_b[0, 1], alpha)

    mod = ada_mod(cvec, ada_w, ada_b, 1).reshape(SUBLANES, N_MOD, d)
    k, v, q = mla_project(h_all, mod[tile_rows_all], nctx, nb, mla_w_dq[0], mla_q_norm[0], mla_w_uq[0],
                          mla_w_dkv[0], mla_kv_norm[0], mla_w_ukv[0])
    att = mla_attend(q, k, v, nctx).reshape(nb * nlat, -1)
    h_lat = h_all.reshape(nb, ltot, d)[:, nctx:].reshape(nb * nlat, d)
    mod_lat = mod[tile_rows_lat]
    h1, t = post_mixer(h_lat, (att,), mod_lat, mla_w_o[0], ln_g[1, 0], ln_b[1, 0], alpha)
    f = peer_channel(t, peer_w_q[1], peer_keys[1], peer_u, peer_v, 1)
    out = residual_ln(h1, f, mod_lat, ln_g[1, 1], ln_b[1, 1], alpha)
    return out.reshape(nb, nlat, d).astype(x.dtype)
```
